```python
import math
import jax, jax.numpy as jnp
from jax import lax
import numpy as np

D_MODEL = 1024
BATCH = 16
SEQ = 2048
DEPTH = 2
DEC_BATCH = 128
DEC_SEQ = 1
PAST_LEN = 16384
PAGE_SIZE = 128

HEAD_DIM = 64
N_AB = (DEPTH + 1) // 2
N_CD = DEPTH // 2
A_HEADS = 8
A_KV_HEADS = 2
A_GROUP = A_HEADS // A_KV_HEADS
CMP_BLOCK = 32
SEL_BLOCK = 64
N_SEL = 16
NSA_WINDOW = 512
LRU_WIDTH = D_MODEL // 2
LRU_BLOCKS = 8
LRU_BLOCK_DIM = LRU_WIDTH // LRU_BLOCKS
LRU_CONV = 4
LRU_C = 8.0
C_HEADS = 8
C_KV_HEADS = 2
C_GROUP = C_HEADS // C_KV_HEADS
SWA_WINDOW = 128
S5_WIDTH = D_MODEL // 2
S5_GROUP_CH = 16
S5_GROUPS = S5_WIDTH // S5_GROUP_CH
S5_STATE = 64
D_FF = (8 * D_MODEL // 3) // 128 * 128
FFN_CONV = 3
N_BUCKETS = 32
MAX_DISTANCE = 128
BIAS_HEADS = A_HEADS
DN_ALPHA = (2 * DEPTH) ** 0.25
DN_BETA = (8 * DEPTH) ** -0.25
Q_BLOCK = 128
SEL_Q_BLOCK = 32
LN_EPS = 1e-5
SCALE = HEAD_DIM ** -0.5
NEG = -1e30
FORCE_SCORE = 1e4
A_WIDTH = A_HEADS * HEAD_DIM
A_KV_COLS = 2 * A_KV_HEADS * HEAD_DIM
AB_SPLITS = (A_WIDTH, A_KV_COLS, A_KV_COLS, A_KV_COLS, 3 * A_HEADS, LRU_WIDTH, LRU_WIDTH)
C_WIDTH = C_HEADS * HEAD_DIM
CD_SPLITS = (C_WIDTH, 2 * C_KV_HEADS * HEAD_DIM, S5_WIDTH)
AB_IN = sum(AB_SPLITS)
CD_IN = sum(CD_SPLITS)
AB_MIX = A_WIDTH + LRU_WIDTH
CD_MIX = C_WIDTH + S5_WIDTH
F32 = jnp.float32

kernel_name = 'hybrid_nsa_rglru_swa_s5_decoder_step'


def layer_norm(x, g, b):
    xf = x.astype(F32)
    mu = jnp.mean(xf, axis=-1, keepdims=True)
    var = jnp.mean(jnp.square(xf - mu), axis=-1, keepdims=True)
    return ((xf - mu) * lax.rsqrt(var + LN_EPS) * g.astype(F32) + b.astype(F32)).astype(x.dtype)


def ada_mod(c, w, b):
    m = jax.nn.silu(c) @ w + b
    shift, scale, gate = jnp.split(m, 3, axis=-1)
    return shift[:, None, :], scale[:, None, :], gate[:, None, :]


def split_cols(z, sizes):
    out, off = [], 0
    for s in sizes:
        out.append(z[..., off:off + s])
        off += s
    return out


def causal_dwconv(x, buf, w, b):
    T = x.shape[1]
    xp = jnp.concatenate([buf.astype(x.dtype), x], axis=1)
    y = xp[:, 0:T] * w[0]
    for j in range(1, w.shape[0]):
        y = y + xp[:, j:j + T] * w[j]
    return y + b, xp[:, T:]


def linear_scan(a, u, h0):
    def step(h, au):
        h = au[0] * h + au[1]
        return h, h
    h_last, hs = lax.scan(step, h0, (jnp.swapaxes(a, 0, 1), jnp.swapaxes(u, 0, 1)))
    return jnp.swapaxes(hs, 0, 1), h_last


def t5_bucket(dist):
    n = jnp.maximum(dist, 0)
    max_exact = N_BUCKETS // 2
    nf = jnp.maximum(n, 1).astype(F32)
    large = max_exact + (jnp.log(nf / max_exact) / math.log(MAX_DISTANCE / max_exact) * (N_BUCKETS - max_exact)).astype(jnp.int32)
    return jnp.where(n < max_exact, n, jnp.minimum(large, N_BUCKETS - 1))


def masked_softmax(s, mask, sink=None):
    s = jnp.where(mask, s, NEG)
    m = jnp.max(s, axis=-1, keepdims=True)
    if sink is not None:
        m = jnp.maximum(m, sink)
    e = jnp.where(mask, jnp.exp(s - m), 0.0)
    den = jnp.sum(e, axis=-1, keepdims=True)
    if sink is not None:
        den = den + jnp.exp(sink - m)
    return e / jnp.where(den > 0, den, 1.0)


def query_sweep(fn, qblock, xs):
    T = xs[0].shape[1]
    n = T // qblock
    xb = tuple(jnp.swapaxes(x.reshape(x.shape[0], n, qblock, *x.shape[2:]), 0, 1) for x in xs)
    out = lax.map(lambda a: fn(a[0], *a[1]), (jnp.arange(n, dtype=jnp.int32), xb))
    out = jnp.swapaxes(out, 0, 1)
    return out.reshape(out.shape[0], T, *out.shape[3:])


def window_attn(q, k, v, pos_q, pos_k, window, table, sinks):
    n_kv, n_group = q.shape[2], q.shape[3]
    s = jnp.einsum('btkgd,bskd->btkgs', q, k, preferred_element_type=F32) * SCALE
    dist = pos_q[:, None] - pos_k[None, :]
    mask = (dist >= 0) & (dist < window) & (pos_k[None, :] >= 0)
    bias = table[t5_bucket(dist)].astype(F32).reshape(dist.shape[0], dist.shape[1], n_kv, n_group)
    s = s + jnp.transpose(bias, (0, 2, 3, 1))[None]
    sink = None if sinks is None else sinks.astype(F32).reshape(1, 1, n_kv, n_group, 1)
    p = masked_softmax(s, mask[None, :, None, None, :], sink)
    return jnp.einsum('btkgs,bskd->btkgd', p, v.astype(F32))


def banded_window_attn(q, k, v, window, table, sinks):
    pad = ((0, 0), (window, 0), (0, 0), (0, 0))
    kp, vp = jnp.pad(k, pad), jnp.pad(v, pad)
    span = window + Q_BLOCK
    def blk(i, qi):
        start = i * Q_BLOCK
        ks = lax.dynamic_slice_in_dim(kp, start, span, axis=1)
        vs = lax.dynamic_slice_in_dim(vp, start, span, axis=1)
        pos_q = start + jnp.arange(Q_BLOCK, dtype=jnp.int32)
        pos_k = start - window + jnp.arange(span, dtype=jnp.int32)
        return window_attn(qi, ks, vs, pos_q, pos_k, window, table, sinks)
    return query_sweep(blk, Q_BLOCK, (q,))


def compress_kv(kv, pe, w):
    B, L = kv.shape[0], kv.shape[1]
    blk = kv.reshape(B, L // CMP_BLOCK, CMP_BLOCK, 2, kv.shape[3], HEAD_DIM)
    out = jnp.einsum('bncskd,scde->bnske', blk, w)
    return out + jnp.einsum('scd,scde->se', pe, w)[None, None, :, None, :]


def nsa_compressed(q, pos_q, kcv):
    nc = kcv.shape[1]
    s = jnp.einsum('btkgd,bnkd->btkgn', q, kcv[:, :, 0], preferred_element_type=F32) * SCALE
    end = (jnp.arange(nc, dtype=jnp.int32) + 1) * CMP_BLOCK - 1
    mask = (end[None, :] <= pos_q[:, None])[None, :, None, None, :]
    p = masked_softmax(s, mask)
    return jnp.einsum('btkgn,bnkd->btkgd', p, kcv[:, :, 1].astype(F32)), p


def nsa_select_indices(p_cmp, pos_q, n_blk):
    B, T, KV, G, NC = p_cmp.shape
    ratio = SEL_BLOCK // CMP_BLOCK
    imp = jnp.sum(p_cmp, axis=3).reshape(B, T, KV, NC // ratio, ratio).sum(-1)
    n_cand = max(n_blk, N_SEL)
    imp = jnp.pad(imp, ((0, 0), (0, 0), (0, 0), (0, n_cand - NC // ratio)))
    j = jnp.arange(n_cand, dtype=jnp.int32)[None, :]
    cur = (pos_q // SEL_BLOCK)[:, None]
    forced = (j == 0) | (j == cur) | (j == cur - 1)
    score = jnp.where(forced[None, :, None, :], FORCE_SCORE, jnp.where((j <= cur)[None, :, None, :], imp, NEG))
    _, idx = lax.top_k(score, N_SEL)
    valid = idx <= cur[None, :, :, None]
    return idx, valid


def nsa_selected(q, pos_q, idx, valid, g, table):
    B, T, KV, N = idx.shape
    G = q.shape[3]
    pos_k = idx[..., None] * SEL_BLOCK + jnp.arange(SEL_BLOCK, dtype=jnp.int32)
    dist = pos_q[None, :, None, None, None] - pos_k
    mask = (valid[..., None] & (dist >= 0)).reshape(B, T, KV, 1, N * SEL_BLOCK)
    s = jnp.einsum('btkgd,btknsd->btkgns', q, g[..., 0, :], preferred_element_type=F32) * SCALE
    table_r = table.reshape(N_BUCKETS, KV, G).astype(F32)
    bias = table_r[t5_bucket(dist), jnp.arange(KV)[None, None, :, None, None]]
    s = s + jnp.moveaxis(bias, -1, 3)
    p = masked_softmax(s.reshape(B, T, KV, G, N * SEL_BLOCK), mask)
    v = g[..., 1, :].reshape(B, T, KV, N * SEL_BLOCK, g.shape[-1]).astype(F32)
    return jnp.einsum('btkgm,btkmd->btkgd', p, v)


def rglru(xb, gb, conv_buf, h0, prm):
    B, T, W = xb.shape
    xc, new_buf = causal_dwconv(xb, conv_buf, prm['lru_conv_w'], prm['lru_conv_b'])
    xf = xc.astype(F32)
    xr = xf.reshape(B, T, LRU_BLOCKS, LRU_BLOCK_DIM)
    r = jax.nn.sigmoid(jnp.einsum('btnd,nde->btne', xr, prm['lru_w_r']).reshape(B, T, W) + prm['lru_b_r'])
    i = jax.nn.sigmoid(jnp.einsum('btnd,nde->btne', xr, prm['lru_w_i']).reshape(B, T, W) + prm['lru_b_i'])
    log_a = -LRU_C * r * jax.nn.softplus(-prm['lru_lambda'].astype(F32))
    a = jnp.exp(log_a)
    u = jnp.sqrt(-jnp.expm1(2.0 * log_a)) * (i * xf)
    hs, h_last = linear_scan(a, u, h0.astype(F32))
    return hs * jax.nn.gelu(gb.astype(F32)), new_buf, h_last


def complex_affine_combine(e1, e2):
    a1r, a1i, b1r, b1i = e1
    a2r, a2i, b2r, b2i = e2
    return (a2r * a1r - a2i * a1i, a2r * a1i + a2i * a1r,
            a2r * b1r - a2i * b1i + b2r, a2r * b1i + a2i * b1r + b2i)


def s5_layer(u, h0_re, h0_im, prm):
    B, T, _ = u.shape
    uf = u.astype(F32)
    ug = uf.reshape(B, T, S5_GROUPS, S5_GROUP_CH)
    lr = prm['lam_re'].astype(F32)
    li = prm['lam_im'].astype(F32)
    dt = jnp.exp(prm['log_dt'].astype(F32))[:, None]
    mag = jnp.exp(lr * dt)
    ab_re, ab_im = mag * jnp.cos(li * dt), mag * jnp.sin(li * dt)
    den = lr * lr + li * li
    f_re = ((ab_re - 1.0) * lr + ab_im * li) / den
    f_im = (ab_im * lr - (ab_re - 1.0) * li) / den
    br, bim = prm['b_re'].astype(F32), prm['b_im'].astype(F32)
    bb_re = f_re[..., None] * br - f_im[..., None] * bim
    bb_im = f_re[..., None] * bim + f_im[..., None] * br
    bu_re = jnp.einsum('btgc,gpc->btgp', ug, bb_re)
    bu_im = jnp.einsum('btgc,gpc->btgp', ug, bb_im)
    a_re = jnp.broadcast_to(ab_re, bu_re.shape)
    a_im = jnp.broadcast_to(ab_im, bu_im.shape)
    A_re, A_im, H_re, H_im = lax.associative_scan(complex_affine_combine, (a_re, a_im, bu_re, bu_im), axis=1)
    h0r, h0i = h0_re.astype(F32)[:, None], h0_im.astype(F32)[:, None]
    h_re = H_re + A_re * h0r - A_im * h0i
    h_im = H_im + A_re * h0i + A_im * h0r
    y = jnp.einsum('btgp,gcp->btgc', h_re, prm['c_re'].astype(F32)) - jnp.einsum('btgp,gcp->btgc', h_im, prm['c_im'].astype(F32))
    y = y.reshape(B, T, S5_WIDTH) + prm['d'].astype(F32) * uf
    g = jax.nn.gelu(y)
    out = g * jax.nn.sigmoid(g @ prm['w_glu'].astype(F32) + prm['b_glu'].astype(F32))
    return out, h_re[:, -1], h_im[:, -1]


def ab_project(h, prm):
    q, kvc, kvs, kvw, gates, xb, gb = split_cols(h @ prm['w_in'], AB_SPLITS)
    B, T = h.shape[0], h.shape[1]
    q = q.reshape(B, T, A_KV_HEADS, A_GROUP, HEAD_DIM)
    kvc = kvc.reshape(B, T, 2, A_KV_HEADS, HEAD_DIM)
    kvs = kvs.reshape(B, T, 2, A_KV_HEADS, HEAD_DIM)
    kvw = kvw.reshape(B, T, 2, A_KV_HEADS, HEAD_DIM)
    gates = jax.nn.sigmoid(gates.astype(F32)).reshape(B, T, 3, A_KV_HEADS, A_GROUP, 1)
    return q, kvc, kvs, kvw, gates, xb, gb


def ab_merge(o_cmp, o_slc, o_win, gates, y_lru, w_out, dtype):
    o = gates[:, :, 0] * o_cmp + gates[:, :, 1] * o_slc + gates[:, :, 2] * o_win
    B, T = o.shape[0], o.shape[1]
    return jnp.concatenate([o.reshape(B, T, A_WIDTH), y_lru], axis=-1).astype(dtype) @ w_out


def ab_prompt(h, prm, table):
    q, kvc, kvs, kvw, gates, xb, gb = ab_project(h, prm)
    B, T = h.shape[0], h.shape[1]
    pos_q = jnp.arange(T, dtype=jnp.int32)
    kcv = compress_kv(kvc, prm['cmp_pe'], prm['cmp_w'])
    o_cmp, p_cmp = nsa_compressed(q, pos_q, kcv)
    n_blk = T // SEL_BLOCK
    idx, valid = nsa_select_indices(p_cmp, pos_q, n_blk)
    blocks = jnp.transpose(kvs.reshape(B, n_blk, SEL_BLOCK, 2, A_KV_HEADS, HEAD_DIM), (0, 4, 1, 2, 3, 5))
    bi = jnp.arange(B)[:, None, None, None]
    ki = jnp.arange(A_KV_HEADS)[None, None, :, None]
    def sel_block(i, qi, idx_i, valid_i):
        pq = i * SEL_Q_BLOCK + jnp.arange(SEL_Q_BLOCK, dtype=jnp.int32)
        g = blocks[bi, ki, jnp.minimum(idx_i, n_blk - 1)]
        return nsa_selected(qi, pq, idx_i, valid_i, g, table)
    o_slc = query_sweep(sel_block, SEL_Q_BLOCK, (q, idx, valid))
    o_win = banded_window_attn(q, kvw[:, :, 0], kvw[:, :, 1], NSA_WINDOW, table, None)
    y_lru, lru_buf, lru_h = rglru(xb, gb, jnp.zeros((B, LRU_CONV - 1, LRU_WIDTH), xb.dtype), jnp.zeros((B, LRU_WIDTH), F32), prm)
    y = ab_merge(o_cmp, o_slc, o_win, gates, y_lru, prm['w_out'], h.dtype)
    n_pg = T // PAGE_SIZE
    w_keep = min(NSA_WINDOW, T)
    return y, (kvc.reshape(B, n_pg, PAGE_SIZE, 2, A_KV_HEADS, HEAD_DIM),
               kvs.reshape(B, n_pg, PAGE_SIZE, 2, A_KV_HEADS, HEAD_DIM),
               kvw[:, T - w_keep:], lru_buf, lru_h)


def ab_sample(h, prm, table, pool_cmp, pool_slc, layer, win_buf, lru_buf, lru_h, page_table):
    q, kvc, kvs, kvw, gates, xb, gb = ab_project(h, prm)
    B, T = h.shape[0], h.shape[1]
    pos_q = PAST_LEN + jnp.arange(T, dtype=jnp.int32)
    n_past_blk = PAST_LEN // SEL_BLOCK
    n_tail_blk = -(-T // SEL_BLOCK)
    tail_pad = n_tail_blk * SEL_BLOCK - T
    pad5 = ((0, 0), (0, tail_pad), (0, 0), (0, 0), (0, 0))
    past_c = pool_cmp[page_table, :, layer].reshape(B, PAST_LEN, 2, A_KV_HEADS, HEAD_DIM)
    kcv = jnp.concatenate([compress_kv(past_c, prm['cmp_pe'], prm['cmp_w']),
                           compress_kv(jnp.pad(kvc, pad5).astype(past_c.dtype), prm['cmp_pe'], prm['cmp_w'])], axis=1)
    o_cmp, p_cmp = nsa_compressed(q, pos_q, kcv)
    idx, valid = nsa_select_indices(p_cmp, pos_q, n_past_blk + n_tail_blk)
    bi = jnp.arange(B)[:, None, None, None]
    ki = jnp.arange(A_KV_HEADS)[None, None, :, None]
    blocks_per_page = PAGE_SIZE // SEL_BLOCK
    idx_p = jnp.minimum(idx, n_past_blk - 1)
    phys = page_table[bi, idx_p // blocks_per_page]
    rows = (idx_p % blocks_per_page)[..., None] * SEL_BLOCK + jnp.arange(SEL_BLOCK, dtype=jnp.int32)
    g_past = pool_slc[phys[..., None], rows, layer, :, ki[..., None], :]
    tail = jnp.transpose(jnp.pad(kvs, pad5).reshape(B, n_tail_blk, SEL_BLOCK, 2, A_KV_HEADS, HEAD_DIM), (0, 4, 1, 2, 3, 5))
    g_tail = tail[bi, ki, jnp.clip(idx - n_past_blk, 0, n_tail_blk - 1)]
    g = jnp.where((idx >= n_past_blk)[..., None, None, None], g_tail.astype(g_past.dtype), g_past)
    o_slc = nsa_selected(q, pos_q, idx, valid, g, table)
    w_buf = win_buf.shape[1]
    kv_all = jnp.concatenate([win_buf.astype(kvw.dtype), kvw], axis=1)
    pos_k = PAST_LEN - w_buf + jnp.arange(w_buf + T, dtype=jnp.int32)
    o_win = window_attn(q, kv_all[:, :, 0], kv_all[:, :, 1], pos_q, pos_k, NSA_WINDOW, table, None)
    y_lru, new_buf, new_h = rglru(xb, gb, lru_buf, lru_h, prm)
    y = ab_merge(o_cmp, o_slc, o_win, gates, y_lru, prm['w_out'], h.dtype)
    return y, (kvc, kvs, kv_all[:, T:], new_buf, new_h)


def cd_project(h, prm):
    q, kv, u = split_cols(h @ prm['w_in'], CD_SPLITS)
    B, T = h.shape[0], h.shape[1]
    return q.reshape(B, T, C_KV_HEADS, C_GROUP, HEAD_DIM), kv.reshape(B, T, 2, C_KV_HEADS, HEAD_DIM), u


def cd_merge(o, y_s5, w_out, dtype):
    B, T = o.shape[0], o.shape[1]
    return jnp.concatenate([o.reshape(B, T, C_WIDTH), y_s5], axis=-1).astype(dtype) @ w_out


def cd_prompt(h, prm, table):
    q, kv, u = cd_project(h, prm)
    B, T = h.shape[0], h.shape[1]
    o = banded_window_attn(q, kv[:, :, 0], kv[:, :, 1], SWA_WINDOW, table, prm['sinks'])
    z = jnp.zeros((B, S5_GROUPS, S5_STATE), F32)
    y_s5, s_re, s_im = s5_layer(u, z, z, prm)
    w_keep = min(SWA_WINDOW, T)
    return cd_merge(o, y_s5, prm['w_out'], h.dtype), (kv[:, T - w_keep:], s_re, s_im)


def cd_sample(h, prm, table, win_buf, s_re, s_im):
    q, kv, u = cd_project(h, prm)
    T = h.shape[1]
    w_buf = win_buf.shape[1]
    kv_all = jnp.concatenate([win_buf.astype(kv.dtype), kv], axis=1)
    pos_q = PAST_LEN + jnp.arange(T, dtype=jnp.int32)
    pos_k = PAST_LEN - w_buf + jnp.arange(w_buf + T, dtype=jnp.int32)
    o = window_attn(q, kv_all[:, :, 0], kv_all[:, :, 1], pos_q, pos_k, SWA_WINDOW, table, prm['sinks'])
    y_s5, n_re, n_im = s5_layer(u, s_re, s_im, prm)
    return cd_merge(o, y_s5, prm['w_out'], h.dtype), (kv_all[:, T:], n_re, n_im)


def conv_ffn(h, buf, prm):
    z = h @ prm['w_up']
    g, v = z[..., :D_FF], z[..., D_FF:]
    g, new_buf = causal_dwconv(g, buf, prm['w_conv'], prm['b_conv'])
    return (jax.nn.gelu(g) * v) @ prm['w_down'], new_buf


def setup_inputs(seed: int = 0) -> dict:
    key = jax.random.key(seed)
    keys = iter(jax.random.split(key, 64))
    def nrm(shape, scale=1.0):
        return scale * jax.random.normal(next(keys), shape, F32)
    n_pages = PAST_LEN // PAGE_SIZE
    n_used = DEC_BATCH * n_pages
    n_pool = n_used + n_used // 4
    w_nsa = min(NSA_WINDOW, PAST_LEN)
    w_swa = min(SWA_WINDOW, PAST_LEN)
    x_prompt = nrm((BATCH, SEQ, D_MODEL))
    x_sample = nrm((DEC_BATCH, DEC_SEQ, D_MODEL))
    cache_kv_cmp = nrm((n_pool, PAGE_SIZE, N_AB, 2, A_KV_HEADS, HEAD_DIM))
    cache_kv_slc = nrm((n_pool, PAGE_SIZE, N_AB, 2, A_KV_HEADS, HEAD_DIM))
    cache_kv_win = nrm((DEC_BATCH, N_AB, w_nsa, 2, A_KV_HEADS, HEAD_DIM))
    state_lru_conv = nrm((DEC_BATCH, N_AB, LRU_CONV - 1, LRU_WIDTH))
    state_lru_h = nrm((DEC_BATCH, N_AB, LRU_WIDTH), 0.5)
    cache_kv_swa = nrm((DEC_BATCH, N_CD, w_swa, 2, C_KV_HEADS, HEAD_DIM))
    state_s5_re = nrm((DEC_BATCH, N_CD, S5_GROUPS, S5_STATE), 0.1)
    state_s5_im = nrm((DEC_BATCH, N_CD, S5_GROUPS, S5_STATE), 0.1)
    state_ffn_conv = nrm((DEC_BATCH, DEPTH, FFN_CONV - 1, D_FF))
    page_table = jax.random.permutation(next(keys), n_pool)[:n_used].reshape(DEC_BATCH, n_pages).astype(jnp.int32)
    c_prompt = nrm((BATCH, D_MODEL))
    c_sample = nrm((DEC_BATCH, D_MODEL))
    a_lru = jax.random.uniform(next(keys), (N_AB, LRU_WIDTH), F32, 0.9, 0.999) ** (1.0 / LRU_C)
    lru_lambda = jnp.log(a_lru) - jnp.log1p(-a_lru)
    s5_log_dt = jax.random.uniform(next(keys), (N_CD, S5_GROUPS), F32, math.log(1e-3), math.log(1e-1))
    s5_lambda_re = -0.5 + nrm((N_CD, S5_GROUPS, S5_STATE), 0.01)
    s5_lambda_im = jnp.pi * jnp.arange(S5_STATE, dtype=F32) + nrm((N_CD, S5_GROUPS, S5_STATE), 0.01)
    return {
        'x_prompt': x_prompt,
        'x_sample': x_sample,
        'cache_kv_cmp': cache_kv_cmp,
        'cache_kv_slc': cache_kv_slc,
        'cache_kv_win': cache_kv_win,
        'state_lru_conv': state_lru_conv,
        'state_lru_h': state_lru_h,
        'cache_kv_swa': cache_kv_swa,
        'state_s5_re': state_s5_re,
        'state_s5_im': state_s5_im,
        'state_ffn_conv': state_ffn_conv,
        'page_table': page_table,
        'c_prompt': c_prompt,
        'c_sample': c_sample,
        'rel_bias': nrm((N_BUCKETS, BIAS_HEADS), 0.5),
        'w_mod': nrm((DEPTH, 2, D_MODEL, 3 * D_MODEL), 0.5 * D_MODEL ** -0.5),
        'b_mod': nrm((DEPTH, 2, 3 * D_MODEL), 0.02),
        'ln_g': 1.0 + nrm((DEPTH, 2, D_MODEL), 0.02),
        'ln_b': nrm((DEPTH, 2, D_MODEL), 0.02),
        'w_in_ab': nrm((N_AB, D_MODEL, AB_IN), D_MODEL ** -0.5),
        'w_out_ab': nrm((N_AB, AB_MIX, D_MODEL), DN_BETA * AB_MIX ** -0.5),
        'nsa_cmp_pe': nrm((N_AB, 2, CMP_BLOCK, HEAD_DIM), 0.5),
        'nsa_cmp_w': nrm((N_AB, 2, CMP_BLOCK, HEAD_DIM, HEAD_DIM), (CMP_BLOCK * HEAD_DIM) ** -0.5),
        'lru_conv_w': nrm((N_AB, LRU_CONV, LRU_WIDTH), LRU_CONV ** -0.5),
        'lru_conv_b': nrm((N_AB, LRU_WIDTH), 0.02),
        'lru_w_r': nrm((N_AB, LRU_BLOCKS, LRU_BLOCK_DIM, LRU_BLOCK_DIM), LRU_BLOCK_DIM ** -0.5),
        'lru_b_r': nrm((N_AB, LRU_WIDTH), 0.02),
        'lru_w_i': nrm((N_AB, LRU_BLOCKS, LRU_BLOCK_DIM, LRU_BLOCK_DIM), LRU_BLOCK_DIM ** -0.5),
        'lru_b_i': nrm((N_AB, LRU_WIDTH), 0.02),
        'lru_lambda': lru_lambda,
        'w_in_cd': nrm((N_CD, D_MODEL, CD_IN), D_MODEL ** -0.5),
        'w_out_cd': nrm((N_CD, CD_MIX, D_MODEL), DN_BETA * CD_MIX ** -0.5),
        'swa_sinks': nrm((N_CD, C_HEADS), 1.0),
        's5_lambda_re': s5_lambda_re,
        's5_lambda_im': s5_lambda_im,
        's5_log_dt': s5_log_dt,
        's5_b_re': nrm((N_CD, S5_GROUPS, S5_STATE, S5_GROUP_CH), (2 * S5_GROUP_CH) ** -0.5),
        's5_b_im': nrm((N_CD, S5_GROUPS, S5_STATE, S5_GROUP_CH), (2 * S5_GROUP_CH) ** -0.5),
        's5_c_re': nrm((N_CD, S5_GROUPS, S5_GROUP_CH, S5_STATE), S5_STATE ** -0.5),
        's5_c_im': nrm((N_CD, S5_GROUPS, S5_GROUP_CH, S5_STATE), S5_STATE ** -0.5),
        's5_d': nrm((N_CD, S5_WIDTH), 1.0),
        's5_w_glu': nrm((N_CD, S5_WIDTH, S5_WIDTH), S5_WIDTH ** -0.5),
        's5_b_glu': nrm((N_CD, S5_WIDTH), 0.02),
        'w_ffn_up': nrm((DEPTH, D_MODEL, 2 * D_FF), D_MODEL ** -0.5),
        'w_ffn_conv': nrm((DEPTH, FFN_CONV, D_FF), FFN_CONV ** -0.5),
        'b_ffn_conv': nrm((DEPTH, D_FF), 0.02),
        'w_ffn_down': nrm((DEPTH, D_FF, D_MODEL), DN_BETA * D_FF ** -0.5),
    }


def reference(x_prompt, x_sample, cache_kv_cmp, cache_kv_slc, cache_kv_win, state_lru_conv, state_lru_h,
              cache_kv_swa, state_s5_re, state_s5_im, state_ffn_conv, page_table, c_prompt, c_sample,
              rel_bias, w_mod, b_mod, ln_g, ln_b, w_in_ab, w_out_ab, nsa_cmp_pe, nsa_cmp_w,
              lru_conv_w, lru_conv_b, lru_w_r, lru_b_r, lru_w_i, lru_b_i, lru_lambda,
              w_in_cd, w_out_cd, swa_sinks, s5_lambda_re, s5_lambda_im, s5_log_dt, s5_b_re, s5_b_im,
              s5_c_re, s5_c_im, s5_d, s5_w_glu, s5_b_glu, w_ffn_up, w_ffn_conv, b_ffn_conv, w_ffn_down):
    xp, xs = x_prompt, x_sample
    Bp = xp.shape[0]
    pk_c, pk_s, pk_w, pl_c, pl_h, pc_w, ps_r, ps_i, pf = [], [], [], [], [], [], [], [], []
    sk_c, sk_s, sk_w, sl_c, sl_h, sc_w, ss_r, ss_i, sf = [], [], [], [], [], [], [], [], []
    for li in range(DEPTH):
        j = li // 2
        shp, scp, gtp = ada_mod(c_prompt, w_mod[li, 0], b_mod[li, 0])
        shs, scs, gts = ada_mod(c_sample, w_mod[li, 0], b_mod[li, 0])
        hp = xp * (1 + scp) + shp
        hs = xs * (1 + scs) + shs
        if li % 2 == 0:
            prm = {'w_in': w_in_ab[j], 'w_out': w_out_ab[j], 'cmp_pe': nsa_cmp_pe[j], 'cmp_w': nsa_cmp_w[j],
                   'lru_conv_w': lru_conv_w[j], 'lru_conv_b': lru_conv_b[j], 'lru_w_r': lru_w_r[j],
                   'lru_b_r': lru_b_r[j], 'lru_w_i': lru_w_i[j], 'lru_b_i': lru_b_i[j], 'lru_lambda': lru_lambda[j]}
            yp, stp = ab_prompt(hp, prm, rel_bias)
            ys, sts = ab_sample(hs, prm, rel_bias, cache_kv_cmp, cache_kv_slc, j, cache_kv_win[:, j],
                                state_lru_conv[:, j], state_lru_h[:, j], page_table)
            pk_c.append(stp[0]); pk_s.append(stp[1]); pk_w.append(stp[2]); pl_c.append(stp[3]); pl_h.append(stp[4])
            sk_c.append(sts[0]); sk_s.append(sts[1]); sk_w.append(sts[2]); sl_c.append(sts[3]); sl_h.append(sts[4])
        else:
            prm = {'w_in': w_in_cd[j], 'w_out': w_out_cd[j], 'sinks': swa_sinks[j], 'lam_re': s5_lambda_re[j],
                   'lam_im': s5_lambda_im[j], 'log_dt': s5_log_dt[j], 'b_re': s5_b_re[j], 'b_im': s5_b_im[j],
                   'c_re': s5_c_re[j], 'c_im': s5_c_im[j], 'd': s5_d[j], 'w_glu': s5_w_glu[j], 'b_glu': s5_b_glu[j]}
            yp, stp = cd_prompt(hp, prm, rel_bias)
            ys, sts = cd_sample(hs, prm, rel_bias, cache_kv_swa[:, j], state_s5_re[:, j], state_s5_im[:, j])
            pc_w.append(stp[0]); ps_r.append(stp[1]); ps_i.append(stp[2])
            sc_w.append(sts[0]); ss_r.append(sts[1]); ss_i.append(sts[2])
        xp = layer_norm(DN_ALPHA * xp + (1 + gtp) * yp, ln_g[li, 0], ln_b[li, 0])
        xs = layer_norm(DN_ALPHA * xs + (1 + gts) * ys, ln_g[li, 0], ln_b[li, 0])
        shp, scp, gtp = ada_mod(c_prompt, w_mod[li, 1], b_mod[li, 1])
        shs, scs, gts = ada_mod(c_sample, w_mod[li, 1], b_mod[li, 1])
        fprm = {'w_up': w_ffn_up[li], 'w_conv': w_ffn_conv[li], 'b_conv': b_ffn_conv[li], 'w_down': w_ffn_down[li]}
        yp, bp = conv_ffn(xp * (1 + scp) + shp, jnp.zeros((Bp, FFN_CONV - 1, D_FF), xp.dtype), fprm)
        ys, bs = conv_ffn(xs * (1 + scs) + shs, state_ffn_conv[:, li], fprm)
        pf.append(bp); sf.append(bs)
        xp = layer_norm(DN_ALPHA * xp + (1 + gtp) * yp, ln_g[li, 1], ln_b[li, 1])
        xs = layer_norm(DN_ALPHA * xs + (1 + gts) * ys, ln_g[li, 1], ln_b[li, 1])
    return (xp, xs,
            jnp.stack(pk_c, axis=3), jnp.stack(pk_s, axis=3), jnp.stack(pk_w, axis=1),
            jnp.stack(pl_c, axis=1), jnp.stack(pl_h, axis=1), jnp.stack(pc_w, axis=1),
            jnp.stack(ps_r, axis=1), jnp.stack(ps_i, axis=1), jnp.stack(pf, axis=1),
            jnp.stack(sk_c, axis=2), jnp.stack(sk_s, axis=2), jnp.stack(sk_w, axis=1),
            jnp.stack(sl_c, axis=1), jnp.stack(sl_h, axis=1), jnp.stack(sc_w, axis=1),
            jnp.stack(ss_r, axis=1), jnp.stack(ss_i, axis=1), jnp.stack(sf, axis=1))
```

```python
import functools
import math

import numpy as np
import jax
import jax.numpy as jnp
from jax import lax
from jax.experimental import pallas as pl
from jax.experimental.pallas import tpu as pltpu

D_MODEL = 1024
DEPTH = 2
PAST_LEN = 16384
PAGE_SIZE = 128
HEAD_DIM = 64
N_AB = (DEPTH + 1) // 2
N_HEADS = 8
N_KV = 2
GROUP = N_HEADS // N_KV
CMP_BLOCK = 32
SEL_BLOCK = 64
N_SEL = 16
NSA_WINDOW = 512
LRU_WIDTH = D_MODEL // 2
LRU_BLOCKS = 8
LRU_CONV = 4
LRU_C = 8.0
SWA_WINDOW = 128
S5_WIDTH = D_MODEL // 2
S5_GROUP_CH = 16
S5_GROUPS = S5_WIDTH // S5_GROUP_CH
S5_STATE = 64
D_FF = (8 * D_MODEL // 3) // 128 * 128
FFN_CONV = 3
N_BUCKETS = 32
MAX_DISTANCE = 128
DN_ALPHA = (2 * DEPTH) ** 0.25
LN_EPS = 1e-5
SCALE = HEAD_DIM ** -0.5
NEG = -1e30
FORCE_SCORE = 1e4
KV_COLS = 2 * N_KV * HEAD_DIM
CMP_IN = CMP_BLOCK * KV_COLS
F32 = jnp.float32
BF16 = jnp.bfloat16

LANES = 128
TQ = 128
VMEM_LIMIT = 56 * 1024 * 1024


def _cp(sem, vmem=VMEM_LIMIT):
    return pltpu.CompilerParams(dimension_semantics=sem, vmem_limit_bytes=vmem)


def _dot(a, b):
    return jnp.dot(a, b, preferred_element_type=F32)


def _dot_nt(a, b):
    return lax.dot_general(a, b, (((1,), (1,)), ((), ())), preferred_element_type=F32)


def _gelu(x):
    cdf = 0.5 * (1.0 + jnp.tanh(math.sqrt(2.0 / math.pi) * (x + 0.044715 * (x * x * x))))
    return x * cdf


def _softplus(x):
    return jnp.maximum(x, 0.0) + jnp.log1p(jnp.exp(-jnp.abs(x)))


def _expm1(x):
    u = jnp.exp(x)
    um1 = u - 1.0
    edge = (u == 1.0) | (um1 == -1.0)
    r = um1 * x / jnp.log(jnp.where(edge, 2.0, u))
    return jnp.where(u == 1.0, x, jnp.where(um1 == -1.0, -1.0, r))


def _t5_bucket(dist):
    n = jnp.maximum(dist, 0)
    max_exact = N_BUCKETS // 2
    nf = jnp.maximum(n, 1).astype(F32)
    large = max_exact + (jnp.log(nf / max_exact) / math.log(MAX_DISTANCE / max_exact) * (N_BUCKETS - max_exact)).astype(jnp.int32)
    return jnp.where(n < max_exact, n, jnp.minimum(large, N_BUCKETS - 1))


def _bucket_saturates_from(n0, n1):
    n = np.arange(n0, n1, dtype=np.float64)
    b = 16 + np.floor(np.log(n / 16.0) / math.log(8.0) * 16.0)
    return bool(np.all(b >= N_BUCKETS - 1 + 0.5))


assert _bucket_saturates_from(TQ + 1, PAST_LEN + 2 * TQ)


def _mod_body(c_ref, w_ref, b_ref, o_ref):
    c = c_ref[...]
    h = (c * jax.nn.sigmoid(c)).astype(BF16)
    o_ref[0] = _dot(h, w_ref[0].astype(BF16)) + b_ref[0]


def _ada_mod_all(c_all, w_mod, b_mod):
    n_sub = w_mod.shape[0]
    rows = c_all.shape[0]
    tn = 1024
    return pl.pallas_call(
        _mod_body,
        grid=(n_sub, 3 * D_MODEL // tn),
        in_specs=[pl.BlockSpec((rows, D_MODEL), lambda s, n: (0, 0)),
                  pl.BlockSpec((1, D_MODEL, tn), lambda s, n: (s, 0, n)),
                  pl.BlockSpec((1, 1, tn), lambda s, n: (s, 0, n))],
        out_specs=pl.BlockSpec((1, rows, tn), lambda s, n: (s, 0, n)),
        out_shape=jax.ShapeDtypeStruct((n_sub, rows, 3 * D_MODEL), F32),
        compiler_params=_cp(("arbitrary", "arbitrary")),
        name="ada_mod",
    )(c_all, w_mod, b_mod)


def _modmm_body(x_ref, sc_ref, sh_ref, *refs, n_w, out_tm):
    h = (x_ref[0] * (1.0 + sc_ref[0]) + sh_ref[0]).astype(BF16)
    for w_ref, o_ref, tm_major in zip(refs[:n_w], refs[n_w:], out_tm):
        r = _dot(h, w_ref[...])
        if tm_major:
            o_ref[...] = r
        else:
            o_ref[0] = r


def _modmm(x, sc, sh, ws, out_tm, tm):
    B, T, D = x.shape
    per_row = sc.shape[1] != 1
    mod_spec = pl.BlockSpec((1, tm if per_row else 1, D), (lambda b, t: (b, t, 0)) if per_row else (lambda b, t: (b, 0, 0)))
    in_specs = [pl.BlockSpec((1, tm, D), lambda b, t: (b, t, 0)), mod_spec, mod_spec]
    out_specs, out_shape = [], []
    for w, tmj in zip(ws, out_tm):
        n = w.shape[1]
        in_specs.append(pl.BlockSpec((D, n), lambda b, t: (0, 0)))
        if tmj:
            out_specs.append(pl.BlockSpec((tm, n), lambda b, t: (t, b)))
            out_shape.append(jax.ShapeDtypeStruct((T, B * n), F32))
        else:
            out_specs.append(pl.BlockSpec((1, tm, n), lambda b, t: (b, t, 0)))
            out_shape.append(jax.ShapeDtypeStruct((B, T, n), F32))
    return pl.pallas_call(
        functools.partial(_modmm_body, n_w=len(ws), out_tm=tuple(out_tm)),
        grid=(B, T // tm),
        in_specs=in_specs, out_specs=out_specs, out_shape=out_shape,
        compiler_params=_cp(("arbitrary", "arbitrary")),
        name="mod_proj",
    )(x, sc, sh, *ws)


def _mm_res_ln_body(*refs, n_a, a_tm, x_tm, o_tm):
    a_refs, w_refs = refs[:n_a], refs[n_a:2 * n_a]
    x_ref, gt_ref, g_ref, b_ref, o_ref = refs[2 * n_a:]
    y = None
    for a_ref, w_ref, tmj in zip(a_refs, w_refs, a_tm):
        a = (a_ref[...] if tmj else a_ref[0]).astype(BF16)
        r = _dot(a, w_ref[...])
        y = r if y is None else y + r
    x = x_ref[...] if x_tm else x_ref[0]
    z = DN_ALPHA * x + (1.0 + gt_ref[0]) * y
    mu = jnp.mean(z, axis=-1, keepdims=True)
    var = jnp.mean(jnp.square(z - mu), axis=-1, keepdims=True)
    out = (z - mu) * lax.rsqrt(var + LN_EPS) * g_ref[...] + b_ref[...]
    if o_tm:
        o_ref[...] = out
    else:
        o_ref[0] = out


def _mm_res_ln(a_list, a_tm, ws, x, x_tm, gate, ln_g, ln_b, o_tm, B, T, tm):
    D = D_MODEL
    per_row = gate.shape[1] != 1

    def spec(k, tmj):
        if tmj:
            return pl.BlockSpec((tm, k), lambda b, t: (t, b))
        return pl.BlockSpec((1, tm, k), lambda b, t: (b, t, 0))

    in_specs = [spec(w.shape[0], tmj) for w, tmj in zip(ws, a_tm)]
    in_specs += [pl.BlockSpec(w.shape, lambda b, t: (0, 0)) for w in ws]
    in_specs += [spec(D, x_tm),
                 pl.BlockSpec((1, tm if per_row else 1, D), (lambda b, t: (b, t, 0)) if per_row else (lambda b, t: (b, 0, 0))),
                 pl.BlockSpec((1, D), lambda b, t: (0, 0)),
                 pl.BlockSpec((1, D), lambda b, t: (0, 0))]
    out_shape = jax.ShapeDtypeStruct((T, B * D) if o_tm else (B, T, D), F32)
    return pl.pallas_call(
        functools.partial(_mm_res_ln_body, n_a=len(ws), a_tm=tuple(a_tm), x_tm=x_tm, o_tm=o_tm),
        grid=(B, T // tm),
        in_specs=in_specs, out_specs=spec(D, o_tm), out_shape=out_shape,
        compiler_params=_cp(("arbitrary", "arbitrary")),
        name="proj_res_ln",
    )(*a_list, *ws, x, gate, ln_g.reshape(1, D), ln_b.reshape(1, D))


def _compress_body(x_ref, w_ref, pe_ref, o_ref):
    w = w_ref[...]
    pe_term = _dot(pe_ref[...].astype(BF16), w)[0:1]
    o_ref[...] = _dot(x_ref[...].astype(BF16), w) + pe_term


def _compress(x_rows, w_full, pe_rows, tr):
    R = x_rows.shape[0]
    return pl.pallas_call(
        _compress_body,
        grid=(R // tr,),
        in_specs=[pl.BlockSpec((tr, CMP_IN), lambda r: (r, 0)),
                  pl.BlockSpec((CMP_IN, KV_COLS), lambda r: (0, 0)),
                  pl.BlockSpec((8, CMP_IN), lambda r: (0, 0))],
        out_specs=pl.BlockSpec((tr, KV_COLS), lambda r: (r, 0)),
        out_shape=jax.ShapeDtypeStruct((R, KV_COLS), F32),
        compiler_params=_cp(("arbitrary",)),
        name="nsa_compress",
    )(x_rows, w_full, pe_rows)


def _compress_weights(cmp_w, cmp_pe):
    eye = jnp.eye(2, dtype=F32)
    wt = jnp.transpose(cmp_w, (1, 0, 2, 3))
    w_full = jnp.einsum('csde,st,kl->cskdtle', wt, eye, eye).reshape(CMP_IN, KV_COLS).astype(BF16)
    pe = jnp.transpose(cmp_pe, (1, 0, 2))[:, :, None, :]
    pe = jnp.broadcast_to(pe, (CMP_BLOCK, 2, N_KV, HEAD_DIM)).reshape(1, CMP_IN)
    return w_full, jnp.broadcast_to(pe, (8, CMP_IN))


def _padded_queries(q):
    lo = lax.broadcasted_iota(jnp.int32, (TQ, LANES), 1) < HEAD_DIM
    out = []
    for h in range(N_HEADS):
        x = q[:, (h // 2) * LANES:(h // 2 + 1) * LANES]
        k = h // GROUP
        if h % 2 != k:
            x = pltpu.roll(x, HEAD_DIM, axis=1)
        out.append(jnp.where(lo if k == 0 else ~lo, x, 0.0).astype(BF16))
    return out


def _attend(carry, qg, kt, vt, bias, mask):
    m, l, acc = carry
    s = _dot_nt(kt, qg) + bias
    if mask is not None:
        s = jnp.where(mask, s, NEG)
    m_new = jnp.maximum(m, jnp.max(s, axis=0, keepdims=True))
    alpha = jnp.exp(m - m_new)
    p = jnp.exp(s - m_new)
    if mask is not None:
        p = jnp.where(mask, p, 0.0)
    l = alpha * l + jnp.sum(p, axis=0, keepdims=True)
    acc = alpha * acc + _dot(vt, p.astype(BF16))
    return m_new, l, acc


def _finish(carry):
    _, l, acc = carry
    return acc / jnp.where(l > 0, l, 1.0)


def _key_tile(kv_ref, vt_ref, k, j):
    start = pl.multiple_of(j * TQ, TQ)
    kt = kv_ref[0, pl.ds(start, TQ), 0:LANES].astype(BF16)
    vt = vt_ref[k * HEAD_DIM:(k + 1) * HEAD_DIM, pl.ds(start, TQ)].astype(BF16)
    return kt, vt


def _window_branch(carry, qg, kv_ref, vt_ref, k, i, n_win, bnear_ref, bfar, causal, lt_mask):
    for delta in range(n_win, 0, -1):
        bias = bnear_ref[1, k] if delta == 1 else bfar
        mask = lt_mask if delta == n_win else None

        def body(j, c, bias=bias, mask=mask):
            kt, vt = _key_tile(kv_ref, vt_ref, k, j)
            return _attend(c, qg, kt, vt, bias, mask)

        has = i >= delta
        lo = jnp.where(has, i - delta, 0)
        carry = lax.fori_loop(lo, jnp.where(has, lo + 1, 0), body, carry)
    kt, vt = _key_tile(kv_ref, vt_ref, k, i)
    return _attend(carry, qg, kt, vt, bnear_ref[0, k], causal)


def _transpose_values(kv_ref, vt_ref, T):
    for c in range(T // TQ):
        vt_ref[:, c * TQ:(c + 1) * TQ] = kv_ref[0, c * TQ:(c + 1) * TQ, LANES:2 * LANES].T


def _store_heads(o_ref, o_groups):
    for k in range(N_KV):
        for pair in range(GROUP // 2):
            a = o_groups[k][:, (2 * pair) * TQ:(2 * pair + 1) * TQ]
            b = o_groups[k][:, (2 * pair + 1) * TQ:(2 * pair + 2) * TQ]
            col = (k * GROUP + 2 * pair) * HEAD_DIM
            o_ref[0, :, col:col + 2 * HEAD_DIM] = jnp.concatenate([a, b], axis=0).T


def _nsa_prompt_body(q_ref, kvs_ref, kvw_ref, kc_ref, vct_ref, gates_ref, bnear_ref, bfar_ref, o_ref,
                     vst_ref, vwt_ref, sel_ref, *, T):
    i = pl.program_id(1)
    nc = T // CMP_BLOCK
    nsel = nc // 2

    @pl.when(i == 0)
    def _():
        _transpose_values(kvs_ref, vst_ref, T)
        _transpose_values(kvw_ref, vwt_ref, T)

    t0 = i * TQ
    qpads = _padded_queries(q_ref[0] * SCALE)
    gt = jax.nn.sigmoid(gates_ref[0]).T
    W = GROUP * TQ
    c_lane = lax.broadcasted_iota(jnp.int32, (1, W), 1) & (TQ - 1)
    tpos = t0 + c_lane
    r_key = lax.broadcasted_iota(jnp.int32, (TQ, W), 0)
    causal = c_lane >= r_key
    lt_mask = c_lane < r_key
    r_c = lax.broadcasted_iota(jnp.int32, (nc, W), 0)
    blk = 2 * (r_c % nsel) + r_c // nsel
    vis = (blk + 1) * CMP_BLOCK - 1 <= tpos
    j_s = lax.broadcasted_iota(jnp.int32, (nsel, TQ), 0)
    cur = (t0 + lax.broadcasted_iota(jnp.int32, (nsel, TQ), 1)) // SEL_BLOCK
    init = (jnp.full((1, W), NEG, F32), jnp.zeros((1, W), F32), jnp.zeros((HEAD_DIM, W), F32))
    o_groups = []
    for k in range(N_KV):
        qg = jnp.concatenate(qpads[GROUP * k:GROUP * (k + 1)], axis=0)
        s = jnp.where(vis, _dot_nt(kc_ref[0].astype(BF16), qg), NEG)
        m = jnp.max(s, axis=0, keepdims=True)
        e = jnp.where(vis, jnp.exp(s - m), 0.0)
        den = jnp.sum(e, axis=0, keepdims=True)
        p = e / jnp.where(den > 0, den, 1.0)
        o_cmp = _dot(vct_ref[0, k * HEAD_DIM:(k + 1) * HEAD_DIM, :].astype(BF16), p.astype(BF16))
        ps = p[:, 0:TQ] + p[:, TQ:2 * TQ] + p[:, 2 * TQ:3 * TQ] + p[:, 3 * TQ:4 * TQ]
        imp = ps[0:nsel] + ps[nsel:nc]
        forced = (j_s == 0) | (j_s == cur) | (j_s == cur - 1)
        score = jnp.where(forced, FORCE_SCORE, jnp.where(j_s <= cur, imp, NEG))
        chosen = jnp.zeros((nsel, TQ), F32)
        for _ in range(N_SEL):
            top = jnp.max(score, axis=0, keepdims=True)
            idx = jnp.min(jnp.where(score == top, j_s, nsel), axis=0, keepdims=True)
            hit = j_s == idx
            chosen = jnp.where(hit, 1.0, chosen)
            score = jnp.where(hit, -jnp.inf, score)
        chosen = jnp.where(j_s <= cur, chosen, 0.0)
        sel_ref[k] = jnp.concatenate([chosen] * GROUP, axis=1)

        def sel_mask(j):
            r0 = sel_ref[k, pl.ds(2 * j, 1), :]
            r1 = sel_ref[k, pl.ds(2 * j + 1, 1), :]
            return jnp.where(r_key < SEL_BLOCK, r0, r1) > 0.5

        bfar = bfar_ref[k]

        def far_body(j, c):
            kt, vt = _key_tile(kvs_ref, vst_ref, k, j)
            return _attend(c, qg, kt, vt, bfar, sel_mask(j))

        def near_body(j, c):
            kt, vt = _key_tile(kvs_ref, vst_ref, k, j)
            return _attend(c, qg, kt, vt, bnear_ref[1, k], sel_mask(j))

        n_far = jnp.maximum(i - 1, 0)
        c_slc = lax.fori_loop(0, n_far, far_body, init)
        c_slc = lax.fori_loop(n_far, i, near_body, c_slc)
        kt, vt = _key_tile(kvs_ref, vst_ref, k, i)
        c_slc = _attend(c_slc, qg, kt, vt, bnear_ref[0, k], sel_mask(i) & causal)
        o_slc = _finish(c_slc)
        c_win = _window_branch(init, qg, kvw_ref, vwt_ref, k, i, NSA_WINDOW // TQ, bnear_ref, bfar, causal, lt_mask)
        o_win = _finish(c_win)

        def gate_row(branch):
            return jnp.concatenate([gt[branch * N_HEADS + GROUP * k + hh:branch * N_HEADS + GROUP * k + hh + 1, :]
                                    for hh in range(GROUP)], axis=1)

        o_groups.append(gate_row(0) * o_cmp + gate_row(1) * o_slc + gate_row(2) * o_win)
    _store_heads(o_ref, o_groups)


def _bias_tiles(rel_bias):
    r = jnp.arange(TQ, dtype=jnp.int32)[:, None]
    c = jnp.arange(TQ, dtype=jnp.int32)[None, :]
    near = jnp.stack([rel_bias[_t5_bucket(d * TQ + c - r)] for d in range(2)])
    near = jnp.transpose(near.reshape(2, TQ, TQ, N_KV, GROUP), (0, 3, 1, 4, 2)).reshape(2, N_KV, TQ, GROUP * TQ)
    far = jnp.repeat(rel_bias[N_BUCKETS - 1].reshape(N_KV, 1, GROUP), TQ, axis=2)
    return near.astype(F32), far.astype(F32)


def _nsa_prompt_attn(q, kvs, kvw, kc, vct, gates, bnear, bfar):
    B, T, _ = q.shape
    nc = T // CMP_BLOCK
    W = GROUP * TQ
    return pl.pallas_call(
        functools.partial(_nsa_prompt_body, T=T),
        grid=(B, T // TQ),
        in_specs=[pl.BlockSpec((1, TQ, N_HEADS * HEAD_DIM), lambda b, i: (b, i, 0)),
                  pl.BlockSpec((1, T, KV_COLS), lambda b, i: (b, 0, 0)),
                  pl.BlockSpec((1, T, KV_COLS), lambda b, i: (b, 0, 0)),
                  pl.BlockSpec((1, nc, LANES), lambda b, i: (b, 0, 0)),
                  pl.BlockSpec((1, LANES, nc), lambda b, i: (b, 0, 0)),
                  pl.BlockSpec((1, TQ, LANES), lambda b, i: (b, i, 0)),
                  pl.BlockSpec((2, N_KV, TQ, W), lambda b, i: (0, 0, 0, 0)),
                  pl.BlockSpec((N_KV, 1, W), lambda b, i: (0, 0, 0))],
        out_specs=pl.BlockSpec((1, TQ, N_HEADS * HEAD_DIM), lambda b, i: (b, i, 0)),
        out_shape=jax.ShapeDtypeStruct((B, T, N_HEADS * HEAD_DIM), F32),
        scratch_shapes=[pltpu.VMEM((LANES, T), F32), pltpu.VMEM((LANES, T), F32),
                        pltpu.VMEM((N_KV, nc // 2, W), F32)],
        compiler_params=_cp(("arbitrary", "arbitrary")),
        name="nsa_prompt_attn",
    )(q, kvs, kvw, kc, vct, gates, bnear, bfar)


def _swa_prompt_body(q_ref, kv_ref, sink_ref, bnear_ref, bfar_ref, o_ref, vt_ref, *, T):
    i = pl.program_id(1)

    @pl.when(i == 0)
    def _():
        _transpose_values(kv_ref, vt_ref, T)

    qpads = _padded_queries(q_ref[0] * SCALE)
    W = GROUP * TQ
    c_lane = lax.broadcasted_iota(jnp.int32, (1, W), 1) & (TQ - 1)
    r_key = lax.broadcasted_iota(jnp.int32, (TQ, W), 0)
    causal = c_lane >= r_key
    lt_mask = c_lane < r_key
    o_groups = []
    for k in range(N_KV):
        qg = jnp.concatenate(qpads[GROUP * k:GROUP * (k + 1)], axis=0)
        init = (sink_ref[k], jnp.ones((1, W), F32), jnp.zeros((HEAD_DIM, W), F32))
        c = _window_branch(init, qg, kv_ref, vt_ref, k, i, SWA_WINDOW // TQ, bnear_ref, bfar_ref[k], causal, lt_mask)
        o_groups.append(_finish(c))
    _store_heads(o_ref, o_groups)


def _swa_prompt_attn(q, kv, sink_rows, bnear, bfar):
    B, T, _ = q.shape
    W = GROUP * TQ
    return pl.pallas_call(
        functools.partial(_swa_prompt_body, T=T),
        grid=(B, T // TQ),
        in_specs=[pl.BlockSpec((1, TQ, N_HEADS * HEAD_DIM), lambda b, i: (b, i, 0)),
                  pl.BlockSpec((1, T, KV_COLS), lambda b, i: (b, 0, 0)),
                  pl.BlockSpec((N_KV, 1, W), lambda b, i: (0, 0, 0)),
                  pl.BlockSpec((2, N_KV, TQ, W), lambda b, i: (0, 0, 0, 0)),
                  pl.BlockSpec((N_KV, 1, W), lambda b, i: (0, 0, 0))],
        out_specs=pl.BlockSpec((1, TQ, N_HEADS * HEAD_DIM), lambda b, i: (b, i, 0)),
        out_shape=jax.ShapeDtypeStruct((B, T, N_HEADS * HEAD_DIM), F32),
        scratch_shapes=[pltpu.VMEM((LANES, T), F32)],
        compiler_params=_cp(("arbitrary", "arbitrary")),
        name="swa_prompt_attn",
    )(q, kv, sink_rows, bnear, bfar)


def _rglru_body(xb_ref, gb_ref, prev0_ref, h0_ref, cw_ref, cb_ref, wr_ref, br_ref, wi_ref, bi_ref, lam_ref,
                y_ref, buf_ref, hl_ref, prev_ref, h_ref, a_ref, u_ref, *, tt):
    i = pl.program_id(0)

    @pl.when(i == 0)
    def _():
        prev_ref[...] = prev0_ref[...]
        h_ref[...] = h0_ref[...]

    x = xb_ref[...]
    B, W = x.shape[1], x.shape[2]
    xp = jnp.concatenate([prev_ref[...], x], axis=0)
    xc = xp[0:tt] * cw_ref[0:1, :]
    for j in range(1, LRU_CONV):
        xc = xc + xp[j:j + tt] * cw_ref[j:j + 1, :]
    xc = xc + cb_ref[...]
    prev_ref[...] = xp[tt:tt + LRU_CONV - 1]
    xf = xc.reshape(tt * B, W)
    xh = xf.astype(BF16)
    r = jax.nn.sigmoid(_dot(xh, wr_ref[...]) + br_ref[...])
    ig = jax.nn.sigmoid(_dot(xh, wi_ref[...]) + bi_ref[...])
    log_a = -LRU_C * r * _softplus(-lam_ref[...])
    a_ref[...] = jnp.exp(log_a).reshape(tt, B, W)
    u_ref[...] = (jnp.sqrt(-_expm1(2.0 * log_a)) * (ig * xf)).reshape(tt, B, W)

    def step(t, h):
        h = a_ref[t] * h + u_ref[t]
        u_ref[t] = h
        return h

    h = lax.fori_loop(0, tt, step, h_ref[...])
    h_ref[...] = h
    y_ref[...] = u_ref[...] * _gelu(gb_ref[...])
    buf_ref[...] = prev_ref[...]
    hl_ref[...] = h


def _rglru(xb, gb, prev0, h0, prm, tt):
    T, B, W = xb.shape
    full2 = lambda shp: pl.BlockSpec(shp, lambda i: (0, 0))
    blk = pl.BlockSpec((tt, B, W), lambda i: (i, 0, 0))
    return pl.pallas_call(
        functools.partial(_rglru_body, tt=tt),
        grid=(T // tt,),
        in_specs=[blk, blk, pl.BlockSpec((LRU_CONV - 1, B, W), lambda i: (0, 0, 0)), full2((B, W)),
                  full2((LRU_CONV, W)), full2((1, W)), full2((W, W)), full2((1, W)), full2((W, W)), full2((1, W)),
                  full2((1, W))],
        out_specs=[blk, pl.BlockSpec((LRU_CONV - 1, B, W), lambda i: (0, 0, 0)), full2((B, W))],
        out_shape=[jax.ShapeDtypeStruct((T, B, W), F32), jax.ShapeDtypeStruct((LRU_CONV - 1, B, W), F32),
                   jax.ShapeDtypeStruct((B, W), F32)],
        scratch_shapes=[pltpu.VMEM((LRU_CONV - 1, B, W), F32), pltpu.VMEM((B, W), F32),
                        pltpu.VMEM((tt, B, W), F32), pltpu.VMEM((tt, B, W), F32)],
        compiler_params=_cp(("arbitrary",)),
        name="rglru",
    )(xb, gb, prev0, h0, prm['conv_w'], prm['conv_b'], prm['w_r'], prm['b_r'], prm['w_i'], prm['b_i'], prm['lam'])


def _block_diag(w):
    n, d, e = w.shape
    return jnp.einsum('nde,nm->ndme', w, jnp.eye(n, dtype=w.dtype)).reshape(n * d, n * e)


S5_CHUNKS = 4
S5_CH_IN = S5_WIDTH // S5_CHUNKS
S5_CH_ST = S5_GROUPS * S5_STATE // S5_CHUNKS
S5_NSTATE = S5_GROUPS * S5_STATE


def _s5_disc(lr, li, dt):
    mag = jnp.exp(lr * dt)
    ab_re, ab_im = mag * jnp.cos(li * dt), mag * jnp.sin(li * dt)
    den = lr * lr + li * li
    f_re = ((ab_re - 1.0) * lr + ab_im * li) / den
    f_im = (ab_im * lr - (ab_re - 1.0) * li) / den
    return ab_re, ab_im, f_re, f_im


def _s5_param_body(lr_ref, li_ref, ldt_ref, lre_ref, lie_ref, bre_ref, bim_ref, are_ref, aim_ref, bbre_ref, bbim_ref):
    dt = jnp.exp(ldt_ref[...])
    ab_re, ab_im, _, _ = _s5_disc(lr_ref[...], li_ref[...], dt)
    are_ref[...] = ab_re
    aim_ref[...] = ab_im
    _, _, f_re, f_im = _s5_disc(lre_ref[...], lie_ref[...], dt)
    br, bim = bre_ref[...], bim_ref[...]
    bbre_ref[...] = f_re * br - f_im * bim
    bbim_ref[...] = f_re * bim + f_im * br


def _s5_params(lam_re, lam_im, log_dt, b_re, b_im):
    G, P, C = b_re.shape
    rep = lambda a: jnp.repeat(a, C, axis=1)
    shapes = [jax.ShapeDtypeStruct((G, P), F32)] * 2 + [jax.ShapeDtypeStruct((G, P * C), F32)] * 2
    return pl.pallas_call(_s5_param_body, out_shape=shapes, name="s5_discretise")(
        lam_re, lam_im, log_dt.reshape(G, 1), rep(lam_re), rep(lam_im), b_re.reshape(G, P * C), b_im.reshape(G, P * C))


def _s5_body(u_ref, h0r_ref, h0i_ref, are_ref, aim_ref, bre_ref, bim_ref, cre_ref, cim_ref, d_ref, wg_ref, bg_ref,
             y_ref, sr_ref, si_ref, hr_ref, hi_ref, xr_ref, xi_ref, *, tt):
    i = pl.program_id(0)

    @pl.when(i == 0)
    def _():
        hr_ref[...] = h0r_ref[...]
        hi_ref[...] = h0i_ref[...]

    u3 = u_ref[...]
    B, W = u3.shape[1], u3.shape[2]
    uf = u3.reshape(tt * B, W)
    y_parts = []
    for ck in range(S5_CHUNKS):
        uc = uf[:, ck * S5_CH_IN:(ck + 1) * S5_CH_IN].astype(BF16)
        xr_ref[...] = _dot(uc, bre_ref[ck]).reshape(tt, B, S5_CH_ST)
        xi_ref[...] = _dot(uc, bim_ref[ck]).reshape(tt, B, S5_CH_ST)
        lanes = slice(ck * S5_CH_ST, (ck + 1) * S5_CH_ST)
        a_re = jnp.broadcast_to(are_ref[:, lanes], (B, S5_CH_ST))
        a_im = jnp.broadcast_to(aim_ref[:, lanes], (B, S5_CH_ST))

        def step(t, h):
            h_re, h_im = h
            n_re = a_re * h_re - a_im * h_im + xr_ref[t]
            n_im = a_re * h_im + a_im * h_re + xi_ref[t]
            xr_ref[t] = n_re
            xi_ref[t] = n_im
            return n_re, n_im

        h_re, h_im = lax.fori_loop(0, tt, step, (hr_ref[:, lanes], hi_ref[:, lanes]))
        hr_ref[:, lanes] = h_re
        hi_ref[:, lanes] = h_im
        hre = xr_ref[...].reshape(tt * B, S5_CH_ST).astype(BF16)
        him = xi_ref[...].reshape(tt * B, S5_CH_ST).astype(BF16)
        y_parts.append(_dot(hre, cre_ref[ck]) - _dot(him, cim_ref[ck]))
    y = jnp.concatenate(y_parts, axis=1) + d_ref[...] * uf
    g = _gelu(y)
    out = g * jax.nn.sigmoid(_dot(g.astype(BF16), wg_ref[...]) + bg_ref[...])
    y_ref[...] = out.reshape(tt, B, W)
    sr_ref[...] = hr_ref[...]
    si_ref[...] = hi_ref[...]


def _s5(u, h0_re, h0_im, prm, tt):
    T, B, W = u.shape
    full2 = lambda shp: pl.BlockSpec(shp, lambda i: (0, 0))
    full3 = lambda shp: pl.BlockSpec(shp, lambda i: (0, 0, 0))
    blk = pl.BlockSpec((tt, B, W), lambda i: (i, 0, 0))
    st = jax.ShapeDtypeStruct((B, S5_NSTATE), F32)
    return pl.pallas_call(
        functools.partial(_s5_body, tt=tt),
        grid=(T // tt,),
        in_specs=[blk, full2((B, S5_NSTATE)), full2((B, S5_NSTATE)), full2((1, S5_NSTATE)), full2((1, S5_NSTATE)),
                  full3((S5_CHUNKS, S5_CH_IN, S5_CH_ST)), full3((S5_CHUNKS, S5_CH_IN, S5_CH_ST)),
                  full3((S5_CHUNKS, S5_CH_ST, S5_CH_IN)), full3((S5_CHUNKS, S5_CH_ST, S5_CH_IN)),
                  full2((1, W)), full2((W, W)), full2((1, W))],
        out_specs=[blk, full2((B, S5_NSTATE)), full2((B, S5_NSTATE))],
        out_shape=[jax.ShapeDtypeStruct((T, B, W), F32), st, st],
        scratch_shapes=[pltpu.VMEM((B, S5_NSTATE), F32), pltpu.VMEM((B, S5_NSTATE), F32),
                        pltpu.VMEM((tt, B, S5_CH_ST), F32), pltpu.VMEM((tt, B, S5_CH_ST), F32)],
        compiler_params=_cp(("arbitrary",)),
        name="s5",
    )(u, h0_re, h0_im, prm['a_re'], prm['a_im'], prm['b_re'], prm['b_im'], prm['c_re'], prm['c_im'],
      prm['d'], prm['w_glu'], prm['b_glu'])


def _s5_prepare(lam_re, lam_im, log_dt, b_re, b_im, c_re, c_im, d, w_glu, b_glu):
    a_re, a_im, bb_re, bb_im = _s5_params(lam_re, lam_im, log_dt, b_re, b_im)
    gpc = S5_GROUPS // S5_CHUNKS
    eye = jnp.eye(gpc, dtype=F32)

    def in_mat(bb):
        bb = bb.reshape(S5_CHUNKS, gpc, S5_STATE, S5_GROUP_CH)
        return jnp.einsum('kgpc,gh->kgchp', bb, eye).reshape(S5_CHUNKS, S5_CH_IN, S5_CH_ST).astype(BF16)

    def out_mat(c):
        c = c.reshape(S5_CHUNKS, gpc, S5_GROUP_CH, S5_STATE)
        return jnp.einsum('kgcp,gh->kgphc', c, eye).reshape(S5_CHUNKS, S5_CH_ST, S5_CH_IN).astype(BF16)

    return {'a_re': a_re.reshape(1, S5_NSTATE), 'a_im': a_im.reshape(1, S5_NSTATE),
            'b_re': in_mat(bb_re), 'b_im': in_mat(bb_im), 'c_re': out_mat(c_re), 'c_im': out_mat(c_im),
            'd': d.reshape(1, S5_WIDTH), 'w_glu': w_glu.astype(BF16), 'b_glu': b_glu.reshape(1, S5_WIDTH)}


FF_CHUNK = 896


def _ffn_up_body(x_ref, sc_ref, sh_ref, w_ref, cw_ref, cb_ref, prev0_ref, a_ref, buf_ref, prev_ref, *, tt):
    i = pl.program_id(0)

    @pl.when(i == 0)
    def _():
        prev_ref[...] = prev0_ref[...]

    x = x_ref[...]
    B, D = x.shape[1], x.shape[2]
    h = (x * (1.0 + sc_ref[...]) + sh_ref[...]).reshape(tt * B, D).astype(BF16)
    for c0 in range(0, D_FF, FF_CHUNK):
        cols = slice(c0, c0 + FF_CHUNK)
        g = _dot(h, w_ref[:, cols]).reshape(tt, B, FF_CHUNK)
        v = _dot(h, w_ref[:, D_FF + c0:D_FF + c0 + FF_CHUNK]).reshape(tt, B, FF_CHUNK)
        gp = jnp.concatenate([prev_ref[:, :, cols], g], axis=0)
        y = gp[0:tt] * cw_ref[0:1, cols]
        for j in range(1, FFN_CONV):
            y = y + gp[j:j + tt] * cw_ref[j:j + 1, cols]
        y = y + cb_ref[:, cols]
        prev_ref[:, :, cols] = gp[tt:tt + FFN_CONV - 1]
        a_ref[:, :, cols] = (_gelu(y) * v).astype(BF16)
    buf_ref[...] = prev_ref[...]


def _ffn_up(x, sc, sh, w_up, conv_w, conv_b, prev0, tt):
    T, B, D = x.shape
    full2 = lambda shp: pl.BlockSpec(shp, lambda i: (0, 0))
    return pl.pallas_call(
        functools.partial(_ffn_up_body, tt=tt),
        grid=(T // tt,),
        in_specs=[pl.BlockSpec((tt, B, D), lambda i: (i, 0, 0)), full2((B, D)), full2((B, D)),
                  full2((D, 2 * D_FF)), full2((FFN_CONV, D_FF)), full2((1, D_FF)),
                  pl.BlockSpec((FFN_CONV - 1, B, D_FF), lambda i: (0, 0, 0))],
        out_specs=[pl.BlockSpec((tt, B, D_FF), lambda i: (i, 0, 0)),
                   pl.BlockSpec((FFN_CONV - 1, B, D_FF), lambda i: (0, 0, 0))],
        out_shape=[jax.ShapeDtypeStruct((T, B, D_FF), BF16), jax.ShapeDtypeStruct((FFN_CONV - 1, B, D_FF), F32)],
        scratch_shapes=[pltpu.VMEM((FFN_CONV - 1, B, D_FF), F32)],
        compiler_params=_cp(("arbitrary",)),
        name="ffn_up_conv",
    )(x, sc, sh, w_up, conv_w, conv_b.reshape(1, D_FF), prev0)


CMP_HALF = 384
N_PAST_CMP = PAST_LEN // CMP_BLOCK
N_CMP_ALL = N_PAST_CMP + 2
N_PAST_SEL = PAST_LEN // SEL_BLOCK


def _softmax_rows(s, mask):
    if mask is not None:
        s = jnp.where(mask, s, NEG)
    m = jnp.max(s, axis=1, keepdims=True)
    e = jnp.exp(s - m)
    if mask is not None:
        e = jnp.where(mask, e, 0.0)
    den = jnp.sum(e, axis=1, keepdims=True)
    return e / jnp.where(den > 0, den, 1.0)


def _own_half(o):
    row = lax.broadcasted_iota(jnp.int32, (N_HEADS, HEAD_DIM), 0)
    return jnp.where(row < GROUP, o[:, 0:HEAD_DIM], o[:, HEAD_DIM:2 * HEAD_DIM])


def _cmp_sample_body(q_ref, kcv_ref, o_ref, imp_ref):
    q = (q_ref[0] * SCALE).astype(BF16)
    kc = kcv_ref[0, :, 0:LANES].astype(BF16)
    vc = kcv_ref[0, :, LANES:2 * LANES].astype(BF16)
    s = _dot_nt(q, kc)
    lane = lax.broadcasted_iota(jnp.int32, (N_HEADS, 2 * CMP_HALF), 1)
    pos = lane % CMP_HALF
    blk = 2 * pos + lane // CMP_HALF
    vis = (blk < N_CMP_ALL) & ((blk + 1) * CMP_BLOCK - 1 <= PAST_LEN)
    p = _softmax_rows(s, vis)
    o_ref[0] = _own_half(_dot(p.astype(BF16), vc))
    row = lax.broadcasted_iota(jnp.int32, (N_HEADS, 2 * CMP_HALF), 0)
    g0 = jnp.sum(jnp.where(row < GROUP, p, 0.0), axis=0, keepdims=True)
    g1 = jnp.sum(jnp.where(row >= GROUP, p, 0.0), axis=0, keepdims=True)
    gs = jnp.concatenate([g0, g1], axis=0)
    imp_ref[0] = gs[:, 0:CMP_HALF] + gs[:, CMP_HALF:2 * CMP_HALF]


def _cmp_sample(qpad, kcv_all):
    B = qpad.shape[0]
    return pl.pallas_call(
        _cmp_sample_body,
        grid=(B,),
        in_specs=[pl.BlockSpec((1, N_HEADS, LANES), lambda b: (b, 0, 0)),
                  pl.BlockSpec((1, 2 * CMP_HALF, KV_COLS), lambda b: (b, 0, 0))],
        out_specs=[pl.BlockSpec((1, N_HEADS, HEAD_DIM), lambda b: (b, 0, 0)),
                   pl.BlockSpec((1, N_KV, CMP_HALF), lambda b: (b, 0, 0))],
        out_shape=[jax.ShapeDtypeStruct((B, N_HEADS, HEAD_DIM), F32), jax.ShapeDtypeStruct((B, N_KV, CMP_HALF), F32)],
        compiler_params=_cp(("arbitrary",)),
        name="nsa_sample_cmp",
    )(qpad, kcv_all)


def _topk_sample_body(imp_ref, idx_ref):
    imp = imp_ref[...]
    j = lax.broadcasted_iota(jnp.int32, imp.shape, 0)
    cur = N_PAST_SEL
    forced = (j == 0) | (j == cur) | (j == cur - 1)
    score = jnp.where(forced, FORCE_SCORE, jnp.where(j <= cur, imp, NEG))
    rows = []
    for _ in range(N_SEL):
        top = jnp.max(score, axis=0, keepdims=True)
        idx = jnp.min(jnp.where(score == top, j, CMP_HALF), axis=0, keepdims=True)
        rows.append(idx)
        score = jnp.where(j == idx, -jnp.inf, score)
    idx_ref[...] = jnp.concatenate(rows, axis=0)


def _topk_sample(imp_t):
    return pl.pallas_call(
        _topk_sample_body,
        out_shape=jax.ShapeDtypeStruct((N_SEL, imp_t.shape[1]), jnp.int32),
        name="nsa_sample_topk",
    )(imp_t)


def _selwin_sample_body(idx_ref, phys_ref, pool_ref, q_ref, new_ref, win_ref, bsel_ref, bwin_ref, gate_ref, ocmp_ref,
                        o_ref, g_ref, bias_ref, sem):
    b = pl.program_id(0)
    nb = pl.num_programs(0)
    per_req = N_KV * N_SEL

    def block_copy(req, slot, k, n):
        src = phys_ref[req * per_req + k * N_SEL + n]
        return pltpu.make_async_copy(pool_ref.at[src], g_ref.at[slot, k, pl.ds(n * SEL_BLOCK, SEL_BLOCK)],
                                     sem.at[slot, k * N_SEL + n])

    def start_all(req, slot):
        for k in range(N_KV):
            for n in range(N_SEL):
                block_copy(req, slot, k, n).start()

    @pl.when(b == 0)
    def _():
        start_all(0, 0)

    @pl.when(b + 1 < nb)
    def _():
        start_all(b + 1, (b + 1) % 2)

    slot = b % 2
    for k in range(N_KV):
        for n in range(N_SEL):
            block_copy(b, slot, k, n).wait()

    q = (q_ref[0] * SCALE).astype(BF16)
    lane = lax.broadcasted_iota(jnp.int32, (1, N_SEL * SEL_BLOCK), 1)
    o_sel = []
    for k in range(N_KV):
        nvec = jnp.zeros((1, N_SEL * SEL_BLOCK), jnp.int32)
        for n in range(N_SEL):
            blk = idx_ref[b * per_req + k * N_SEL + n]
            nvec = jnp.where(lane // SEL_BLOCK == n, blk, nvec)
            bias_ref[:, n * SEL_BLOCK:(n + 1) * SEL_BLOCK] = bsel_ref[blk]

            @pl.when(blk == N_PAST_SEL)
            def _():
                g_ref[slot, k, pl.ds(n * SEL_BLOCK, 1), :] = new_ref[0]

        dist = PAST_LEN - (nvec * SEL_BLOCK + lane % SEL_BLOCK)
        mask = (nvec <= N_PAST_SEL) & (dist >= 0)
        kt = g_ref[slot, k, :, 0:LANES].astype(BF16)
        vt = g_ref[slot, k, :, LANES:2 * LANES].astype(BF16)
        p = _softmax_rows(_dot_nt(q, kt) + bias_ref[...], mask)
        o_sel.append(_dot(p.astype(BF16), vt))
    row = lax.broadcasted_iota(jnp.int32, (N_HEADS, HEAD_DIM), 0)
    o_slc = jnp.where(row < GROUP, o_sel[0][:, 0:HEAD_DIM], o_sel[1][:, HEAD_DIM:2 * HEAD_DIM])
    kw = win_ref[0, :, 0:LANES].astype(BF16)
    vw = win_ref[0, :, LANES:2 * LANES].astype(BF16)
    pw = _softmax_rows(_dot_nt(q, kw) + bwin_ref[...], None)
    o_win = _own_half(_dot(pw.astype(BF16), vw))
    g = jax.nn.sigmoid(gate_ref[0])
    o_ref[0] = g[:, 0:1] * ocmp_ref[0] + g[:, 1:2] * o_slc + g[:, 2:3] * o_win


def _selwin_sample(idx_flat, phys_flat, pool_blocks, qpad, new_row, win_new, bias_sel, bias_win, gates, o_cmp):
    B = qpad.shape[0]
    nk = N_SEL * SEL_BLOCK
    spec3 = lambda shp: pl.BlockSpec(shp, lambda b, *_: (b, 0, 0))
    grid_spec = pltpu.PrefetchScalarGridSpec(
        num_scalar_prefetch=2,
        grid=(B,),
        in_specs=[pl.BlockSpec(memory_space=pl.ANY),
                  spec3((1, N_HEADS, LANES)), spec3((1, 1, KV_COLS)), spec3((1, NSA_WINDOW, KV_COLS)),
                  pl.BlockSpec((N_PAST_SEL + 1, N_HEADS, SEL_BLOCK), lambda b, *_: (0, 0, 0)),
                  pl.BlockSpec((N_HEADS, NSA_WINDOW), lambda b, *_: (0, 0)),
                  spec3((1, N_HEADS, 3)), spec3((1, N_HEADS, HEAD_DIM))],
        out_specs=spec3((1, N_HEADS, HEAD_DIM)),
        scratch_shapes=[pltpu.VMEM((2, N_KV, nk, KV_COLS), F32), pltpu.VMEM((N_HEADS, nk), F32),
                        pltpu.SemaphoreType.DMA((2, N_KV * N_SEL))],
    )
    return pl.pallas_call(
        _selwin_sample_body,
        grid_spec=grid_spec,
        out_shape=jax.ShapeDtypeStruct((B, N_HEADS, HEAD_DIM), F32),
        compiler_params=_cp(("arbitrary",)),
        name="nsa_sample_sel_win",
    )(idx_flat, phys_flat, pool_blocks, qpad, new_row, win_new, bias_sel, bias_win, gates, o_cmp)


def _swa_sample_body(q_ref, kv_ref, bias_ref, sink_ref, o_ref):
    q = (q_ref[0] * SCALE).astype(BF16)
    kt = kv_ref[0, :, 0:LANES].astype(BF16)
    vt = kv_ref[0, :, LANES:2 * LANES].astype(BF16)
    s = _dot_nt(q, kt) + bias_ref[...]
    sink = sink_ref[...]
    m = jnp.maximum(jnp.max(s, axis=1, keepdims=True), sink)
    e = jnp.exp(s - m)
    den = jnp.sum(e, axis=1, keepdims=True) + jnp.exp(sink - m)
    p = e / jnp.where(den > 0, den, 1.0)
    o_ref[0] = _own_half(_dot(p.astype(BF16), vt))


def _swa_sample(qpad, kv_new, bias, sinks):
    B = qpad.shape[0]
    return pl.pallas_call(
        _swa_sample_body,
        grid=(B,),
        in_specs=[pl.BlockSpec((1, N_HEADS, LANES), lambda b: (b, 0, 0)),
                  pl.BlockSpec((1, SWA_WINDOW, KV_COLS), lambda b: (b, 0, 0)),
                  pl.BlockSpec((N_HEADS, SWA_WINDOW), lambda b: (0, 0)),
                  pl.BlockSpec((N_HEADS, 1), lambda b: (0, 0))],
        out_specs=pl.BlockSpec((1, N_HEADS, HEAD_DIM), lambda b: (b, 0, 0)),
        out_shape=jax.ShapeDtypeStruct((B, N_HEADS, HEAD_DIM), F32),
        compiler_params=_cp(("arbitrary",)),
        name="swa_sample",
    )(qpad, kv_new, bias, sinks)


def _pad_queries_sample(q):
    B = q.shape[0]
    qh = q.reshape(B, N_KV, GROUP, HEAD_DIM)
    z = jnp.zeros_like(qh[:, 0])
    return jnp.concatenate([jnp.concatenate([qh[:, 0], z], axis=-1), jnp.concatenate([z, qh[:, 1]], axis=-1)], axis=1)


def _split_mod(m, rows):
    sh, sc, gt = m[rows, 0:D_MODEL], m[rows, D_MODEL:2 * D_MODEL], m[rows, 2 * D_MODEL:]
    return sh, sc, gt


def _ffn(x_tm, mod, rows, B, T, w_up, conv_w, conv_b, w_down, prev0, ln_g, ln_b, tt, tm):
    sh, sc, gt = _split_mod(mod, rows)
    a, buf = _ffn_up(x_tm.reshape(T, B, D_MODEL), sc, sh, w_up, conv_w, conv_b, prev0, tt)
    gate = gt[:, None, :] if T > 1 else gt[None]
    nb, nt = (B, T) if T > 1 else (1, B)
    x = _mm_res_ln([a.reshape(nt, nb * D_FF)], [True], [w_down], x_tm.reshape(nt, nb * D_MODEL), True, gate,
                   ln_g, ln_b, False, nb, nt, tm)
    return x, buf


def kernel(x_prompt, x_sample, cache_kv_cmp, cache_kv_slc, cache_kv_win, state_lru_conv, state_lru_h, cache_kv_swa, state_s5_re, state_s5_im, state_ffn_conv, page_table, c_prompt, c_sample, rel_bias, w_mod, b_mod, ln_g, ln_b, w_in_ab, w_out_ab, nsa_cmp_pe, nsa_cmp_w, lru_conv_w, lru_conv_b, lru_w_r, lru_b_r, lru_w_i, lru_b_i, lru_lambda, w_in_cd, w_out_cd, swa_sinks, s5_lambda_re, s5_lambda_im, s5_log_dt, s5_b_re, s5_b_im, s5_c_re, s5_c_im, s5_d, s5_w_glu, s5_b_glu, w_ffn_up, w_ffn_conv, b_ffn_conv, w_ffn_down):
    assert N_AB == 1 and DEPTH == 2
    Bp, T, D = x_prompt.shape
    Bs = x_sample.shape[0]
    n_pool = cache_kv_cmp.shape[0]
    TM = 512 if T % 512 == 0 else T
    TT_REC = 64 if T % 64 == 0 else T
    TT_S5 = 32 if T % 32 == 0 else T
    TT_FF = 32 if T % 32 == 0 else T
    prow, srow = slice(0, Bp), slice(Bp, Bp + Bs)

    mod = _ada_mod_all(jnp.concatenate([c_prompt, c_sample], axis=0),
                       w_mod.reshape(2 * DEPTH, D, 3 * D), b_mod.reshape(2 * DEPTH, 1, 3 * D))
    bnear, bfar = _bias_tiles(rel_bias)
    xs = x_sample.reshape(1, Bs, D)

    w = w_in_ab[0].astype(BF16)
    o = np.cumsum([0, N_HEADS * HEAD_DIM, KV_COLS, KV_COLS, KV_COLS, 3 * N_HEADS, LRU_WIDTH, LRU_WIDTH])
    w_gates = jnp.pad(w[:, o[4]:o[5]], ((0, 0), (0, LANES - 3 * N_HEADS)))
    ws = [w[:, o[0]:o[1]], w[:, o[1]:o[2]], w[:, o[2]:o[3]], w[:, o[3]:o[4]], w_gates, w[:, o[5]:o[6]], w[:, o[6]:o[7]]]
    out_tm = [False, False, False, False, False, True, True]
    w_full, pe_rows = _compress_weights(nsa_cmp_w[0], nsa_cmp_pe[0])
    lru = {'conv_w': lru_conv_w[0], 'conv_b': lru_conv_b[0].reshape(1, -1),
           'w_r': _block_diag(lru_w_r[0]).astype(BF16), 'b_r': lru_b_r[0].reshape(1, -1),
           'w_i': _block_diag(lru_w_i[0]).astype(BF16), 'b_i': lru_b_i[0].reshape(1, -1),
           'lam': lru_lambda[0].reshape(1, -1)}
    w_out = w_out_ab[0].astype(BF16)
    w_out_parts = [w_out[0:N_HEADS * HEAD_DIM], w_out[N_HEADS * HEAD_DIM:]]

    shp, scp, gtp = _split_mod(mod[0], prow)
    q, kvc, kvs, kvw, gates, xb, gb = _modmm(x_prompt, scp[:, None], shp[:, None], ws, out_tm, TM)
    nc = T // CMP_BLOCK
    kcv = _compress(kvc.reshape(Bp * nc, CMP_IN), w_full, pe_rows, min(256, Bp * nc)).reshape(Bp, nc, KV_COLS)
    kcv = jnp.concatenate([kcv[:, 0::2], kcv[:, 1::2]], axis=1)
    o_att = _nsa_prompt_attn(q, kvs, kvw, kcv[:, :, 0:LANES], jnp.swapaxes(kcv[:, :, LANES:], 1, 2), gates, bnear, bfar)
    y_lru, p_lru_buf, p_lru_h = _rglru(xb.reshape(T, Bp, LRU_WIDTH), gb.reshape(T, Bp, LRU_WIDTH),
                                       jnp.zeros((LRU_CONV - 1, Bp, LRU_WIDTH), F32), jnp.zeros((Bp, LRU_WIDTH), F32),
                                       lru, TT_REC)
    xp1 = _mm_res_ln([o_att, y_lru.reshape(T, Bp * LRU_WIDTH)], [False, True], w_out_parts, x_prompt, False,
                     gtp[:, None], ln_g[0, 0], ln_b[0, 0], True, Bp, T, TM)
    p_kv_cmp = kvc.reshape(Bp, T // PAGE_SIZE, PAGE_SIZE, 1, 2, N_KV, HEAD_DIM)
    p_kv_slc = kvs.reshape(Bp, T // PAGE_SIZE, PAGE_SIZE, 1, 2, N_KV, HEAD_DIM)
    wk = min(NSA_WINDOW, T)
    p_kv_win = kvw[:, T - wk:].reshape(Bp, 1, wk, 2, N_KV, HEAD_DIM)

    shs, scs, gts = _split_mod(mod[0], srow)
    q_s, kvc_s, kvs_s, kvw_s, gates_s, xb_s, gb_s = _modmm(xs, scs[None], shs[None], ws, out_tm, Bs)
    q_s, kvc_s, kvs_s, kvw_s, gates_s = q_s[0], kvc_s[0], kvs_s[0], kvw_s[0], gates_s[0]
    pool_rows = n_pool * (PAGE_SIZE // CMP_BLOCK)
    kcv_pool = _compress(cache_kv_cmp.reshape(pool_rows, CMP_IN), w_full, pe_rows, 256)
    tail = jnp.pad(kvc_s[:, None, :], ((0, 0), (0, SEL_BLOCK - 1), (0, 0))).reshape(Bs * 2, CMP_IN)
    kcv_tail = _compress(tail, w_full, pe_rows, Bs * 2)
    pos = jnp.arange(CMP_HALF, dtype=jnp.int32)
    blk = jnp.concatenate([2 * pos, 2 * pos + 1])
    per_page = PAGE_SIZE // CMP_BLOCK
    past_row = page_table[:, jnp.minimum(blk, N_PAST_CMP - 1) // per_page] * per_page + blk % per_page
    tail_row = pool_rows + 2 * jnp.arange(Bs, dtype=jnp.int32)[:, None] + jnp.clip(blk - N_PAST_CMP, 0, 1)
    src = jnp.where(blk < N_PAST_CMP, past_row, tail_row)
    kcv_all = jnp.concatenate([kcv_pool, kcv_tail], axis=0)[src]
    qpad_s = _pad_queries_sample(q_s)
    o_cmp_s, imp_s = _cmp_sample(qpad_s, kcv_all)
    idx_t = _topk_sample(imp_s.reshape(Bs * N_KV, CMP_HALF).T)
    idx = idx_t.T.reshape(Bs, N_KV, N_SEL)
    idx_p = jnp.minimum(idx, N_PAST_SEL - 1)
    per_page_s = PAGE_SIZE // SEL_BLOCK
    phys = page_table[jnp.arange(Bs)[:, None, None], idx_p // per_page_s] * per_page_s + idx_p % per_page_s
    kpos = jnp.arange((N_PAST_SEL + 1) * SEL_BLOCK, dtype=jnp.int32)
    bias_sel = rel_bias[_t5_bucket(PAST_LEN - kpos)].reshape(N_PAST_SEL + 1, SEL_BLOCK, N_HEADS)
    bias_sel = jnp.swapaxes(bias_sel, 1, 2)
    win_new = jnp.concatenate([cache_kv_win[:, 0, 1:].reshape(Bs, NSA_WINDOW - 1, KV_COLS), kvw_s[:, None]], axis=1)
    bias_win = rel_bias[_t5_bucket(NSA_WINDOW - 1 - jnp.arange(NSA_WINDOW, dtype=jnp.int32))].T
    gates3 = jnp.swapaxes(gates_s[:, 0:3 * N_HEADS].reshape(Bs, 3, N_HEADS), 1, 2)
    o_att_s = _selwin_sample(idx.reshape(-1), phys.reshape(-1).astype(jnp.int32),
                             cache_kv_slc.reshape(n_pool * per_page_s, SEL_BLOCK, KV_COLS), qpad_s,
                             kvs_s[:, None], win_new, bias_sel, bias_win, gates3, o_cmp_s)
    y_lru_s, s_lru_buf, s_lru_h = _rglru(xb_s.reshape(1, Bs, LRU_WIDTH), gb_s.reshape(1, Bs, LRU_WIDTH),
                                         jnp.swapaxes(state_lru_conv[:, 0], 0, 1), state_lru_h[:, 0], lru, 1)
    xs1 = _mm_res_ln([o_att_s.reshape(1, Bs, N_HEADS * HEAD_DIM), y_lru_s], [False, False], w_out_parts, xs, False,
                     gts[None], ln_g[0, 0], ln_b[0, 0], True, 1, Bs, Bs)

    ffn_w = [(w_ffn_up[li].astype(BF16), w_ffn_conv[li], b_ffn_conv[li], w_ffn_down[li].astype(BF16)) for li in range(DEPTH)]
    zero_ff = jnp.zeros((FFN_CONV - 1, Bp, D_FF), F32)
    xp2, p_ff0 = _ffn(xp1, mod[1], prow, Bp, T, *ffn_w[0], zero_ff, ln_g[0, 1], ln_b[0, 1], TT_FF, TM)
    xs2, s_ff0 = _ffn(xs1, mod[1], srow, Bs, 1, *ffn_w[0], jnp.swapaxes(state_ffn_conv[:, 0], 0, 1),
                      ln_g[0, 1], ln_b[0, 1], 1, Bs)

    w = w_in_cd[0].astype(BF16)
    ws = [w[:, 0:N_HEADS * HEAD_DIM], w[:, N_HEADS * HEAD_DIM:N_HEADS * HEAD_DIM + KV_COLS], w[:, N_HEADS * HEAD_DIM + KV_COLS:]]
    out_tm = [False, False, True]
    s5p = _s5_prepare(s5_lambda_re[0], s5_lambda_im[0], s5_log_dt[0], s5_b_re[0], s5_b_im[0], s5_c_re[0], s5_c_im[0],
                      s5_d[0], s5_w_glu[0], s5_b_glu[0])
    w_out = w_out_cd[0].astype(BF16)
    w_out_parts = [w_out[0:N_HEADS * HEAD_DIM], w_out[N_HEADS * HEAD_DIM:]]
    sink_rows = jnp.repeat(swa_sinks[0].reshape(N_KV, 1, GROUP), TQ, axis=2)

    shp, scp, gtp = _split_mod(mod[2], prow)
    q, kv, u = _modmm(xp2, scp[:, None], shp[:, None], ws, out_tm, TM)
    o_att = _swa_prompt_attn(q, kv, sink_rows, bnear, bfar)
    zst = jnp.zeros((Bp, S5_NSTATE), F32)
    y_s5, p_s5_re, p_s5_im = _s5(u.reshape(T, Bp, S5_WIDTH), zst, zst, s5p, TT_S5)
    xp3 = _mm_res_ln([o_att, y_s5.reshape(T, Bp * S5_WIDTH)], [False, True], w_out_parts, xp2, False,
                     gtp[:, None], ln_g[1, 0], ln_b[1, 0], True, Bp, T, TM)
    wk = min(SWA_WINDOW, T)
    p_kv_swa = kv[:, T - wk:].reshape(Bp, 1, wk, 2, N_KV, HEAD_DIM)

    shs, scs, gts = _split_mod(mod[2], srow)
    q_s, kv_s, u_s = _modmm(xs2, scs[None], shs[None], ws, out_tm, Bs)
    swa_new = jnp.concatenate([cache_kv_swa[:, 0, 1:].reshape(Bs, SWA_WINDOW - 1, KV_COLS), jnp.swapaxes(kv_s, 0, 1)], axis=1)
    bias_swa = rel_bias[_t5_bucket(SWA_WINDOW - 1 - jnp.arange(SWA_WINDOW, dtype=jnp.int32))].T
    o_att_s = _swa_sample(_pad_queries_sample(q_s[0]), swa_new, bias_swa, swa_sinks[0].reshape(N_HEADS, 1))
    y_s5_s, s_s5_re, s_s5_im = _s5(u_s.reshape(1, Bs, S5_WIDTH), state_s5_re[:, 0].reshape(Bs, S5_NSTATE),
                                   state_s5_im[:, 0].reshape(Bs, S5_NSTATE), s5p, 1)
    xs3 = _mm_res_ln([o_att_s.reshape(1, Bs, N_HEADS * HEAD_DIM), y_s5_s], [False, False], w_out_parts, xs2, False,
                     gts[None], ln_g[1, 0], ln_b[1, 0], True, 1, Bs, Bs)

    xp4, p_ff1 = _ffn(xp3, mod[3], prow, Bp, T, *ffn_w[1], zero_ff, ln_g[1, 1], ln_b[1, 1], TT_FF, TM)
    xs4, s_ff1 = _ffn(xs3, mod[3], srow, Bs, 1, *ffn_w[1], jnp.swapaxes(state_ffn_conv[:, 1], 0, 1),
                      ln_g[1, 1], ln_b[1, 1], 1, Bs)

    kv6 = lambda a, n: a.reshape(n, 1, 1, 2, N_KV, HEAD_DIM)
    return (xp4, xs4.reshape(Bs, 1, D),
            p_kv_cmp, p_kv_slc, p_kv_win,
            jnp.swapaxes(p_lru_buf, 0, 1)[:, None], p_lru_h[:, None], p_kv_swa,
            p_s5_re.reshape(Bp, 1, S5_GROUPS, S5_STATE), p_s5_im.reshape(Bp, 1, S5_GROUPS, S5_STATE),
            jnp.stack([jnp.swapaxes(p_ff0, 0, 1), jnp.swapaxes(p_ff1, 0, 1)], axis=1),
            kv6(kvc_s, Bs), kv6(kvs_s, Bs), win_new.reshape(Bs, 1, NSA_WINDOW, 2, N_KV, HEAD_DIM),
            jnp.swapaxes(s_lru_buf, 0, 1)[:, None], s_lru_h[:, None],
            swa_new.reshape(Bs, 1, SWA_WINDOW, 2, N_KV, HEAD_DIM),
            s_s5_re.reshape(Bs, 1, S5_GROUPS, S5_STATE), s_s5_im.reshape(Bs, 1, S5_GROUPS, S5_STATE),
            jnp.stack([jnp.swapaxes(s_ff0, 0, 1), jnp.swapaxes(s_ff1, 0, 1)], axis=1))
```

```python
import functools
import math

import numpy as np
import jax
import jax.numpy as jnp
from jax import lax
from jax.experimental import pallas as pl
from jax.experimental.pallas import tpu as pltpu

D_MODEL = 1024
DEPTH = 2
PAST_LEN = 16384
PAGE_SIZE = 128
HEAD_DIM = 64
N_AB = (DEPTH + 1) // 2
N_HEADS = 8
N_KV = 2
GROUP = N_HEADS // N_KV
CMP_BLOCK = 32
SEL_BLOCK = 64
N_SEL = 16
NSA_WINDOW = 512
LRU_WIDTH = D_MODEL // 2
LRU_BLOCKS = 8
LRU_CONV = 4
LRU_C = 8.0
SWA_WINDOW = 128
S5_WIDTH = D_MODEL // 2
S5_GROUP_CH = 16
S5_GROUPS = S5_WIDTH // S5_GROUP_CH
S5_STATE = 64
D_FF = (8 * D_MODEL // 3) // 128 * 128
FFN_CONV = 3
N_BUCKETS = 32
MAX_DISTANCE = 128
DN_ALPHA = (2 * DEPTH) ** 0.25
LN_EPS = 1e-5
SCALE = HEAD_DIM ** -0.5
NEG = -1e30
FORCE_SCORE = 1e4
KV_COLS = 2 * N_KV * HEAD_DIM
CMP_IN = CMP_BLOCK * KV_COLS
F32 = jnp.float32
BF16 = jnp.bfloat16

LANES = 128
TQ = 128
VMEM_LIMIT = 56 * 1024 * 1024


def _cp(sem, vmem=VMEM_LIMIT):
    return pltpu.CompilerParams(dimension_semantics=sem, vmem_limit_bytes=vmem)


def _dot(a, b):
    return jnp.dot(a, b, preferred_element_type=F32)


def _dot_nt(a, b):
    return lax.dot_general(a, b, (((1,), (1,)), ((), ())), preferred_element_type=F32)


def _gelu(x):
    cdf = 0.5 * (1.0 + jnp.tanh(math.sqrt(2.0 / math.pi) * (x + 0.044715 * (x * x * x))))
    return x * cdf


def _softplus(x):
    return jnp.maximum(x, 0.0) + jnp.log1p(jnp.exp(-jnp.abs(x)))


def _expm1(x):
    u = jnp.exp(x)
    um1 = u - 1.0
    edge = (u == 1.0) | (um1 == -1.0)
    r = um1 * x / jnp.log(jnp.where(edge, 2.0, u))
    return jnp.where(u == 1.0, x, jnp.where(um1 == -1.0, -1.0, r))


def _t5_bucket(dist):
    n = jnp.maximum(dist, 0)
    max_exact = N_BUCKETS // 2
    nf = jnp.maximum(n, 1).astype(F32)
    large = max_exact + (jnp.log(nf / max_exact) / math.log(MAX_DISTANCE / max_exact) * (N_BUCKETS - max_exact)).astype(jnp.int32)
    return jnp.where(n < max_exact, n, jnp.minimum(large, N_BUCKETS - 1))


def _bucket_saturates_from(n0, n1):
    n = np.arange(n0, n1, dtype=np.float64)
    b = 16 + np.floor(np.log(n / 16.0) / math.log(8.0) * 16.0)
    return bool(np.all(b >= N_BUCKETS - 1 + 0.5))


assert _bucket_saturates_from(TQ + 1, PAST_LEN + 2 * TQ)


def _mod_body(c_ref, w_ref, b_ref, o_ref):
    c = c_ref[...]
    h = (c * jax.nn.sigmoid(c)).astype(BF16)
    o_ref[0] = _dot(h, w_ref[0].astype(BF16)) + b_ref[0]


def _ada_mod_all(c_all, w_mod, b_mod):
    n_sub = w_mod.shape[0]
    rows = c_all.shape[0]
    tn = 1024
    return pl.pallas_call(
        _mod_body,
        grid=(n_sub, 3 * D_MODEL // tn),
        in_specs=[pl.BlockSpec((rows, D_MODEL), lambda s, n: (0, 0)),
                  pl.BlockSpec((1, D_MODEL, tn), lambda s, n: (s, 0, n)),
                  pl.BlockSpec((1, 1, tn), lambda s, n: (s, 0, n))],
        out_specs=pl.BlockSpec((1, rows, tn), lambda s, n: (s, 0, n)),
        out_shape=jax.ShapeDtypeStruct((n_sub, rows, 3 * D_MODEL), F32),
        compiler_params=_cp(("arbitrary", "arbitrary")),
        name="ada_mod",
    )(c_all, w_mod, b_mod)


def _modmm_body(x_ref, sc_ref, sh_ref, *refs, w_of_out, kinds):
    n_w = max(w_of_out) + 1
    h = (x_ref[0] * (1.0 + sc_ref[0]) + sh_ref[0]).astype(BF16)
    res = {}
    for o_ref, wi, kind in zip(refs[n_w:], w_of_out, kinds):
        if wi not in res:
            res[wi] = _dot(h, refs[wi][...])
        r = res[wi]
        if kind == 'tm':
            o_ref[...] = r
        elif kind == 'bm':
            o_ref[0] = r
        else:
            for p in range(r.shape[0] // PAGE_SIZE):
                o_ref[0, p] = r[p * PAGE_SIZE:(p + 1) * PAGE_SIZE, :].T


def _modmm(x, sc, sh, outs, tm):
    B, T, D = x.shape
    per_row = sc.shape[1] != 1
    mod_spec = pl.BlockSpec((1, tm if per_row else 1, D), (lambda b, t: (b, t, 0)) if per_row else (lambda b, t: (b, 0, 0)))
    in_specs = [pl.BlockSpec((1, tm, D), lambda b, t: (b, t, 0)), mod_spec, mod_spec]
    out_specs, out_shape, w_of_out, kinds = [], [], [], []
    for wi, (w, ks) in enumerate(outs):
        n = w.shape[1]
        in_specs.append(pl.BlockSpec((D, n), lambda b, t: (0, 0)))
        for kind in ks:
            w_of_out.append(wi)
            kinds.append(kind)
            if kind == 'tm':
                out_specs.append(pl.BlockSpec((tm, n), lambda b, t: (t, b)))
                out_shape.append(jax.ShapeDtypeStruct((T, B * n), F32))
            elif kind == 'bm':
                out_specs.append(pl.BlockSpec((1, tm, n), lambda b, t: (b, t, 0)))
                out_shape.append(jax.ShapeDtypeStruct((B, T, n), F32))
            else:
                out_specs.append(pl.BlockSpec((1, tm // PAGE_SIZE, n, PAGE_SIZE), lambda b, t: (b, t, 0, 0)))
                out_shape.append(jax.ShapeDtypeStruct((B, T // PAGE_SIZE, n, PAGE_SIZE), F32))
    return pl.pallas_call(
        functools.partial(_modmm_body, w_of_out=tuple(w_of_out), kinds=tuple(kinds)),
        grid=(B, T // tm),
        in_specs=in_specs, out_specs=out_specs, out_shape=out_shape,
        compiler_params=_cp(("arbitrary", "arbitrary")),
        name="mod_proj",
    )(x, sc, sh, *[w for w, _ in outs])


def _mm_res_ln_body(*refs, n_a, a_tm, x_tm, o_tm):
    a_refs, w_refs = refs[:n_a], refs[n_a:2 * n_a]
    x_ref, gt_ref, g_ref, b_ref, o_ref = refs[2 * n_a:]
    y = None
    for a_ref, w_ref, tmj in zip(a_refs, w_refs, a_tm):
        a = (a_ref[...] if tmj else a_ref[0]).astype(BF16)
        r = _dot(a, w_ref[...])
        y = r if y is None else y + r
    x = x_ref[...] if x_tm else x_ref[0]
    z = DN_ALPHA * x + (1.0 + gt_ref[0]) * y
    mu = jnp.mean(z, axis=-1, keepdims=True)
    var = jnp.mean(jnp.square(z - mu), axis=-1, keepdims=True)
    out = (z - mu) * lax.rsqrt(var + LN_EPS) * g_ref[...] + b_ref[...]
    if o_tm:
        o_ref[...] = out
    else:
        o_ref[0] = out


def _mm_res_ln(a_list, a_tm, ws, x, x_tm, gate, ln_g, ln_b, o_tm, B, T, tm):
    D = D_MODEL
    per_row = gate.shape[1] != 1

    def spec(k, tmj):
        if tmj:
            return pl.BlockSpec((tm, k), lambda b, t: (t, b))
        return pl.BlockSpec((1, tm, k), lambda b, t: (b, t, 0))

    in_specs = [spec(w.shape[0], tmj) for w, tmj in zip(ws, a_tm)]
    in_specs += [pl.BlockSpec(w.shape, lambda b, t: (0, 0)) for w in ws]
    in_specs += [spec(D, x_tm),
                 pl.BlockSpec((1, tm if per_row else 1, D), (lambda b, t: (b, t, 0)) if per_row else (lambda b, t: (b, 0, 0))),
                 pl.BlockSpec((1, D), lambda b, t: (0, 0)),
                 pl.BlockSpec((1, D), lambda b, t: (0, 0))]
    out_shape = jax.ShapeDtypeStruct((T, B * D) if o_tm else (B, T, D), F32)
    return pl.pallas_call(
        functools.partial(_mm_res_ln_body, n_a=len(ws), a_tm=tuple(a_tm), x_tm=x_tm, o_tm=o_tm),
        grid=(B, T // tm),
        in_specs=in_specs, out_specs=spec(D, o_tm), out_shape=out_shape,
        compiler_params=_cp(("arbitrary", "arbitrary")),
        name="proj_res_ln",
    )(*a_list, *ws, x, gate, ln_g.reshape(1, D), ln_b.reshape(1, D))


def _compress_body(x_ref, w_ref, pe_ref, o_ref):
    w = w_ref[...]
    pe_term = _dot(pe_ref[...].astype(BF16), w)[0:1]
    o_ref[...] = _dot(x_ref[...].astype(BF16), w) + pe_term


def _compress(x_rows, w_full, pe_rows, tr):
    R = x_rows.shape[0]
    return pl.pallas_call(
        _compress_body,
        grid=(R // tr,),
        in_specs=[pl.BlockSpec((tr, CMP_IN), lambda r: (r, 0)),
                  pl.BlockSpec((CMP_IN, KV_COLS), lambda r: (0, 0)),
                  pl.BlockSpec((8, CMP_IN), lambda r: (0, 0))],
        out_specs=pl.BlockSpec((tr, KV_COLS), lambda r: (r, 0)),
        out_shape=jax.ShapeDtypeStruct((R, KV_COLS), F32),
        compiler_params=_cp(("arbitrary",)),
        name="nsa_compress",
    )(x_rows, w_full, pe_rows)


def _compress_weights(cmp_w, cmp_pe):
    eye = jnp.eye(2, dtype=F32)
    wt = jnp.transpose(cmp_w, (1, 0, 2, 3))
    w_full = jnp.einsum('csde,st,kl->cskdtle', wt, eye, eye).reshape(CMP_IN, KV_COLS).astype(BF16)
    pe = jnp.transpose(cmp_pe, (1, 0, 2))[:, :, None, :]
    pe = jnp.broadcast_to(pe, (CMP_BLOCK, 2, N_KV, HEAD_DIM)).reshape(1, CMP_IN)
    return w_full, jnp.broadcast_to(pe, (8, CMP_IN))


POOL_PAGES = 128
POOL_PITCH = 264
BLK_PER_PAGE = PAGE_SIZE // CMP_BLOCK


def _compress_pool_body(pool_ref, w_ref, pe_ref, o_ref, buf_ref, sem):
    i = pl.program_id(0)
    n = pl.num_programs(0)
    P = POOL_PAGES

    def page_copy(step, slot, p):
        dst = pl.multiple_of((slot * P + p) * POOL_PITCH, 8)
        return pltpu.make_async_copy(pool_ref.at[step * P + p], buf_ref.at[pl.ds(dst, KV_COLS), :], sem.at[slot])

    def start_pages(step, slot):
        def go(p, c):
            page_copy(step, slot, p).start()
            return c
        lax.fori_loop(0, P, go, 0)

    @pl.when(i == 0)
    def _():
        start_pages(0, 0)

    @pl.when(i + 1 < n)
    def _():
        start_pages(i + 1, (i + 1) % 2)

    slot = i % 2

    def wait_page(p, c):
        page_copy(i, slot, p).wait()
        return c

    lax.fori_loop(0, P, wait_page, 0)
    base = slot * (P * POOL_PITCH)

    def feature_rows(f):
        return buf_ref[pl.ds(base + f, P, stride=POOL_PITCH), :]

    DC = 8
    for s in range(2):
        out = _dot(pe_ref[s].astype(BF16), w_ref[s])[0:1]
        for d0 in range(0, HEAD_DIM, DC):
            lhs = jnp.concatenate(
                [jnp.concatenate([feature_rows((s * N_KV + k) * HEAD_DIM + d) for d in range(d0, d0 + DC)], axis=1)
                 for k in range(N_KV)], axis=0).astype(BF16)
            out = out + _dot(lhs, w_ref[s, d0 * PAGE_SIZE:(d0 + DC) * PAGE_SIZE, :])
        for k in range(N_KV):
            o_ref[s * N_KV + k] = out[k * P:(k + 1) * P]


def _compress_pool(pool_t, w2, pe2):
    n_pool = pool_t.shape[0]
    kdim = HEAD_DIM * PAGE_SIZE
    ncol = BLK_PER_PAGE * HEAD_DIM
    return pl.pallas_call(
        _compress_pool_body,
        grid=(n_pool // POOL_PAGES,),
        in_specs=[pl.BlockSpec(memory_space=pl.ANY),
                  pl.BlockSpec((2, kdim, ncol), lambda i: (0, 0, 0), pipeline_mode=pl.Buffered(1)),
                  pl.BlockSpec((2, 8, kdim), lambda i: (0, 0, 0), pipeline_mode=pl.Buffered(1))],
        out_specs=pl.BlockSpec((2 * N_KV, POOL_PAGES, ncol), lambda i: (0, i, 0)),
        out_shape=jax.ShapeDtypeStruct((2 * N_KV, n_pool, ncol), F32),
        scratch_shapes=[pltpu.VMEM((2 * POOL_PAGES * POOL_PITCH, PAGE_SIZE), F32), pltpu.SemaphoreType.DMA((2,))],
        compiler_params=_cp(("arbitrary",)),
        name="nsa_compress_pool",
    )(pool_t, w2, pe2)


def _compress_pool_weights(cmp_w, cmp_pe):
    eye = jnp.eye(BLK_PER_PAGE, dtype=F32)
    w2 = jnp.einsum('scde,bB->sdbcBe', cmp_w, eye).reshape(2, HEAD_DIM * PAGE_SIZE, BLK_PER_PAGE * HEAD_DIM).astype(BF16)
    pe = jnp.transpose(cmp_pe, (0, 2, 1))[:, :, None, :]
    pe = jnp.broadcast_to(pe, (2, HEAD_DIM, BLK_PER_PAGE, CMP_BLOCK)).reshape(2, 1, HEAD_DIM * PAGE_SIZE)
    return w2, jnp.broadcast_to(pe, (2, 8, HEAD_DIM * PAGE_SIZE))


def _padded_queries(q):
    lo = lax.broadcasted_iota(jnp.int32, (TQ, LANES), 1) < HEAD_DIM
    out = []
    for h in range(N_HEADS):
        x = q[:, (h // 2) * LANES:(h // 2 + 1) * LANES]
        k = h // GROUP
        if h % 2 != k:
            x = pltpu.roll(x, HEAD_DIM, axis=1)
        out.append(jnp.where(lo if k == 0 else ~lo, x, 0.0).astype(BF16))
    return out


def _attend(carry, qg, kt, vt, bias, mask):
    m, l, acc = carry
    s = _dot_nt(kt, qg) + bias
    if mask is not None:
        s = jnp.where(mask, s, NEG)
    m_new = jnp.maximum(m, jnp.max(s, axis=0, keepdims=True))
    alpha = jnp.exp(m - m_new)
    p = jnp.exp(s - m_new)
    if mask is not None:
        p = jnp.where(mask, p, 0.0)
    l = alpha * l + jnp.sum(p, axis=0, keepdims=True)
    acc = alpha * acc + _dot(vt, p.astype(BF16))
    return m_new, l, acc


def _finish(carry):
    _, l, acc = carry
    return acc / jnp.where(l > 0, l, 1.0)


def _key_tile(kv_ref, kvt_ref, k, j):
    start = pl.multiple_of(j * TQ, TQ)
    kt = kv_ref[0, pl.ds(start, TQ), 0:LANES].astype(BF16)
    vt = kvt_ref[0, j, LANES + k * HEAD_DIM:LANES + (k + 1) * HEAD_DIM, :].astype(BF16)
    return kt, vt


def _attend_groups(carries, qgs, kv_ref, vt_ref, j, biases, masks):
    out = []
    for k in range(N_KV):
        kt, vt = _key_tile(kv_ref, vt_ref, k, j)
        out.append(_attend(carries[k], qgs[k], kt, vt, biases[k], masks[k]))
    return tuple(out)


def _window_branch(carries, qgs, kv_ref, vt_ref, i, n_win, bnear_ref, bfars, causal, lt_mask):
    for delta in range(n_win, 0, -1):
        near = delta == 1
        mask = lt_mask if delta == n_win else None

        def body(j, c, near=near, mask=mask):
            biases = [bnear_ref[1, k] for k in range(N_KV)] if near else bfars
            return _attend_groups(c, qgs, kv_ref, vt_ref, j, biases, [mask] * N_KV)

        has = i >= delta
        lo = jnp.where(has, i - delta, 0)
        carries = lax.fori_loop(lo, jnp.where(has, lo + 1, 0), body, carries)
    return _attend_groups(carries, qgs, kv_ref, vt_ref, i, [bnear_ref[0, k] for k in range(N_KV)], [causal] * N_KV)


def _store_heads(o_ref, o_groups):
    for k in range(N_KV):
        for pair in range(GROUP // 2):
            a = o_groups[k][:, (2 * pair) * TQ:(2 * pair + 1) * TQ]
            b = o_groups[k][:, (2 * pair + 1) * TQ:(2 * pair + 2) * TQ]
            col = (k * GROUP + 2 * pair) * HEAD_DIM
            o_ref[0, :, col:col + 2 * HEAD_DIM] = jnp.concatenate([a, b], axis=0).T


def _nsa_prompt_body(q_ref, kvs_ref, vst_ref, kvw_ref, vwt_ref, kc_ref, vct_ref, gates_ref, bnear_ref, bfar_ref, o_ref,
                     sel_ref, *, T):
    i = pl.program_id(1)
    nc = T // CMP_BLOCK
    nsel = nc // 2
    t0 = i * TQ
    qpads = _padded_queries(q_ref[0] * SCALE)
    gt = jax.nn.sigmoid(gates_ref[0]).T
    W = GROUP * TQ
    c_lane = lax.broadcasted_iota(jnp.int32, (1, W), 1) & (TQ - 1)
    tpos = t0 + c_lane
    r_key = lax.broadcasted_iota(jnp.int32, (TQ, W), 0)
    causal = c_lane >= r_key
    lt_mask = c_lane < r_key
    r_c = lax.broadcasted_iota(jnp.int32, (nc, W), 0)
    blk = 2 * (r_c % nsel) + r_c // nsel
    vis = (blk + 1) * CMP_BLOCK - 1 <= tpos
    j_s = lax.broadcasted_iota(jnp.int32, (nsel, TQ), 0)
    cur = (t0 + lax.broadcasted_iota(jnp.int32, (nsel, TQ), 1)) // SEL_BLOCK
    init = (jnp.full((1, W), NEG, F32), jnp.zeros((1, W), F32), jnp.zeros((HEAD_DIM, W), F32))
    qgs, o_cmps = [], []
    for k in range(N_KV):
        qg = jnp.concatenate(qpads[GROUP * k:GROUP * (k + 1)], axis=0)
        qgs.append(qg)
        s = jnp.where(vis, _dot_nt(kc_ref[0].astype(BF16), qg), NEG)
        m = jnp.max(s, axis=0, keepdims=True)
        e = jnp.where(vis, jnp.exp(s - m), 0.0)
        den = jnp.sum(e, axis=0, keepdims=True)
        p = e / jnp.where(den > 0, den, 1.0)
        o_cmps.append(_dot(vct_ref[0, k * HEAD_DIM:(k + 1) * HEAD_DIM, :].astype(BF16), p.astype(BF16)))
        ps = p[:, 0:TQ] + p[:, TQ:2 * TQ] + p[:, 2 * TQ:3 * TQ] + p[:, 3 * TQ:4 * TQ]
        imp = ps[0:nsel] + ps[nsel:nc]
        forced = (j_s == 0) | (j_s == cur) | (j_s == cur - 1)
        score = jnp.where(forced, FORCE_SCORE, jnp.where(j_s <= cur, imp, NEG))
        chosen = jnp.zeros((nsel, TQ), F32)
        for _ in range(N_SEL):
            top = jnp.max(score, axis=0, keepdims=True)
            idx = jnp.min(jnp.where(score == top, j_s, nsel), axis=0, keepdims=True)
            hit = j_s == idx
            chosen = jnp.where(hit, 1.0, chosen)
            score = jnp.where(hit, -jnp.inf, score)
        chosen = jnp.where(j_s <= cur, chosen, 0.0)
        sel_ref[k] = jnp.concatenate([chosen] * GROUP, axis=1)

    def sel_masks(j, extra=None):
        out = []
        for k in range(N_KV):
            r0 = sel_ref[k, pl.ds(2 * j, 1), :]
            r1 = sel_ref[k, pl.ds(2 * j + 1, 1), :]
            m = jnp.where(r_key < SEL_BLOCK, r0, r1) > 0.5
            out.append(m if extra is None else m & extra)
        return out

    bfars = [bfar_ref[k] for k in range(N_KV)]

    def far_body(j, c):
        return _attend_groups(c, qgs, kvs_ref, vst_ref, j, bfars, sel_masks(j))

    def near_body(j, c):
        return _attend_groups(c, qgs, kvs_ref, vst_ref, j, [bnear_ref[1, k] for k in range(N_KV)], sel_masks(j))

    n_far = jnp.maximum(i - 1, 0)
    c_slc = lax.fori_loop(0, n_far, far_body, (init, init))
    c_slc = lax.fori_loop(n_far, i, near_body, c_slc)
    c_slc = _attend_groups(c_slc, qgs, kvs_ref, vst_ref, i, [bnear_ref[0, k] for k in range(N_KV)], sel_masks(i, causal))
    c_win = _window_branch((init, init), qgs, kvw_ref, vwt_ref, i, NSA_WINDOW // TQ, bnear_ref, bfars, causal, lt_mask)

    o_groups = []
    for k in range(N_KV):
        def gate_row(branch):
            return jnp.concatenate([gt[branch * N_HEADS + GROUP * k + hh:branch * N_HEADS + GROUP * k + hh + 1, :]
                                    for hh in range(GROUP)], axis=1)

        o_groups.append(gate_row(0) * o_cmps[k] + gate_row(1) * _finish(c_slc[k]) + gate_row(2) * _finish(c_win[k]))
    _store_heads(o_ref, o_groups)


def _bias_tiles(rel_bias):
    r = jnp.arange(TQ, dtype=jnp.int32)[:, None]
    c = jnp.arange(TQ, dtype=jnp.int32)[None, :]
    near = jnp.stack([rel_bias[_t5_bucket(d * TQ + c - r)] for d in range(2)])
    near = jnp.transpose(near.reshape(2, TQ, TQ, N_KV, GROUP), (0, 3, 1, 4, 2)).reshape(2, N_KV, TQ, GROUP * TQ)
    far = jnp.repeat(rel_bias[N_BUCKETS - 1].reshape(N_KV, 1, GROUP), TQ, axis=2)
    return near.astype(F32), far.astype(F32)


def _nsa_prompt_attn(q, kvs, kvs_pt, kvw, kvw_pt, kc, vct, gates, bnear, bfar):
    B, T, _ = q.shape
    nc = T // CMP_BLOCK
    W = GROUP * TQ
    seq = pl.BlockSpec((1, T, KV_COLS), lambda b, i: (b, 0, 0))
    seq_pt = pl.BlockSpec((1, T // PAGE_SIZE, KV_COLS, PAGE_SIZE), lambda b, i: (b, 0, 0, 0))
    return pl.pallas_call(
        functools.partial(_nsa_prompt_body, T=T),
        grid=(B, T // TQ),
        in_specs=[pl.BlockSpec((1, TQ, N_HEADS * HEAD_DIM), lambda b, i: (b, i, 0)),
                  seq, seq_pt, seq, seq_pt,
                  pl.BlockSpec((1, nc, LANES), lambda b, i: (b, 0, 0)),
                  pl.BlockSpec((1, LANES, nc), lambda b, i: (b, 0, 0)),
                  pl.BlockSpec((1, TQ, LANES), lambda b, i: (b, i, 0)),
                  pl.BlockSpec((2, N_KV, TQ, W), lambda b, i: (0, 0, 0, 0)),
                  pl.BlockSpec((N_KV, 1, W), lambda b, i: (0, 0, 0))],
        out_specs=pl.BlockSpec((1, TQ, N_HEADS * HEAD_DIM), lambda b, i: (b, i, 0)),
        out_shape=jax.ShapeDtypeStruct((B, T, N_HEADS * HEAD_DIM), F32),
        scratch_shapes=[pltpu.VMEM((N_KV, nc // 2, W), F32)],
        compiler_params=_cp(("arbitrary", "arbitrary")),
        name="nsa_prompt_attn",
    )(q, kvs, kvs_pt, kvw, kvw_pt, kc, vct, gates, bnear, bfar)


def _swa_prompt_body(q_ref, kv_ref, vt_ref, sink_ref, bnear_ref, bfar_ref, o_ref):
    i = pl.program_id(1)
    qpads = _padded_queries(q_ref[0] * SCALE)
    W = GROUP * TQ
    c_lane = lax.broadcasted_iota(jnp.int32, (1, W), 1) & (TQ - 1)
    r_key = lax.broadcasted_iota(jnp.int32, (TQ, W), 0)
    causal = c_lane >= r_key
    lt_mask = c_lane < r_key
    qgs = [jnp.concatenate(qpads[GROUP * k:GROUP * (k + 1)], axis=0) for k in range(N_KV)]
    inits = tuple((sink_ref[k], jnp.ones((1, W), F32), jnp.zeros((HEAD_DIM, W), F32)) for k in range(N_KV))
    c = _window_branch(inits, qgs, kv_ref, vt_ref, i, SWA_WINDOW // TQ, bnear_ref,
                       [bfar_ref[k] for k in range(N_KV)], causal, lt_mask)
    _store_heads(o_ref, [_finish(c[k]) for k in range(N_KV)])


def _swa_prompt_attn(q, kv, kv_pt, sink_rows, bnear, bfar):
    B, T, _ = q.shape
    W = GROUP * TQ
    return pl.pallas_call(
        _swa_prompt_body,
        grid=(B, T // TQ),
        in_specs=[pl.BlockSpec((1, TQ, N_HEADS * HEAD_DIM), lambda b, i: (b, i, 0)),
                  pl.BlockSpec((1, T, KV_COLS), lambda b, i: (b, 0, 0)),
                  pl.BlockSpec((1, T // PAGE_SIZE, KV_COLS, PAGE_SIZE), lambda b, i: (b, 0, 0, 0)),
                  pl.BlockSpec((N_KV, 1, W), lambda b, i: (0, 0, 0)),
                  pl.BlockSpec((2, N_KV, TQ, W), lambda b, i: (0, 0, 0, 0)),
                  pl.BlockSpec((N_KV, 1, W), lambda b, i: (0, 0, 0))],
        out_specs=pl.BlockSpec((1, TQ, N_HEADS * HEAD_DIM), lambda b, i: (b, i, 0)),
        out_shape=jax.ShapeDtypeStruct((B, T, N_HEADS * HEAD_DIM), F32),
        compiler_params=_cp(("arbitrary", "arbitrary")),
        name="swa_prompt_attn",
    )(q, kv, kv_pt, sink_rows, bnear, bfar)


def _rglru_body(xb_ref, gb_ref, prev0_ref, h0_ref, cw_ref, cb_ref, wr_ref, br_ref, wi_ref, bi_ref, lam_ref,
                y_ref, buf_ref, hl_ref, prev_ref, h_ref, a_ref, u_ref, *, tt):
    i = pl.program_id(0)

    @pl.when(i == 0)
    def _():
        prev_ref[...] = prev0_ref[...]
        h_ref[...] = h0_ref[...]

    x = xb_ref[...]
    B, W = x.shape[1], x.shape[2]
    xp = jnp.concatenate([prev_ref[...], x], axis=0)
    xc = xp[0:tt] * cw_ref[0:1, :]
    for j in range(1, LRU_CONV):
        xc = xc + xp[j:j + tt] * cw_ref[j:j + 1, :]
    xc = xc + cb_ref[...]
    prev_ref[...] = xp[tt:tt + LRU_CONV - 1]
    xf = xc.reshape(tt * B, W)
    xh = xf.astype(BF16)
    r = jax.nn.sigmoid(_dot(xh, wr_ref[...]) + br_ref[...])
    ig = jax.nn.sigmoid(_dot(xh, wi_ref[...]) + bi_ref[...])
    log_a = -LRU_C * r * _softplus(-lam_ref[...])
    a_ref[...] = jnp.exp(log_a).reshape(tt, B, W)
    u_ref[...] = (jnp.sqrt(-_expm1(2.0 * log_a)) * (ig * xf)).reshape(tt, B, W)

    def step(t, h):
        h = a_ref[t] * h + u_ref[t]
        u_ref[t] = h
        return h

    h = lax.fori_loop(0, tt, step, h_ref[...])
    h_ref[...] = h
    y_ref[...] = u_ref[...] * _gelu(gb_ref[...])
    buf_ref[...] = prev_ref[...]
    hl_ref[...] = h


def _rglru(xb, gb, prev0, h0, prm, tt):
    T, B, W = xb.shape
    full2 = lambda shp: pl.BlockSpec(shp, lambda i: (0, 0))
    blk = pl.BlockSpec((tt, B, W), lambda i: (i, 0, 0))
    return pl.pallas_call(
        functools.partial(_rglru_body, tt=tt),
        grid=(T // tt,),
        in_specs=[blk, blk, pl.BlockSpec((LRU_CONV - 1, B, W), lambda i: (0, 0, 0)), full2((B, W)),
                  full2((LRU_CONV, W)), full2((1, W)), full2((W, W)), full2((1, W)), full2((W, W)), full2((1, W)),
                  full2((1, W))],
        out_specs=[blk, pl.BlockSpec((LRU_CONV - 1, B, W), lambda i: (0, 0, 0)), full2((B, W))],
        out_shape=[jax.ShapeDtypeStruct((T, B, W), F32), jax.ShapeDtypeStruct((LRU_CONV - 1, B, W), F32),
                   jax.ShapeDtypeStruct((B, W), F32)],
        scratch_shapes=[pltpu.VMEM((LRU_CONV - 1, B, W), F32), pltpu.VMEM((B, W), F32),
                        pltpu.VMEM((tt, B, W), F32), pltpu.VMEM((tt, B, W), F32)],
        compiler_params=_cp(("arbitrary",)),
        name="rglru",
    )(xb, gb, prev0, h0, prm['conv_w'], prm['conv_b'], prm['w_r'], prm['b_r'], prm['w_i'], prm['b_i'], prm['lam'])


def _block_diag(w):
    n, d, e = w.shape
    return jnp.einsum('nde,nm->ndme', w, jnp.eye(n, dtype=w.dtype)).reshape(n * d, n * e)


S5_CHUNKS = 4
S5_CH_IN = S5_WIDTH // S5_CHUNKS
S5_CH_ST = S5_GROUPS * S5_STATE // S5_CHUNKS
S5_NSTATE = S5_GROUPS * S5_STATE


def _s5_disc(lr, li, dt):
    mag = jnp.exp(lr * dt)
    ab_re, ab_im = mag * jnp.cos(li * dt), mag * jnp.sin(li * dt)
    den = lr * lr + li * li
    f_re = ((ab_re - 1.0) * lr + ab_im * li) / den
    f_im = (ab_im * lr - (ab_re - 1.0) * li) / den
    return ab_re, ab_im, f_re, f_im


def _s5_param_body(lr_ref, li_ref, ldt_ref, lre_ref, lie_ref, bre_ref, bim_ref, are_ref, aim_ref, bbre_ref, bbim_ref):
    dt = jnp.exp(ldt_ref[...])
    ab_re, ab_im, _, _ = _s5_disc(lr_ref[...], li_ref[...], dt)
    are_ref[...] = ab_re
    aim_ref[...] = ab_im
    _, _, f_re, f_im = _s5_disc(lre_ref[...], lie_ref[...], dt)
    br, bim = bre_ref[...], bim_ref[...]
    bbre_ref[...] = f_re * br - f_im * bim
    bbim_ref[...] = f_re * bim + f_im * br


def _s5_params(lam_re, lam_im, log_dt, b_re, b_im):
    G, P, C = b_re.shape
    rep = lambda a: jnp.repeat(a, C, axis=1)
    shapes = [jax.ShapeDtypeStruct((G, P), F32)] * 2 + [jax.ShapeDtypeStruct((G, P * C), F32)] * 2
    return pl.pallas_call(_s5_param_body, out_shape=shapes, name="s5_discretise")(
        lam_re, lam_im, log_dt.reshape(G, 1), rep(lam_re), rep(lam_im), b_re.reshape(G, P * C), b_im.reshape(G, P * C))


def _s5_body(u_ref, h0r_ref, h0i_ref, are_ref, aim_ref, bre_ref, bim_ref, cre_ref, cim_ref, d_ref, wg_ref, bg_ref,
             y_ref, sr_ref, si_ref, hr_ref, hi_ref, xr_ref, xi_ref, *, tt):
    i = pl.program_id(0)

    @pl.when(i == 0)
    def _():
        hr_ref[...] = h0r_ref[...]
        hi_ref[...] = h0i_ref[...]

    u3 = u_ref[...]
    B, W = u3.shape[1], u3.shape[2]
    uf = u3.reshape(tt * B, W)
    y_parts = []
    for ck in range(S5_CHUNKS):
        uc = uf[:, ck * S5_CH_IN:(ck + 1) * S5_CH_IN].astype(BF16)
        xr_ref[...] = _dot(uc, bre_ref[ck]).reshape(tt, B, S5_CH_ST)
        xi_ref[...] = _dot(uc, bim_ref[ck]).reshape(tt, B, S5_CH_ST)
        lanes = slice(ck * S5_CH_ST, (ck + 1) * S5_CH_ST)
        a_re = jnp.broadcast_to(are_ref[:, lanes], (B, S5_CH_ST))
        a_im = jnp.broadcast_to(aim_ref[:, lanes], (B, S5_CH_ST))

        def step(t, h):
            h_re, h_im = h
            n_re = a_re * h_re - a_im * h_im + xr_ref[t]
            n_im = a_re * h_im + a_im * h_re + xi_ref[t]
            xr_ref[t] = n_re
            xi_ref[t] = n_im
            return n_re, n_im

        h_re, h_im = lax.fori_loop(0, tt, step, (hr_ref[:, lanes], hi_ref[:, lanes]))
        hr_ref[:, lanes] = h_re
        hi_ref[:, lanes] = h_im
        hre = xr_ref[...].reshape(tt * B, S5_CH_ST).astype(BF16)
        him = xi_ref[...].reshape(tt * B, S5_CH_ST).astype(BF16)
        y_parts.append(_dot(hre, cre_ref[ck]) - _dot(him, cim_ref[ck]))
    y = jnp.concatenate(y_parts, axis=1) + d_ref[...] * uf
    g = _gelu(y)
    out = g * jax.nn.sigmoid(_dot(g.astype(BF16), wg_ref[...]) + bg_ref[...])
    y_ref[...] = out.reshape(tt, B, W)
    sr_ref[...] = hr_ref[...]
    si_ref[...] = hi_ref[...]


def _s5(u, h0_re, h0_im, prm, tt):
    T, B, W = u.shape
    full2 = lambda shp: pl.BlockSpec(shp, lambda i: (0, 0))
    full3 = lambda shp: pl.BlockSpec(shp, lambda i: (0, 0, 0))
    blk = pl.BlockSpec((tt, B, W), lambda i: (i, 0, 0))
    st = jax.ShapeDtypeStruct((B, S5_NSTATE), F32)
    return pl.pallas_call(
        functools.partial(_s5_body, tt=tt),
        grid=(T // tt,),
        in_specs=[blk, full2((B, S5_NSTATE)), full2((B, S5_NSTATE)), full2((1, S5_NSTATE)), full2((1, S5_NSTATE)),
                  full3((S5_CHUNKS, S5_CH_IN, S5_CH_ST)), full3((S5_CHUNKS, S5_CH_IN, S5_CH_ST)),
                  full3((S5_CHUNKS, S5_CH_ST, S5_CH_IN)), full3((S5_CHUNKS, S5_CH_ST, S5_CH_IN)),
                  full2((1, W)), full2((W, W)), full2((1, W))],
        out_specs=[blk, full2((B, S5_NSTATE)), full2((B, S5_NSTATE))],
        out_shape=[jax.ShapeDtypeStruct((T, B, W), F32), st, st],
        scratch_shapes=[pltpu.VMEM((B, S5_NSTATE), F32), pltpu.VMEM((B, S5_NSTATE), F32),
                        pltpu.VMEM((tt, B, S5_CH_ST), F32), pltpu.VMEM((tt, B, S5_CH_ST), F32)],
        compiler_params=_cp(("arbitrary",)),
        name="s5",
    )(u, h0_re, h0_im, prm['a_re'], prm['a_im'], prm['b_re'], prm['b_im'], prm['c_re'], prm['c_im'],
      prm['d'], prm['w_glu'], prm['b_glu'])


def _s5_prepare(lam_re, lam_im, log_dt, b_re, b_im, c_re, c_im, d, w_glu, b_glu):
    a_re, a_im, bb_re, bb_im = _s5_params(lam_re, lam_im, log_dt, b_re, b_im)
    gpc = S5_GROUPS // S5_CHUNKS
    eye = jnp.eye(gpc, dtype=F32)

    def in_mat(bb):
        bb = bb.reshape(S5_CHUNKS, gpc, S5_STATE, S5_GROUP_CH)
        return jnp.einsum('kgpc,gh->kgchp', bb, eye).reshape(S5_CHUNKS, S5_CH_IN, S5_CH_ST).astype(BF16)

    def out_mat(c):
        c = c.reshape(S5_CHUNKS, gpc, S5_GROUP_CH, S5_STATE)
        return jnp.einsum('kgcp,gh->kgphc', c, eye).reshape(S5_CHUNKS, S5_CH_ST, S5_CH_IN).astype(BF16)

    return {'a_re': a_re.reshape(1, S5_NSTATE), 'a_im': a_im.reshape(1, S5_NSTATE),
            'b_re': in_mat(bb_re), 'b_im': in_mat(bb_im), 'c_re': out_mat(c_re), 'c_im': out_mat(c_im),
            'd': d.reshape(1, S5_WIDTH), 'w_glu': w_glu.astype(BF16), 'b_glu': b_glu.reshape(1, S5_WIDTH)}


FF_CHUNK = 896


def _ffn_up_body(x_ref, sc_ref, sh_ref, w_ref, cw_ref, cb_ref, prev0_ref, a_ref, buf_ref, prev_ref, *, tt):
    i = pl.program_id(0)

    @pl.when(i == 0)
    def _():
        prev_ref[...] = prev0_ref[...]

    x = x_ref[...]
    B, D = x.shape[1], x.shape[2]
    h = (x * (1.0 + sc_ref[...]) + sh_ref[...]).reshape(tt * B, D).astype(BF16)
    for c0 in range(0, D_FF, FF_CHUNK):
        cols = slice(c0, c0 + FF_CHUNK)
        g = _dot(h, w_ref[:, cols]).reshape(tt, B, FF_CHUNK)
        v = _dot(h, w_ref[:, D_FF + c0:D_FF + c0 + FF_CHUNK]).reshape(tt, B, FF_CHUNK)
        gp = jnp.concatenate([prev_ref[:, :, cols], g], axis=0)
        y = gp[0:tt] * cw_ref[0:1, cols]
        for j in range(1, FFN_CONV):
            y = y + gp[j:j + tt] * cw_ref[j:j + 1, cols]
        y = y + cb_ref[:, cols]
        prev_ref[:, :, cols] = gp[tt:tt + FFN_CONV - 1]
        a_ref[:, :, cols] = (_gelu(y) * v).astype(BF16)
    buf_ref[...] = prev_ref[...]


def _ffn_up(x, sc, sh, w_up, conv_w, conv_b, prev0, tt):
    T, B, D = x.shape
    full2 = lambda shp: pl.BlockSpec(shp, lambda i: (0, 0))
    return pl.pallas_call(
        functools.partial(_ffn_up_body, tt=tt),
        grid=(T // tt,),
        in_specs=[pl.BlockSpec((tt, B, D), lambda i: (i, 0, 0)), full2((B, D)), full2((B, D)),
                  full2((D, 2 * D_FF)), full2((FFN_CONV, D_FF)), full2((1, D_FF)),
                  pl.BlockSpec((FFN_CONV - 1, B, D_FF), lambda i: (0, 0, 0))],
        out_specs=[pl.BlockSpec((tt, B, D_FF), lambda i: (i, 0, 0)),
                   pl.BlockSpec((FFN_CONV - 1, B, D_FF), lambda i: (0, 0, 0))],
        out_shape=[jax.ShapeDtypeStruct((T, B, D_FF), BF16), jax.ShapeDtypeStruct((FFN_CONV - 1, B, D_FF), F32)],
        scratch_shapes=[pltpu.VMEM((FFN_CONV - 1, B, D_FF), F32)],
        compiler_params=_cp(("arbitrary",)),
        name="ffn_up_conv",
    )(x, sc, sh, w_up, conv_w, conv_b.reshape(1, D_FF), prev0)


CMP_HALF = 384
N_PAST_CMP = PAST_LEN // CMP_BLOCK
N_CMP_ALL = N_PAST_CMP + 2
N_PAST_SEL = PAST_LEN // SEL_BLOCK


def _softmax_rows(s, mask):
    if mask is not None:
        s = jnp.where(mask, s, NEG)
    m = jnp.max(s, axis=1, keepdims=True)
    e = jnp.exp(s - m)
    if mask is not None:
        e = jnp.where(mask, e, 0.0)
    den = jnp.sum(e, axis=1, keepdims=True)
    return e / jnp.where(den > 0, den, 1.0)


def _own_half(o):
    row = lax.broadcasted_iota(jnp.int32, (N_HEADS, HEAD_DIM), 0)
    return jnp.where(row < GROUP, o[:, 0:HEAD_DIM], o[:, HEAD_DIM:2 * HEAD_DIM])


def _cmp_sample_body(q_ref, kcv_ref, o_ref, imp_ref):
    q = (q_ref[0] * SCALE).astype(BF16)
    kc = kcv_ref[0, :, 0:LANES].astype(BF16)
    vc = kcv_ref[0, :, LANES:2 * LANES].astype(BF16)
    s = _dot_nt(q, kc)
    lane = lax.broadcasted_iota(jnp.int32, (N_HEADS, 2 * CMP_HALF), 1)
    pos = lane % CMP_HALF
    blk = 2 * pos + lane // CMP_HALF
    vis = (blk < N_CMP_ALL) & ((blk + 1) * CMP_BLOCK - 1 <= PAST_LEN)
    p = _softmax_rows(s, vis)
    o_ref[0] = _own_half(_dot(p.astype(BF16), vc))
    row = lax.broadcasted_iota(jnp.int32, (N_HEADS, 2 * CMP_HALF), 0)
    g0 = jnp.sum(jnp.where(row < GROUP, p, 0.0), axis=0, keepdims=True)
    g1 = jnp.sum(jnp.where(row >= GROUP, p, 0.0), axis=0, keepdims=True)
    gs = jnp.concatenate([g0, g1], axis=0)
    imp_ref[0] = gs[:, 0:CMP_HALF] + gs[:, CMP_HALF:2 * CMP_HALF]


def _cmp_sample(qpad, kcv_all):
    B = qpad.shape[0]
    return pl.pallas_call(
        _cmp_sample_body,
        grid=(B,),
        in_specs=[pl.BlockSpec((1, N_HEADS, LANES), lambda b: (b, 0, 0)),
                  pl.BlockSpec((1, 2 * CMP_HALF, KV_COLS), lambda b: (b, 0, 0))],
        out_specs=[pl.BlockSpec((1, N_HEADS, HEAD_DIM), lambda b: (b, 0, 0)),
                   pl.BlockSpec((1, N_KV, CMP_HALF), lambda b: (b, 0, 0))],
        out_shape=[jax.ShapeDtypeStruct((B, N_HEADS, HEAD_DIM), F32), jax.ShapeDtypeStruct((B, N_KV, CMP_HALF), F32)],
        compiler_params=_cp(("arbitrary",)),
        name="nsa_sample_cmp",
    )(qpad, kcv_all)


def _topk_sample_body(imp_ref, idx_ref):
    imp = imp_ref[...]
    j = lax.broadcasted_iota(jnp.int32, imp.shape, 0)
    cur = N_PAST_SEL
    forced = (j == 0) | (j == cur) | (j == cur - 1)
    score = jnp.where(forced, FORCE_SCORE, jnp.where(j <= cur, imp, NEG))
    rows = []
    for _ in range(N_SEL):
        top = jnp.max(score, axis=0, keepdims=True)
        idx = jnp.min(jnp.where(score == top, j, CMP_HALF), axis=0, keepdims=True)
        rows.append(idx)
        score = jnp.where(j == idx, -jnp.inf, score)
    idx_ref[...] = jnp.concatenate(rows, axis=0)


def _topk_sample(imp_t):
    return pl.pallas_call(
        _topk_sample_body,
        out_shape=jax.ShapeDtypeStruct((N_SEL, imp_t.shape[1]), jnp.int32),
        name="nsa_sample_topk",
    )(imp_t)


def _shift_in(win_ref, new_ref, width):
    lane = lax.broadcasted_iota(jnp.int32, (KV_COLS, width), 1)
    return jnp.where(lane == width - 1, new_ref[0], pltpu.roll(win_ref[0], width - 1, axis=1))


def _by_group(a0, a1):
    row = lax.broadcasted_iota(jnp.int32, a0.shape, 0)
    return jnp.where(row < GROUP, a0, a1)


def _window_scores(q, buf):
    return _by_group(_dot(q, buf[0:HEAD_DIM].astype(BF16)), _dot(q, buf[HEAD_DIM:2 * HEAD_DIM].astype(BF16)))


def _window_values(p, buf):
    pb = p.astype(BF16)
    return _by_group(_dot_nt(pb, buf[2 * HEAD_DIM:3 * HEAD_DIM].astype(BF16)),
                     _dot_nt(pb, buf[3 * HEAD_DIM:4 * HEAD_DIM].astype(BF16)))


SEL_TAIL_HALF = (N_PAST_SEL - 1) % (PAGE_SIZE // SEL_BLOCK)


def _selwin_sample_body(idx_ref, page_ref, pool_ref, q_ref, newsel_ref, win_ref, newwin_ref, bsel_ref, bwin_ref, gate_ref,
                        ocmp_ref, o_ref, winout_ref, g_ref, bias_ref, sem):
    b = pl.program_id(0)
    nb = pl.num_programs(0)
    per_req = N_KV * N_SEL
    nl = N_SEL * PAGE_SIZE

    def page_copy(req, slot, k, n):
        page = page_ref[req * per_req + k * N_SEL + n]
        return pltpu.make_async_copy(pool_ref.at[page, :, k],
                                     g_ref.at[slot, k, :, :, pl.ds(n * PAGE_SIZE, PAGE_SIZE)],
                                     sem.at[slot, k * N_SEL + n])

    def start_all(req, slot):
        for k in range(N_KV):
            for n in range(N_SEL):
                page_copy(req, slot, k, n).start()

    @pl.when(b == 0)
    def _():
        start_all(0, 0)

    @pl.when(b + 1 < nb)
    def _():
        start_all(b + 1, (b + 1) % 2)

    slot = b % 2
    for k in range(N_KV):
        for n in range(N_SEL):
            page_copy(b, slot, k, n).wait()

    q = (q_ref[0] * SCALE).astype(BF16)
    lane = lax.broadcasted_iota(jnp.int32, (1, nl), 1)
    scores, masks, vts = [], [], []
    for k in range(N_KV):
        nvec = jnp.zeros((1, nl), jnp.int32)
        for n in range(N_SEL):
            blk = idx_ref[b * per_req + k * N_SEL + n]
            nvec = jnp.where(lane // PAGE_SIZE == n, blk, nvec)
            bias_ref[k, :, n * PAGE_SIZE:(n + 1) * PAGE_SIZE] = bsel_ref[blk]

            @pl.when(blk == N_PAST_SEL)
            def _():
                for s in range(2):
                    f0 = (s * N_KV + k) * HEAD_DIM
                    g_ref[slot, k, s, :, pl.ds(n * PAGE_SIZE + SEL_TAIL_HALF * SEL_BLOCK, 1)] = newsel_ref[0, f0:f0 + HEAD_DIM, :]

        half = jnp.minimum(nvec, N_PAST_SEL - 1) % (PAGE_SIZE // SEL_BLOCK)
        in_half = (lane % PAGE_SIZE) // SEL_BLOCK == half
        dist = PAST_LEN - (nvec * SEL_BLOCK + lane % SEL_BLOCK)
        masks.append(jnp.where(in_half & (nvec <= N_PAST_SEL) & (dist >= 0), 1.0, 0.0))
        scores.append(_dot(q, g_ref[slot, k, 0].astype(BF16)) + bias_ref[k])
        vts.append(g_ref[slot, k, 1].astype(BF16))
    row = lax.broadcasted_iota(jnp.int32, (N_HEADS, nl), 0)
    p = _softmax_rows(jnp.where(row < GROUP, scores[0], scores[1]), jnp.where(row < GROUP, masks[0], masks[1]) > 0.5)
    pb = p.astype(BF16)
    o_slc = _by_group(_dot_nt(pb, vts[0]), _dot_nt(pb, vts[1]))
    win = _shift_in(win_ref, newwin_ref, NSA_WINDOW)
    winout_ref[0] = win
    pw = _softmax_rows(_window_scores(q, win) + bwin_ref[...], None)
    o_win = _window_values(pw, win)
    g = jax.nn.sigmoid(gate_ref[0])
    o_ref[0] = g[:, 0:1] * ocmp_ref[0] + g[:, 1:2] * o_slc + g[:, 2:3] * o_win


def _selwin_sample(idx_flat, page_flat, pool5, q, new_sel, win, new_win, bias_sel, bias_win, gates, o_cmp):
    B = q.shape[0]
    nl = N_SEL * PAGE_SIZE
    spec3 = lambda shp: pl.BlockSpec(shp, lambda b, *_: (b, 0, 0))
    grid_spec = pltpu.PrefetchScalarGridSpec(
        num_scalar_prefetch=2,
        grid=(B,),
        in_specs=[pl.BlockSpec(memory_space=pl.ANY),
                  spec3((1, N_HEADS, HEAD_DIM)), spec3((1, KV_COLS, 1)), spec3((1, KV_COLS, NSA_WINDOW)),
                  spec3((1, KV_COLS, 1)),
                  pl.BlockSpec((N_PAST_SEL + 1, N_HEADS, PAGE_SIZE), lambda b, *_: (0, 0, 0)),
                  pl.BlockSpec((N_HEADS, NSA_WINDOW), lambda b, *_: (0, 0)),
                  spec3((1, N_HEADS, 3)), spec3((1, N_HEADS, HEAD_DIM))],
        out_specs=[spec3((1, N_HEADS, HEAD_DIM)), spec3((1, KV_COLS, NSA_WINDOW))],
        scratch_shapes=[pltpu.VMEM((2, N_KV, 2, HEAD_DIM, nl), F32), pltpu.VMEM((N_KV, N_HEADS, nl), F32),
                        pltpu.SemaphoreType.DMA((2, N_KV * N_SEL))],
    )
    return pl.pallas_call(
        _selwin_sample_body,
        grid_spec=grid_spec,
        out_shape=[jax.ShapeDtypeStruct((B, N_HEADS, HEAD_DIM), F32), jax.ShapeDtypeStruct((B, KV_COLS, NSA_WINDOW), F32)],
        compiler_params=_cp(("arbitrary",)),
        name="nsa_sample_sel_win",
    )(idx_flat, page_flat, pool5, q, new_sel, win, new_win, bias_sel, bias_win, gates, o_cmp)


def _swa_sample_body(q_ref, win_ref, new_ref, bias_ref, sink_ref, o_ref, winout_ref):
    q = (q_ref[0] * SCALE).astype(BF16)
    win = _shift_in(win_ref, new_ref, SWA_WINDOW)
    winout_ref[0] = win
    s = _window_scores(q, win) + bias_ref[...]
    sink = sink_ref[...]
    m = jnp.maximum(jnp.max(s, axis=1, keepdims=True), sink)
    e = jnp.exp(s - m)
    den = jnp.sum(e, axis=1, keepdims=True) + jnp.exp(sink - m)
    p = e / jnp.where(den > 0, den, 1.0)
    o_ref[0] = _window_values(p, win)


def _swa_sample(q, win, new_col, bias, sinks):
    B = q.shape[0]
    spec3 = lambda shp: pl.BlockSpec(shp, lambda b: (b, 0, 0))
    return pl.pallas_call(
        _swa_sample_body,
        grid=(B,),
        in_specs=[spec3((1, N_HEADS, HEAD_DIM)), spec3((1, KV_COLS, SWA_WINDOW)), spec3((1, KV_COLS, 1)),
                  pl.BlockSpec((N_HEADS, SWA_WINDOW), lambda b: (0, 0)),
                  pl.BlockSpec((N_HEADS, 1), lambda b: (0, 0))],
        out_specs=[spec3((1, N_HEADS, HEAD_DIM)), spec3((1, KV_COLS, SWA_WINDOW))],
        out_shape=[jax.ShapeDtypeStruct((B, N_HEADS, HEAD_DIM), F32), jax.ShapeDtypeStruct((B, KV_COLS, SWA_WINDOW), F32)],
        compiler_params=_cp(("arbitrary",)),
        name="swa_sample",
    )(q, win, new_col, bias, sinks)


def _window_to_device_view(cache, width):
    return jnp.transpose(cache, (0, 1, 3, 4, 5, 2)).reshape(cache.shape[0], KV_COLS, width)


def _window_from_device_view(buf, width):
    return jnp.transpose(buf.reshape(buf.shape[0], 1, 2, N_KV, HEAD_DIM, width), (0, 1, 5, 2, 3, 4))


def _pad_queries_sample(q):
    B = q.shape[0]
    qh = q.reshape(B, N_KV, GROUP, HEAD_DIM)
    z = jnp.zeros_like(qh[:, 0])
    return jnp.concatenate([jnp.concatenate([qh[:, 0], z], axis=-1), jnp.concatenate([z, qh[:, 1]], axis=-1)], axis=1)


def _split_mod(m, rows):
    sh, sc, gt = m[rows, 0:D_MODEL], m[rows, D_MODEL:2 * D_MODEL], m[rows, 2 * D_MODEL:]
    return sh, sc, gt


def _ffn(x_tm, mod, rows, B, T, w_up, conv_w, conv_b, w_down, prev0, ln_g, ln_b, tt, tm):
    sh, sc, gt = _split_mod(mod, rows)
    a, buf = _ffn_up(x_tm.reshape(T, B, D_MODEL), sc, sh, w_up, conv_w, conv_b, prev0, tt)
    gate = gt[:, None, :] if T > 1 else gt[None]
    nb, nt = (B, T) if T > 1 else (1, B)
    x = _mm_res_ln([a.reshape(nt, nb * D_FF)], [True], [w_down], x_tm.reshape(nt, nb * D_MODEL), True, gate,
                   ln_g, ln_b, False, nb, nt, tm)
    return x, buf


def kernel(x_prompt, x_sample, cache_kv_cmp, cache_kv_slc, cache_kv_win, state_lru_conv, state_lru_h, cache_kv_swa, state_s5_re, state_s5_im, state_ffn_conv, page_table, c_prompt, c_sample, rel_bias, w_mod, b_mod, ln_g, ln_b, w_in_ab, w_out_ab, nsa_cmp_pe, nsa_cmp_w, lru_conv_w, lru_conv_b, lru_w_r, lru_b_r, lru_w_i, lru_b_i, lru_lambda, w_in_cd, w_out_cd, swa_sinks, s5_lambda_re, s5_lambda_im, s5_log_dt, s5_b_re, s5_b_im, s5_c_re, s5_c_im, s5_d, s5_w_glu, s5_b_glu, w_ffn_up, w_ffn_conv, b_ffn_conv, w_ffn_down):
    assert N_AB == 1 and DEPTH == 2
    Bp, T, D = x_prompt.shape
    Bs = x_sample.shape[0]
    n_pool = cache_kv_cmp.shape[0]
    TM = 512 if T % 512 == 0 else T
    TT_REC = 64 if T % 64 == 0 else T
    TT_S5 = 32 if T % 32 == 0 else T
    TT_FF = 32 if T % 32 == 0 else T
    prow, srow = slice(0, Bp), slice(Bp, Bp + Bs)

    mod = _ada_mod_all(jnp.concatenate([c_prompt, c_sample], axis=0),
                       w_mod.reshape(2 * DEPTH, D, 3 * D), b_mod.reshape(2 * DEPTH, 1, 3 * D))
    bnear, bfar = _bias_tiles(rel_bias)
    xs = x_sample.reshape(1, Bs, D)

    w = w_in_ab[0].astype(BF16)
    o = np.cumsum([0, N_HEADS * HEAD_DIM, KV_COLS, KV_COLS, KV_COLS, 3 * N_HEADS, LRU_WIDTH, LRU_WIDTH])
    w_gates = jnp.pad(w[:, o[4]:o[5]], ((0, 0), (0, LANES - 3 * N_HEADS)))
    ws = [w[:, o[0]:o[1]], w[:, o[1]:o[2]], w[:, o[2]:o[3]], w[:, o[3]:o[4]], w_gates, w[:, o[5]:o[6]], w[:, o[6]:o[7]]]
    kinds_p = [['bm'], ['bm', 'pt'], ['bm', 'pt'], ['bm', 'pt'], ['bm'], ['tm'], ['tm']]
    kinds_s = [['bm'], ['bm'], ['bm'], ['bm'], ['bm'], ['tm'], ['tm']]
    w_full, pe_rows = _compress_weights(nsa_cmp_w[0], nsa_cmp_pe[0])
    w2, pe2 = _compress_pool_weights(nsa_cmp_w[0], nsa_cmp_pe[0])
    lru = {'conv_w': lru_conv_w[0], 'conv_b': lru_conv_b[0].reshape(1, -1),
           'w_r': _block_diag(lru_w_r[0]).astype(BF16), 'b_r': lru_b_r[0].reshape(1, -1),
           'w_i': _block_diag(lru_w_i[0]).astype(BF16), 'b_i': lru_b_i[0].reshape(1, -1),
           'lam': lru_lambda[0].reshape(1, -1)}
    w_out = w_out_ab[0].astype(BF16)
    w_out_parts = [w_out[0:N_HEADS * HEAD_DIM], w_out[N_HEADS * HEAD_DIM:]]

    shp, scp, gtp = _split_mod(mod[0], prow)
    q, kvc, kvc_pt, kvs, kvs_pt, kvw, kvw_pt, gates, xb, gb = _modmm(x_prompt, scp[:, None], shp[:, None],
                                                                     list(zip(ws, kinds_p)), TM)
    nc = T // CMP_BLOCK
    kcv = _compress(kvc.reshape(Bp * nc, CMP_IN), w_full, pe_rows, min(256, Bp * nc)).reshape(Bp, nc, KV_COLS)
    kcv = jnp.concatenate([kcv[:, 0::2], kcv[:, 1::2]], axis=1)
    o_att = _nsa_prompt_attn(q, kvs, kvs_pt, kvw, kvw_pt, kcv[:, :, 0:LANES], jnp.swapaxes(kcv[:, :, LANES:], 1, 2),
                             gates, bnear, bfar)
    y_lru, p_lru_buf, p_lru_h = _rglru(xb.reshape(T, Bp, LRU_WIDTH), gb.reshape(T, Bp, LRU_WIDTH),
                                       jnp.zeros((LRU_CONV - 1, Bp, LRU_WIDTH), F32), jnp.zeros((Bp, LRU_WIDTH), F32),
                                       lru, TT_REC)
    xp1 = _mm_res_ln([o_att, y_lru.reshape(T, Bp * LRU_WIDTH)], [False, True], w_out_parts, x_prompt, False,
                     gtp[:, None], ln_g[0, 0], ln_b[0, 0], True, Bp, T, TM)
    npg = T // PAGE_SIZE
    paged_out = lambda a: jnp.transpose(a.reshape(Bp, npg, 1, 2, N_KV, HEAD_DIM, PAGE_SIZE), (0, 1, 6, 2, 3, 4, 5))
    p_kv_cmp, p_kv_slc = paged_out(kvc_pt), paged_out(kvs_pt)
    wk = min(NSA_WINDOW, T)
    p_kv_win = _window_from_device_view(
        jnp.swapaxes(kvw_pt[:, npg - wk // PAGE_SIZE:], 1, 2).reshape(Bp, KV_COLS, wk), wk)

    shs, scs, gts = _split_mod(mod[0], srow)
    q_s, kvc_s, kvs_s, kvw_s, gates_s, xb_s, gb_s = _modmm(xs, scs[None], shs[None], list(zip(ws, kinds_s)), Bs)
    q_s, kvc_s, kvs_s, kvw_s, gates_s = q_s[0], kvc_s[0], kvs_s[0], kvw_s[0], gates_s[0]
    pool_cmp_t = jnp.transpose(cache_kv_cmp, (0, 2, 3, 4, 5, 1)).reshape(n_pool, KV_COLS, PAGE_SIZE)
    pool_slc_t = jnp.transpose(cache_kv_slc, (0, 2, 3, 4, 5, 1)).reshape(n_pool, 2, N_KV, HEAD_DIM, PAGE_SIZE)
    kcv_pool = _compress_pool(pool_cmp_t, w2, pe2)
    n_pages = page_table.shape[1]
    kcv_past = kcv_pool[:, page_table].reshape(2, N_KV, Bs, n_pages, BLK_PER_PAGE, HEAD_DIM)
    kcv_past = jnp.transpose(kcv_past, (2, 3, 4, 0, 1, 5)).reshape(Bs, N_PAST_CMP, KV_COLS)
    tail = jnp.pad(kvc_s[:, None, :], ((0, 0), (0, SEL_BLOCK - 1), (0, 0))).reshape(Bs * 2, CMP_IN)
    kcv_tail = _compress(tail, w_full, pe_rows, Bs * 2).reshape(Bs, 2, KV_COLS)
    pos = jnp.arange(CMP_HALF, dtype=jnp.int32)
    blk = jnp.minimum(jnp.concatenate([2 * pos, 2 * pos + 1]), N_CMP_ALL - 1)
    kcv_all = jnp.concatenate([kcv_past, kcv_tail], axis=1)[:, blk]
    o_cmp_s, imp_s = _cmp_sample(_pad_queries_sample(q_s), kcv_all)
    idx_t = _topk_sample(imp_s.reshape(Bs * N_KV, CMP_HALF).T)
    idx = idx_t.T.reshape(Bs, N_KV, N_SEL)
    per_page_s = PAGE_SIZE // SEL_BLOCK
    idx_p = jnp.minimum(idx, N_PAST_SEL - 1)
    pages = page_table[jnp.arange(Bs)[:, None, None], idx_p // per_page_s]
    blk_id = jnp.arange(N_PAST_SEL + 1, dtype=jnp.int32)
    kpos = blk_id[:, None] * SEL_BLOCK + jnp.arange(SEL_BLOCK, dtype=jnp.int32)[None, :]
    bias_half = jnp.swapaxes(rel_bias[_t5_bucket(PAST_LEN - kpos)], 1, 2)
    in_upper = (jnp.minimum(blk_id, N_PAST_SEL - 1) % per_page_s == 1)[:, None, None]
    zeros_half = jnp.zeros_like(bias_half)
    bias_sel = jnp.where(in_upper, jnp.concatenate([zeros_half, bias_half], axis=2),
                         jnp.concatenate([bias_half, zeros_half], axis=2))
    bias_win = rel_bias[_t5_bucket(NSA_WINDOW - 1 - jnp.arange(NSA_WINDOW, dtype=jnp.int32))].T
    gates3 = jnp.swapaxes(gates_s[:, 0:3 * N_HEADS].reshape(Bs, 3, N_HEADS), 1, 2)
    o_att_s, win_new = _selwin_sample(idx.reshape(-1), pages.reshape(-1).astype(jnp.int32), pool_slc_t,
                                      q_s.reshape(Bs, N_HEADS, HEAD_DIM), kvs_s[:, :, None],
                                      _window_to_device_view(cache_kv_win, NSA_WINDOW), kvw_s[:, :, None],
                                      bias_sel, bias_win, gates3, o_cmp_s)
    y_lru_s, s_lru_buf, s_lru_h = _rglru(xb_s.reshape(1, Bs, LRU_WIDTH), gb_s.reshape(1, Bs, LRU_WIDTH),
                                         jnp.swapaxes(state_lru_conv[:, 0], 0, 1), state_lru_h[:, 0], lru, 1)
    xs1 = _mm_res_ln([o_att_s.reshape(1, Bs, N_HEADS * HEAD_DIM), y_lru_s], [False, False], w_out_parts, xs, False,
                     gts[None], ln_g[0, 0], ln_b[0, 0], True, 1, Bs, Bs)

    ffn_w = [(w_ffn_up[li].astype(BF16), w_ffn_conv[li], b_ffn_conv[li], w_ffn_down[li].astype(BF16)) for li in range(DEPTH)]
    zero_ff = jnp.zeros((FFN_CONV - 1, Bp, D_FF), F32)
    xp2, p_ff0 = _ffn(xp1, mod[1], prow, Bp, T, *ffn_w[0], zero_ff, ln_g[0, 1], ln_b[0, 1], TT_FF, TM)
    xs2, s_ff0 = _ffn(xs1, mod[1], srow, Bs, 1, *ffn_w[0], jnp.swapaxes(state_ffn_conv[:, 0], 0, 1),
                      ln_g[0, 1], ln_b[0, 1], 1, Bs)

    w = w_in_cd[0].astype(BF16)
    ws = [w[:, 0:N_HEADS * HEAD_DIM], w[:, N_HEADS * HEAD_DIM:N_HEADS * HEAD_DIM + KV_COLS], w[:, N_HEADS * HEAD_DIM + KV_COLS:]]
    s5p = _s5_prepare(s5_lambda_re[0], s5_lambda_im[0], s5_log_dt[0], s5_b_re[0], s5_b_im[0], s5_c_re[0], s5_c_im[0],
                      s5_d[0], s5_w_glu[0], s5_b_glu[0])
    w_out = w_out_cd[0].astype(BF16)
    w_out_parts = [w_out[0:N_HEADS * HEAD_DIM], w_out[N_HEADS * HEAD_DIM:]]
    sink_rows = jnp.repeat(swa_sinks[0].reshape(N_KV, 1, GROUP), TQ, axis=2)

    shp, scp, gtp = _split_mod(mod[2], prow)
    q, kv, kv_pt, u = _modmm(xp2, scp[:, None], shp[:, None], list(zip(ws, [['bm'], ['bm', 'pt'], ['tm']])), TM)
    o_att = _swa_prompt_attn(q, kv, kv_pt, sink_rows, bnear, bfar)
    zst = jnp.zeros((Bp, S5_NSTATE), F32)
    y_s5, p_s5_re, p_s5_im = _s5(u.reshape(T, Bp, S5_WIDTH), zst, zst, s5p, TT_S5)
    xp3 = _mm_res_ln([o_att, y_s5.reshape(T, Bp * S5_WIDTH)], [False, True], w_out_parts, xp2, False,
                     gtp[:, None], ln_g[1, 0], ln_b[1, 0], True, Bp, T, TM)
    assert SWA_WINDOW == PAGE_SIZE and T >= SWA_WINDOW
    p_kv_swa = _window_from_device_view(kv_pt[:, npg - 1], SWA_WINDOW)

    shs, scs, gts = _split_mod(mod[2], srow)
    q_s, kv_s, u_s = _modmm(xs2, scs[None], shs[None], list(zip(ws, [['bm'], ['bm'], ['tm']])), Bs)
    bias_swa = rel_bias[_t5_bucket(SWA_WINDOW - 1 - jnp.arange(SWA_WINDOW, dtype=jnp.int32))].T
    o_att_s, swa_new = _swa_sample(q_s[0].reshape(Bs, N_HEADS, HEAD_DIM), _window_to_device_view(cache_kv_swa, SWA_WINDOW),
                                   kv_s[0][:, :, None], bias_swa, swa_sinks[0].reshape(N_HEADS, 1))
    y_s5_s, s_s5_re, s_s5_im = _s5(u_s.reshape(1, Bs, S5_WIDTH), state_s5_re[:, 0].reshape(Bs, S5_NSTATE),
                                   state_s5_im[:, 0].reshape(Bs, S5_NSTATE), s5p, 1)
    xs3 = _mm_res_ln([o_att_s.reshape(1, Bs, N_HEADS * HEAD_DIM), y_s5_s], [False, False], w_out_parts, xs2, False,
                     gts[None], ln_g[1, 0], ln_b[1, 0], True, 1, Bs, Bs)

    xp4, p_ff1 = _ffn(xp3, mod[3], prow, Bp, T, *ffn_w[1], zero_ff, ln_g[1, 1], ln_b[1, 1], TT_FF, TM)
    xs4, s_ff1 = _ffn(xs3, mod[3], srow, Bs, 1, *ffn_w[1], jnp.swapaxes(state_ffn_conv[:, 1], 0, 1),
                      ln_g[1, 1], ln_b[1, 1], 1, Bs)

    kv6 = lambda a, n: a.reshape(n, 1, 1, 2, N_KV, HEAD_DIM)
    return (xp4, xs4.reshape(Bs, 1, D),
            p_kv_cmp, p_kv_slc, p_kv_win,
            jnp.swapaxes(p_lru_buf, 0, 1)[:, None], p_lru_h[:, None], p_kv_swa,
            p_s5_re.reshape(Bp, 1, S5_GROUPS, S5_STATE), p_s5_im.reshape(Bp, 1, S5_GROUPS, S5_STATE),
            jnp.stack([jnp.swapaxes(p_ff0, 0, 1), jnp.swapaxes(p_ff1, 0, 1)], axis=1),
            kv6(kvc_s, Bs), kv6(kvs_s, Bs), _window_from_device_view(win_new, NSA_WINDOW),
            jnp.swapaxes(s_lru_buf, 0, 1)[:, None], s_lru_h[:, None],
            _window_from_device_view(swa_new, SWA_WINDOW),
            s_s5_re.reshape(Bs, 1, S5_GROUPS, S5_STATE), s_s5_im.reshape(Bs, 1, S5_GROUPS, S5_STATE),
            jnp.stack([jnp.swapaxes(s_ff0, 0, 1), jnp.swapaxes(s_ff1, 0, 1)], axis=1))
```

```python
import functools
import math

import numpy as np
import jax
import jax.numpy as jnp
from jax import lax
from jax.experimental import pallas as pl
from jax.experimental.pallas import tpu as pltpu

D_MODEL = 1024
DEPTH = 2
PAST_LEN = 16384
PAGE_SIZE = 128
HEAD_DIM = 64
N_AB = (DEPTH + 1) // 2
N_HEADS = 8
N_KV = 2
GROUP = N_HEADS // N_KV
CMP_BLOCK = 32
SEL_BLOCK = 64
N_SEL = 16
NSA_WINDOW = 512
LRU_WIDTH = D_MODEL // 2
LRU_BLOCKS = 8
LRU_CONV = 4
LRU_C = 8.0
SWA_WINDOW = 128
S5_WIDTH = D_MODEL // 2
S5_GROUP_CH = 16
S5_GROUPS = S5_WIDTH // S5_GROUP_CH
S5_STATE = 64
D_FF = (8 * D_MODEL // 3) // 128 * 128
FFN_CONV = 3
N_BUCKETS = 32
MAX_DISTANCE = 128
DN_ALPHA = (2 * DEPTH) ** 0.25
LN_EPS = 1e-5
SCALE = HEAD_DIM ** -0.5
NEG = -1e30
FORCE_SCORE = 1e4
KV_COLS = 2 * N_KV * HEAD_DIM
F32 = jnp.float32
BF16 = jnp.bfloat16

LANES = 128
TQ = 128
VMEM_LIMIT = 56 * 1024 * 1024


def _cp(sem, vmem=VMEM_LIMIT):
    return pltpu.CompilerParams(dimension_semantics=sem, vmem_limit_bytes=vmem)


def _dot(a, b):
    return jnp.dot(a, b, preferred_element_type=F32)


def _dot_nt(a, b):
    return lax.dot_general(a, b, (((1,), (1,)), ((), ())), preferred_element_type=F32)


def _gelu(x):
    cdf = 0.5 * (1.0 + jnp.tanh(math.sqrt(2.0 / math.pi) * (x + 0.044715 * (x * x * x))))
    return x * cdf


def _softplus(x):
    return jnp.maximum(x, 0.0) + jnp.log1p(jnp.exp(-jnp.abs(x)))


def _expm1(x):
    u = jnp.exp(x)
    um1 = u - 1.0
    edge = (u == 1.0) | (um1 == -1.0)
    r = um1 * x / jnp.log(jnp.where(edge, 2.0, u))
    return jnp.where(u == 1.0, x, jnp.where(um1 == -1.0, -1.0, r))


def _t5_bucket(dist):
    n = jnp.maximum(dist, 0)
    max_exact = N_BUCKETS // 2
    nf = jnp.maximum(n, 1).astype(F32)
    large = max_exact + (jnp.log(nf / max_exact) / math.log(MAX_DISTANCE / max_exact) * (N_BUCKETS - max_exact)).astype(jnp.int32)
    return jnp.where(n < max_exact, n, jnp.minimum(large, N_BUCKETS - 1))


def _rel_bias_at(rel_bias, dist):
    onehot = jax.nn.one_hot(_t5_bucket(dist), N_BUCKETS, dtype=F32)
    return jnp.matmul(onehot, rel_bias.astype(F32), precision=lax.Precision.HIGHEST)


def _bucket_saturates_from(n0, n1):
    n = np.arange(n0, n1, dtype=np.float64)
    b = 16 + np.floor(np.log(n / 16.0) / math.log(8.0) * 16.0)
    return bool(np.all(b >= N_BUCKETS - 1 + 0.5))


assert _bucket_saturates_from(TQ + 1, PAST_LEN + 2 * TQ)


def _mod_body(c_ref, w_ref, b_ref, o_ref):
    c = c_ref[...]
    h = (c * jax.nn.sigmoid(c)).astype(BF16)
    o_ref[0] = _dot(h, w_ref[0].astype(BF16)) + b_ref[0]


def _ada_mod_all(c_all, w_mod, b_mod):
    n_sub = w_mod.shape[0]
    rows = c_all.shape[0]
    tn = 1024
    return pl.pallas_call(
        _mod_body,
        grid=(n_sub, 3 * D_MODEL // tn),
        in_specs=[pl.BlockSpec((rows, D_MODEL), lambda s, n: (0, 0)),
                  pl.BlockSpec((1, D_MODEL, tn), lambda s, n: (s, 0, n)),
                  pl.BlockSpec((1, 1, tn), lambda s, n: (s, 0, n))],
        out_specs=pl.BlockSpec((1, rows, tn), lambda s, n: (s, 0, n)),
        out_shape=jax.ShapeDtypeStruct((n_sub, rows, 3 * D_MODEL), F32),
        compiler_params=_cp(("arbitrary", "arbitrary")),
        name="ada_mod",
    )(c_all, w_mod, b_mod)


def _modmm_body(x_ref, sc_ref, sh_ref, *refs, w_of_out, kinds):
    n_w = max(w_of_out) + 1
    h = (x_ref[0] * (1.0 + sc_ref[0]) + sh_ref[0]).astype(BF16)
    res = {}
    for o_ref, wi, kind in zip(refs[n_w:], w_of_out, kinds):
        if wi not in res:
            res[wi] = _dot(h, refs[wi][...])
        r = res[wi]
        if kind == 'tm':
            o_ref[...] = r
        elif kind == 'bm':
            o_ref[0] = r
        else:
            for p in range(r.shape[0] // PAGE_SIZE):
                o_ref[0, p] = r[p * PAGE_SIZE:(p + 1) * PAGE_SIZE, :].T


def _modmm(x, sc, sh, outs, tm):
    B, T, D = x.shape
    per_row = sc.shape[1] != 1
    mod_spec = pl.BlockSpec((1, tm if per_row else 1, D), (lambda b, t: (b, t, 0)) if per_row else (lambda b, t: (b, 0, 0)))
    in_specs = [pl.BlockSpec((1, tm, D), lambda b, t: (b, t, 0)), mod_spec, mod_spec]
    out_specs, out_shape, w_of_out, kinds = [], [], [], []
    for wi, (w, ks) in enumerate(outs):
        n = w.shape[1]
        in_specs.append(pl.BlockSpec((D, n), lambda b, t: (0, 0)))
        for kind in ks:
            w_of_out.append(wi)
            kinds.append(kind)
            if kind == 'tm':
                out_specs.append(pl.BlockSpec((tm, n), lambda b, t: (t, b)))
                out_shape.append(jax.ShapeDtypeStruct((T, B * n), F32))
            elif kind == 'bm':
                out_specs.append(pl.BlockSpec((1, tm, n), lambda b, t: (b, t, 0)))
                out_shape.append(jax.ShapeDtypeStruct((B, T, n), F32))
            else:
                out_specs.append(pl.BlockSpec((1, tm // PAGE_SIZE, n, PAGE_SIZE), lambda b, t: (b, t, 0, 0)))
                out_shape.append(jax.ShapeDtypeStruct((B, T // PAGE_SIZE, n, PAGE_SIZE), F32))
    return pl.pallas_call(
        functools.partial(_modmm_body, w_of_out=tuple(w_of_out), kinds=tuple(kinds)),
        grid=(B, T // tm),
        in_specs=in_specs, out_specs=out_specs, out_shape=out_shape,
        compiler_params=_cp(("arbitrary", "arbitrary")),
        name="mod_proj",
    )(x, sc, sh, *[w for w, _ in outs])


def _mm_res_ln_body(*refs, n_a, a_tm, x_tm, o_tm):
    a_refs, w_refs = refs[:n_a], refs[n_a:2 * n_a]
    x_ref, gt_ref, g_ref, b_ref, o_ref = refs[2 * n_a:]
    y = None
    for a_ref, w_ref, tmj in zip(a_refs, w_refs, a_tm):
        a = (a_ref[...] if tmj else a_ref[0]).astype(BF16)
        r = _dot(a, w_ref[...])
        y = r if y is None else y + r
    x = x_ref[...] if x_tm else x_ref[0]
    z = DN_ALPHA * x + (1.0 + gt_ref[0]) * y
    mu = jnp.mean(z, axis=-1, keepdims=True)
    var = jnp.mean(jnp.square(z - mu), axis=-1, keepdims=True)
    out = (z - mu) * lax.rsqrt(var + LN_EPS) * g_ref[...] + b_ref[...]
    if o_tm:
        o_ref[...] = out
    else:
        o_ref[0] = out


def _mm_res_ln(a_list, a_tm, ws, x, x_tm, gate, ln_g, ln_b, o_tm, B, T, tm):
    D = D_MODEL
    per_row = gate.shape[1] != 1

    def spec(k, tmj):
        if tmj:
            return pl.BlockSpec((tm, k), lambda b, t: (t, b))
        return pl.BlockSpec((1, tm, k), lambda b, t: (b, t, 0))

    in_specs = [spec(w.shape[0], tmj) for w, tmj in zip(ws, a_tm)]
    in_specs += [pl.BlockSpec(w.shape, lambda b, t: (0, 0)) for w in ws]
    in_specs += [spec(D, x_tm),
                 pl.BlockSpec((1, tm if per_row else 1, D), (lambda b, t: (b, t, 0)) if per_row else (lambda b, t: (b, 0, 0))),
                 pl.BlockSpec((1, D), lambda b, t: (0, 0)),
                 pl.BlockSpec((1, D), lambda b, t: (0, 0))]
    out_shape = jax.ShapeDtypeStruct((T, B * D) if o_tm else (B, T, D), F32)
    return pl.pallas_call(
        functools.partial(_mm_res_ln_body, n_a=len(ws), a_tm=tuple(a_tm), x_tm=x_tm, o_tm=o_tm),
        grid=(B, T // tm),
        in_specs=in_specs, out_specs=spec(D, o_tm), out_shape=out_shape,
        compiler_params=_cp(("arbitrary", "arbitrary")),
        name="proj_res_ln",
    )(*a_list, *ws, x, gate, ln_g.reshape(1, D), ln_b.reshape(1, D))


POOL_PAGES = 128
POOL_PITCH = 264
BLK_PER_PAGE = PAGE_SIZE // CMP_BLOCK


def _compress_pool_body(pool_ref, w_ref, pe_ref, o_ref, buf_ref, sem, *, P):
    i = pl.program_id(0)
    n = pl.num_programs(0)

    def page_copy(step, slot, p):
        dst = pl.multiple_of((slot * P + p) * POOL_PITCH, 8)
        return pltpu.make_async_copy(pool_ref.at[step * P + p], buf_ref.at[pl.ds(dst, KV_COLS), :], sem.at[slot])

    def start_pages(step, slot):
        def go(p, c):
            page_copy(step, slot, p).start()
            return c
        lax.fori_loop(0, P, go, 0)

    @pl.when(i == 0)
    def _():
        start_pages(0, 0)

    @pl.when(i + 1 < n)
    def _():
        start_pages(i + 1, (i + 1) % 2)

    slot = i % 2

    def wait_page(p, c):
        page_copy(i, slot, p).wait()
        return c

    lax.fori_loop(0, P, wait_page, 0)
    base = slot * (P * POOL_PITCH)

    def feature_rows(f):
        return buf_ref[pl.ds(base + f, P, stride=POOL_PITCH), :]

    DC = 8
    for s in range(2):
        out = _dot(pe_ref[s].astype(BF16), w_ref[s])[0:1]
        for d0 in range(0, HEAD_DIM, DC):
            lhs = jnp.concatenate(
                [jnp.concatenate([feature_rows((s * N_KV + k) * HEAD_DIM + d) for d in range(d0, d0 + DC)], axis=1)
                 for k in range(N_KV)], axis=0).astype(BF16)
            out = out + _dot(lhs, w_ref[s, d0 * PAGE_SIZE:(d0 + DC) * PAGE_SIZE, :])
        for k in range(N_KV):
            o_ref[s * N_KV + k] = out[k * P:(k + 1) * P]


def _compress_pool(pool_t, w2, pe2):
    n_pool = pool_t.shape[0]
    P = math.gcd(n_pool, POOL_PAGES)
    assert P % 8 == 0
    kdim = HEAD_DIM * PAGE_SIZE
    ncol = BLK_PER_PAGE * HEAD_DIM
    return pl.pallas_call(
        functools.partial(_compress_pool_body, P=P),
        grid=(n_pool // P,),
        in_specs=[pl.BlockSpec(memory_space=pl.ANY),
                  pl.BlockSpec((2, kdim, ncol), lambda i: (0, 0, 0), pipeline_mode=pl.Buffered(1)),
                  pl.BlockSpec((2, 8, kdim), lambda i: (0, 0, 0), pipeline_mode=pl.Buffered(1))],
        out_specs=pl.BlockSpec((2 * N_KV, P, ncol), lambda i: (0, i, 0)),
        out_shape=jax.ShapeDtypeStruct((2 * N_KV, n_pool, ncol), F32),
        scratch_shapes=[pltpu.VMEM((2 * P * POOL_PITCH, PAGE_SIZE), F32), pltpu.SemaphoreType.DMA((2,))],
        compiler_params=_cp(("arbitrary",)),
        name="nsa_compress_pool",
    )(pool_t, w2, pe2)


def _blocks_by_page(kcv, lead):
    kcv = kcv.reshape(2, N_KV, *lead, BLK_PER_PAGE, HEAD_DIM)
    n = len(lead)
    perm = tuple(range(2, 2 + n)) + (2 + n, 0, 1, 3 + n)
    return jnp.transpose(kcv, perm).reshape(*lead, BLK_PER_PAGE, KV_COLS)


def _compress_pool_weights(cmp_w, cmp_pe):
    eye = jnp.eye(BLK_PER_PAGE, dtype=F32)
    w2 = jnp.einsum('scde,bB->sdbcBe', cmp_w, eye).reshape(2, HEAD_DIM * PAGE_SIZE, BLK_PER_PAGE * HEAD_DIM).astype(BF16)
    pe = jnp.transpose(cmp_pe, (0, 2, 1))[:, :, None, :]
    pe = jnp.broadcast_to(pe, (2, HEAD_DIM, BLK_PER_PAGE, CMP_BLOCK)).reshape(2, 1, HEAD_DIM * PAGE_SIZE)
    return w2, jnp.broadcast_to(pe, (2, 8, HEAD_DIM * PAGE_SIZE))


def _padded_queries(q):
    lo = lax.broadcasted_iota(jnp.int32, (TQ, LANES), 1) < HEAD_DIM
    out = []
    for h in range(N_HEADS):
        x = q[:, (h // 2) * LANES:(h // 2 + 1) * LANES]
        k = h // GROUP
        if h % 2 != k:
            x = pltpu.roll(x, HEAD_DIM, axis=1)
        out.append(jnp.where(lo if k == 0 else ~lo, x, 0.0).astype(BF16))
    return out


T_BEYOND, T_DIAG, T_NEAR, T_FAR, T_FAR_EDGE, T_NEAR_EDGE = range(6)
W_LANES = GROUP * TQ


def _col_max(mv, s):
    return jnp.maximum(mv, jnp.max(s.reshape(s.shape[0] // 8, 8, s.shape[1]), axis=0))


def _col_sum(lv, p):
    return lv + jnp.sum(p.reshape(p.shape[0] // 8, 8, p.shape[1]), axis=0)


def _scores(kv_ref, qg, page, pages):
    start = pl.multiple_of(page * TQ, TQ)
    return _dot_nt(kv_ref[0, pl.ds(start, pages * TQ), 0:LANES].astype(BF16), qg)


def _values_t(kvt_ref, k, page, pages):
    vts = [kvt_ref[0, page + p, LANES + k * HEAD_DIM:LANES + (k + 1) * HEAD_DIM, :] for p in range(pages)]
    return (vts[0] if pages == 1 else jnp.concatenate(vts, axis=1)).astype(BF16)


def _softmax_pass(s_ref, kvt_ref, row0, page0, n_steps, pages, m_rows, init):
    rows = pages * TQ

    def body(t, carry):
        out = []
        for k in range(N_KV):
            lv, acc = carry[k]
            start = pl.multiple_of(row0 + t * rows, TQ)
            p = jnp.exp(s_ref[k, pl.ds(start, rows), :] - m_rows[k])
            vt = _values_t(kvt_ref, k, page0 + t * pages, pages)
            out.append((_col_sum(lv, p), acc + _dot(vt, p.astype(BF16))))
        return tuple(out)

    return lax.fori_loop(0, n_steps, body, init)


def _normalise(m_row, lv, acc, extra=None):
    l = jnp.sum(lv, axis=0, keepdims=True)
    if extra is not None:
        l = l + extra
    valid = m_row > 0.5 * NEG
    return jnp.where(valid, acc, 0.0) / jnp.where(valid & (l > 0), l, 1.0)


def _window_branch(qgs, kv_ref, kvt_ref, s_ref, tiles_ref, i, n_win, m_init):
    lo = jnp.maximum(i - n_win, 0)

    def pass1(j, mvs):
        d = i - j
        if n_win == 1:
            t = jnp.where(d == 0, T_DIAG, T_NEAR_EDGE)
        else:
            t = jnp.where(d == n_win, T_FAR_EDGE, jnp.minimum(d, 2) + T_DIAG)
        out = []
        for k in range(N_KV):
            s = _scores(kv_ref, qgs[k], j, 1) + tiles_ref[t, k]
            s_ref[k, pl.ds(pl.multiple_of((j - lo) * TQ, TQ), TQ), :] = s
            out.append(_col_max(mvs[k], s))
        return tuple(out)

    mvs = lax.fori_loop(lo, i + 1, pass1, tuple(m_init))
    m_rows = [jnp.max(mv, axis=0, keepdims=True) for mv in mvs]
    zero = (jnp.zeros((8, W_LANES), F32), jnp.zeros((HEAD_DIM, W_LANES), F32))
    res = _softmax_pass(s_ref, kvt_ref, 0, lo, i + 1 - lo, 1, m_rows, (zero,) * N_KV)
    return m_rows, res


def _store_heads(o_ref, o_groups):
    for k in range(N_KV):
        for pair in range(GROUP // 2):
            a = o_groups[k][:, (2 * pair) * TQ:(2 * pair + 1) * TQ]
            b = o_groups[k][:, (2 * pair + 1) * TQ:(2 * pair + 2) * TQ]
            col = (k * GROUP + 2 * pair) * HEAD_DIM
            o_ref[0, :, col:col + 2 * HEAD_DIM] = jnp.concatenate([a, b], axis=0).T


def _nsa_prompt_body(q_ref, kvs_ref, vst_ref, kvw_ref, vwt_ref, kc_ref, vct_ref, gates_ref, tiles_ref, o_ref,
                     sel_ref, ssel_ref, swin_ref, *, T):
    i = pl.program_id(1)
    nc = T // CMP_BLOCK
    nsel = nc // 2
    t0 = i * TQ
    qpads = _padded_queries(q_ref[0] * SCALE)
    gt = jax.nn.sigmoid(gates_ref[0]).T
    W = GROUP * TQ
    c_lane = lax.broadcasted_iota(jnp.int32, (1, W), 1) & (TQ - 1)
    tpos = t0 + c_lane
    r_c = lax.broadcasted_iota(jnp.int32, (nc, W), 0)
    blk = 2 * (r_c % nsel) + r_c // nsel
    vis = (blk + 1) * CMP_BLOCK - 1 <= tpos
    j_s = lax.broadcasted_iota(jnp.int32, (nsel, TQ), 0)
    cur = (t0 + lax.broadcasted_iota(jnp.int32, (nsel, TQ), 1)) // SEL_BLOCK
    qgs, o_cmps = [], []
    for k in range(N_KV):
        qg = jnp.concatenate(qpads[GROUP * k:GROUP * (k + 1)], axis=0)
        qgs.append(qg)
        s = jnp.where(vis, _dot_nt(kc_ref[0].astype(BF16), qg), NEG)
        m = jnp.max(s, axis=0, keepdims=True)
        e = jnp.where(vis, jnp.exp(s - m), 0.0)
        den = jnp.sum(e, axis=0, keepdims=True)
        p = e / jnp.where(den > 0, den, 1.0)
        o_cmps.append(_dot(vct_ref[0, k * HEAD_DIM:(k + 1) * HEAD_DIM, :].astype(BF16), p.astype(BF16)))
        ps = p[:, 0:TQ] + p[:, TQ:2 * TQ] + p[:, 2 * TQ:3 * TQ] + p[:, 3 * TQ:4 * TQ]
        imp = ps[0:nsel] + ps[nsel:nc]
        forced = (j_s == 0) | (j_s == cur) | (j_s == cur - 1)
        score = jnp.where(forced, FORCE_SCORE, jnp.where(j_s <= cur, imp, NEG))
        rank = jnp.zeros((nsel, TQ), F32)
        for r in range(nsel):
            row = score[r:r + 1, :]
            ahead = jnp.where(j_s > r, jnp.where(row >= score, 1.0, 0.0), jnp.where(row > score, 1.0, 0.0))
            rank = rank + ahead
        chosen = (rank < N_SEL) & (j_s <= cur)
        sel_ref[k] = jnp.concatenate([jnp.where(chosen, 0.0, NEG)] * GROUP, axis=1)

    n_steps = (i + 2) // 2

    def sel_pass1(t, mvs):
        out = []
        for k in range(N_KV):
            s = _scores(kvs_ref, qgs[k], 2 * t, 2)
            parts = []
            for h in range(4):
                tile = jnp.clip(i - (2 * t + h // 2), -1, 2) + T_DIAG
                half = slice((h % 2) * SEL_BLOCK, (h % 2 + 1) * SEL_BLOCK)
                parts.append(s[h * SEL_BLOCK:(h + 1) * SEL_BLOCK] + sel_ref[k, pl.ds(4 * t + h, 1), :] + tiles_ref[tile, k, half, :])
            s = jnp.concatenate(parts, axis=0)
            ssel_ref[k, pl.ds(pl.multiple_of(t * 2 * TQ, 2 * TQ), 2 * TQ), :] = s
            out.append(_col_max(mvs[k], s))
        return tuple(out)

    m0 = jnp.full((8, W), NEG, F32)
    mvs = lax.fori_loop(0, n_steps, sel_pass1, (m0, m0))
    m_sel = [jnp.max(mv, axis=0, keepdims=True) for mv in mvs]
    zero = (jnp.zeros((8, W), F32), jnp.zeros((HEAD_DIM, W), F32))
    r_sel = _softmax_pass(ssel_ref, vst_ref, 0, 0, n_steps, 2, m_sel, (zero, zero))
    m_win, r_win = _window_branch(qgs, kvw_ref, vwt_ref, swin_ref, tiles_ref, i, NSA_WINDOW // TQ, (m0, m0))

    o_groups = []
    for k in range(N_KV):
        def gate_row(branch):
            return jnp.concatenate([gt[branch * N_HEADS + GROUP * k + hh:branch * N_HEADS + GROUP * k + hh + 1, :]
                                    for hh in range(GROUP)], axis=1)

        o_slc = _normalise(m_sel[k], *r_sel[k])
        o_win = _normalise(m_win[k], *r_win[k])
        o_groups.append(gate_row(0) * o_cmps[k] + gate_row(1) * o_slc + gate_row(2) * o_win)
    _store_heads(o_ref, o_groups)


def _bias_tiles(rel_bias):
    r = jnp.arange(TQ, dtype=jnp.int32)[:, None]
    c = jnp.arange(TQ, dtype=jnp.int32)[None, :]
    near = jnp.stack([_rel_bias_at(rel_bias, d * TQ + c - r) for d in range(2)])
    near = jnp.transpose(near.reshape(2, TQ, TQ, N_KV, GROUP), (0, 3, 1, 4, 2)).reshape(2, N_KV, TQ, GROUP * TQ)
    far = jnp.repeat(rel_bias[N_BUCKETS - 1].reshape(N_KV, 1, GROUP), TQ, axis=2).astype(F32)
    causal = jnp.tile(jnp.where(c >= r, 0.0, NEG), (1, GROUP))
    inside = jnp.tile(jnp.where(c < r, 0.0, NEG), (1, GROUP))
    full = jnp.zeros((N_KV, TQ, GROUP * TQ), F32)
    tiles = jnp.stack([full + NEG, near[0] + causal, near[1] + full, far + full, far + inside, near[1] + inside])
    return tiles.astype(F32)


def _nsa_prompt_attn(q, kvs, kvs_pt, kvw, kvw_pt, kc, vct, gates, btiles):
    B, T, _ = q.shape
    nc = T // CMP_BLOCK
    W = GROUP * TQ
    n_tiles = btiles.shape[0]
    seq = pl.BlockSpec((1, T, KV_COLS), lambda b, i: (b, 0, 0))
    seq_pt = pl.BlockSpec((1, T // PAGE_SIZE, KV_COLS, PAGE_SIZE), lambda b, i: (b, 0, 0, 0))
    return pl.pallas_call(
        functools.partial(_nsa_prompt_body, T=T),
        grid=(B, T // TQ),
        in_specs=[pl.BlockSpec((1, TQ, N_HEADS * HEAD_DIM), lambda b, i: (b, i, 0)),
                  seq, seq_pt, seq, seq_pt,
                  pl.BlockSpec((1, nc, LANES), lambda b, i: (b, 0, 0)),
                  pl.BlockSpec((1, LANES, nc), lambda b, i: (b, 0, 0)),
                  pl.BlockSpec((1, TQ, LANES), lambda b, i: (b, i, 0)),
                  pl.BlockSpec((n_tiles, N_KV, TQ, W), lambda b, i: (0, 0, 0, 0))],
        out_specs=pl.BlockSpec((1, TQ, N_HEADS * HEAD_DIM), lambda b, i: (b, i, 0)),
        out_shape=jax.ShapeDtypeStruct((B, T, N_HEADS * HEAD_DIM), F32),
        scratch_shapes=[pltpu.VMEM((N_KV, nc // 2, W), F32), pltpu.VMEM((N_KV, T, W), F32),
                        pltpu.VMEM((N_KV, NSA_WINDOW + TQ, W), F32)],
        compiler_params=_cp(("arbitrary", "arbitrary")),
        name="nsa_prompt_attn",
    )(q, kvs, kvs_pt, kvw, kvw_pt, kc, vct, gates, btiles)


def _swa_prompt_body(q_ref, kv_ref, vt_ref, sink_ref, tiles_ref, o_ref, s_ref):
    i = pl.program_id(1)
    qpads = _padded_queries(q_ref[0] * SCALE)
    qgs = [jnp.concatenate(qpads[GROUP * k:GROUP * (k + 1)], axis=0) for k in range(N_KV)]
    m_init = [jnp.broadcast_to(sink_ref[k], (8, W_LANES)) for k in range(N_KV)]
    m_rows, res = _window_branch(qgs, kv_ref, vt_ref, s_ref, tiles_ref, i, SWA_WINDOW // TQ, m_init)
    _store_heads(o_ref, [_normalise(m_rows[k], *res[k], extra=jnp.exp(sink_ref[k] - m_rows[k])) for k in range(N_KV)])


def _swa_prompt_attn(q, kv, kv_pt, sink_rows, btiles):
    B, T, _ = q.shape
    W = GROUP * TQ
    n_tiles = btiles.shape[0]
    return pl.pallas_call(
        _swa_prompt_body,
        grid=(B, T // TQ),
        in_specs=[pl.BlockSpec((1, TQ, N_HEADS * HEAD_DIM), lambda b, i: (b, i, 0)),
                  pl.BlockSpec((1, T, KV_COLS), lambda b, i: (b, 0, 0)),
                  pl.BlockSpec((1, T // PAGE_SIZE, KV_COLS, PAGE_SIZE), lambda b, i: (b, 0, 0, 0)),
                  pl.BlockSpec((N_KV, 1, W), lambda b, i: (0, 0, 0)),
                  pl.BlockSpec((n_tiles, N_KV, TQ, W), lambda b, i: (0, 0, 0, 0))],
        out_specs=pl.BlockSpec((1, TQ, N_HEADS * HEAD_DIM), lambda b, i: (b, i, 0)),
        out_shape=jax.ShapeDtypeStruct((B, T, N_HEADS * HEAD_DIM), F32),
        scratch_shapes=[pltpu.VMEM((N_KV, SWA_WINDOW + TQ, W), F32)],
        compiler_params=_cp(("arbitrary", "arbitrary")),
        name="swa_prompt_attn",
    )(q, kv, kv_pt, sink_rows, btiles)


def _rglru_body(xb_ref, gb_ref, prev0_ref, h0_ref, cw_ref, cb_ref, wr_ref, br_ref, wi_ref, bi_ref, lam_ref,
                y_ref, buf_ref, hl_ref, prev_ref, h_ref, a_ref, u_ref, *, tt):
    i = pl.program_id(0)

    @pl.when(i == 0)
    def _():
        prev_ref[...] = prev0_ref[...]
        h_ref[...] = h0_ref[...]

    x = xb_ref[...]
    B, W = x.shape[1], x.shape[2]
    xp = jnp.concatenate([prev_ref[...], x], axis=0)
    xc = xp[0:tt] * cw_ref[0:1, :]
    for j in range(1, LRU_CONV):
        xc = xc + xp[j:j + tt] * cw_ref[j:j + 1, :]
    xc = xc + cb_ref[...]
    prev_ref[...] = xp[tt:tt + LRU_CONV - 1]
    xf = xc.reshape(tt * B, W)
    xh = xf.astype(BF16)
    r = jax.nn.sigmoid(_dot(xh, wr_ref[...]) + br_ref[...])
    ig = jax.nn.sigmoid(_dot(xh, wi_ref[...]) + bi_ref[...])
    log_a = -LRU_C * r * _softplus(-lam_ref[...])
    a_ref[...] = jnp.exp(log_a).reshape(tt, B, W)
    u_ref[...] = (jnp.sqrt(-_expm1(2.0 * log_a)) * (ig * xf)).reshape(tt, B, W)

    def step(t, h):
        h = a_ref[t] * h + u_ref[t]
        u_ref[t] = h
        return h

    h = lax.fori_loop(0, tt, step, h_ref[...])
    h_ref[...] = h
    y_ref[...] = u_ref[...] * _gelu(gb_ref[...])
    buf_ref[...] = prev_ref[...]
    hl_ref[...] = h


def _rglru(xb, gb, prev0, h0, prm, tt):
    T, B, W = xb.shape
    full2 = lambda shp: pl.BlockSpec(shp, lambda i: (0, 0))
    blk = pl.BlockSpec((tt, B, W), lambda i: (i, 0, 0))
    return pl.pallas_call(
        functools.partial(_rglru_body, tt=tt),
        grid=(T // tt,),
        in_specs=[blk, blk, pl.BlockSpec((LRU_CONV - 1, B, W), lambda i: (0, 0, 0)), full2((B, W)),
                  full2((LRU_CONV, W)), full2((1, W)), full2((W, W)), full2((1, W)), full2((W, W)), full2((1, W)),
                  full2((1, W))],
        out_specs=[blk, pl.BlockSpec((LRU_CONV - 1, B, W), lambda i: (0, 0, 0)), full2((B, W))],
        out_shape=[jax.ShapeDtypeStruct((T, B, W), F32), jax.ShapeDtypeStruct((LRU_CONV - 1, B, W), F32),
                   jax.ShapeDtypeStruct((B, W), F32)],
        scratch_shapes=[pltpu.VMEM((LRU_CONV - 1, B, W), F32), pltpu.VMEM((B, W), F32),
                        pltpu.VMEM((tt, B, W), F32), pltpu.VMEM((tt, B, W), F32)],
        compiler_params=_cp(("arbitrary",)),
        name="rglru",
    )(xb, gb, prev0, h0, prm['conv_w'], prm['conv_b'], prm['w_r'], prm['b_r'], prm['w_i'], prm['b_i'], prm['lam'])


def _block_diag(w):
    n, d, e = w.shape
    return jnp.einsum('nde,nm->ndme', w, jnp.eye(n, dtype=w.dtype)).reshape(n * d, n * e)


S5_CHUNKS = 4
S5_CH_IN = S5_WIDTH // S5_CHUNKS
S5_CH_ST = S5_GROUPS * S5_STATE // S5_CHUNKS
S5_NSTATE = S5_GROUPS * S5_STATE


def _s5_disc(lr, li, dt):
    mag = jnp.exp(lr * dt)
    ab_re, ab_im = mag * jnp.cos(li * dt), mag * jnp.sin(li * dt)
    den = lr * lr + li * li
    f_re = ((ab_re - 1.0) * lr + ab_im * li) / den
    f_im = (ab_im * lr - (ab_re - 1.0) * li) / den
    return ab_re, ab_im, f_re, f_im


def _s5_param_body(lr_ref, li_ref, ldt_ref, lre_ref, lie_ref, bre_ref, bim_ref, are_ref, aim_ref, bbre_ref, bbim_ref):
    dt = jnp.exp(ldt_ref[...])
    ab_re, ab_im, _, _ = _s5_disc(lr_ref[...], li_ref[...], dt)
    are_ref[...] = ab_re
    aim_ref[...] = ab_im
    _, _, f_re, f_im = _s5_disc(lre_ref[...], lie_ref[...], dt)
    br, bim = bre_ref[...], bim_ref[...]
    bbre_ref[...] = f_re * br - f_im * bim
    bbim_ref[...] = f_re * bim + f_im * br


def _s5_params(lam_re, lam_im, log_dt, b_re, b_im):
    G, P, C = b_re.shape
    rep = lambda a: jnp.repeat(a, C, axis=1)
    shapes = [jax.ShapeDtypeStruct((G, P), F32)] * 2 + [jax.ShapeDtypeStruct((G, P * C), F32)] * 2
    return pl.pallas_call(_s5_param_body, out_shape=shapes, name="s5_discretise")(
        lam_re, lam_im, log_dt.reshape(G, 1), rep(lam_re), rep(lam_im), b_re.reshape(G, P * C), b_im.reshape(G, P * C))


def _s5_body(u_ref, h0r_ref, h0i_ref, are_ref, aim_ref, bre_ref, bim_ref, cre_ref, cim_ref, d_ref, wg_ref, bg_ref,
             y_ref, sr_ref, si_ref, hr_ref, hi_ref, xr_ref, xi_ref, *, tt):
    i = pl.program_id(0)

    @pl.when(i == 0)
    def _():
        hr_ref[...] = h0r_ref[...]
        hi_ref[...] = h0i_ref[...]

    u3 = u_ref[...]
    B, W = u3.shape[1], u3.shape[2]
    uf = u3.reshape(tt * B, W)
    y_parts = []
    for ck in range(S5_CHUNKS):
        uc = uf[:, ck * S5_CH_IN:(ck + 1) * S5_CH_IN].astype(BF16)
        xr_ref[...] = _dot(uc, bre_ref[ck]).reshape(tt, B, S5_CH_ST)
        xi_ref[...] = _dot(uc, bim_ref[ck]).reshape(tt, B, S5_CH_ST)
        lanes = slice(ck * S5_CH_ST, (ck + 1) * S5_CH_ST)
        a_re = jnp.broadcast_to(are_ref[:, lanes], (B, S5_CH_ST))
        a_im = jnp.broadcast_to(aim_ref[:, lanes], (B, S5_CH_ST))

        def step(t, h):
            h_re, h_im = h
            n_re = a_re * h_re - a_im * h_im + xr_ref[t]
            n_im = a_re * h_im + a_im * h_re + xi_ref[t]
            xr_ref[t] = n_re
            xi_ref[t] = n_im
            return n_re, n_im

        h_re, h_im = lax.fori_loop(0, tt, step, (hr_ref[:, lanes], hi_ref[:, lanes]))
        hr_ref[:, lanes] = h_re
        hi_ref[:, lanes] = h_im
        hre = xr_ref[...].reshape(tt * B, S5_CH_ST).astype(BF16)
        him = xi_ref[...].reshape(tt * B, S5_CH_ST).astype(BF16)
        y_parts.append(_dot(hre, cre_ref[ck]) - _dot(him, cim_ref[ck]))
    y = jnp.concatenate(y_parts, axis=1) + d_ref[...] * uf
    g = _gelu(y)
    out = g * jax.nn.sigmoid(_dot(g.astype(BF16), wg_ref[...]) + bg_ref[...])
    y_ref[...] = out.reshape(tt, B, W)
    sr_ref[...] = hr_ref[...]
    si_ref[...] = hi_ref[...]


def _s5(u, h0_re, h0_im, prm, tt):
    T, B, W = u.shape
    full2 = lambda shp: pl.BlockSpec(shp, lambda i: (0, 0))
    full3 = lambda shp: pl.BlockSpec(shp, lambda i: (0, 0, 0))
    blk = pl.BlockSpec((tt, B, W), lambda i: (i, 0, 0))
    st = jax.ShapeDtypeStruct((B, S5_NSTATE), F32)
    return pl.pallas_call(
        functools.partial(_s5_body, tt=tt),
        grid=(T // tt,),
        in_specs=[blk, full2((B, S5_NSTATE)), full2((B, S5_NSTATE)), full2((1, S5_NSTATE)), full2((1, S5_NSTATE)),
                  full3((S5_CHUNKS, S5_CH_IN, S5_CH_ST)), full3((S5_CHUNKS, S5_CH_IN, S5_CH_ST)),
                  full3((S5_CHUNKS, S5_CH_ST, S5_CH_IN)), full3((S5_CHUNKS, S5_CH_ST, S5_CH_IN)),
                  full2((1, W)), full2((W, W)), full2((1, W))],
        out_specs=[blk, full2((B, S5_NSTATE)), full2((B, S5_NSTATE))],
        out_shape=[jax.ShapeDtypeStruct((T, B, W), F32), st, st],
        scratch_shapes=[pltpu.VMEM((B, S5_NSTATE), F32), pltpu.VMEM((B, S5_NSTATE), F32),
                        pltpu.VMEM((tt, B, S5_CH_ST), F32), pltpu.VMEM((tt, B, S5_CH_ST), F32)],
        compiler_params=_cp(("arbitrary",)),
        name="s5",
    )(u, h0_re, h0_im, prm['a_re'], prm['a_im'], prm['b_re'], prm['b_im'], prm['c_re'], prm['c_im'],
      prm['d'], prm['w_glu'], prm['b_glu'])


def _s5_prepare(lam_re, lam_im, log_dt, b_re, b_im, c_re, c_im, d, w_glu, b_glu):
    a_re, a_im, bb_re, bb_im = _s5_params(lam_re, lam_im, log_dt, b_re, b_im)
    gpc = S5_GROUPS // S5_CHUNKS
    eye = jnp.eye(gpc, dtype=F32)

    def in_mat(bb):
        bb = bb.reshape(S5_CHUNKS, gpc, S5_STATE, S5_GROUP_CH)
        return jnp.einsum('kgpc,gh->kgchp', bb, eye).reshape(S5_CHUNKS, S5_CH_IN, S5_CH_ST).astype(BF16)

    def out_mat(c):
        c = c.reshape(S5_CHUNKS, gpc, S5_GROUP_CH, S5_STATE)
        return jnp.einsum('kgcp,gh->kgphc', c, eye).reshape(S5_CHUNKS, S5_CH_ST, S5_CH_IN).astype(BF16)

    return {'a_re': a_re.reshape(1, S5_NSTATE), 'a_im': a_im.reshape(1, S5_NSTATE),
            'b_re': in_mat(bb_re), 'b_im': in_mat(bb_im), 'c_re': out_mat(c_re), 'c_im': out_mat(c_im),
            'd': d.reshape(1, S5_WIDTH), 'w_glu': w_glu.astype(BF16), 'b_glu': b_glu.reshape(1, S5_WIDTH)}


FF_CHUNK = 896


def _ffn_up_body(x_ref, sc_ref, sh_ref, w_ref, cw_ref, cb_ref, prev0_ref, a_ref, buf_ref, prev_ref, *, tt):
    i = pl.program_id(0)

    @pl.when(i == 0)
    def _():
        prev_ref[...] = prev0_ref[...]

    x = x_ref[...]
    B, D = x.shape[1], x.shape[2]
    h = (x * (1.0 + sc_ref[...]) + sh_ref[...]).reshape(tt * B, D).astype(BF16)
    for c0 in range(0, D_FF, FF_CHUNK):
        cols = slice(c0, c0 + FF_CHUNK)
        g = _dot(h, w_ref[:, cols]).reshape(tt, B, FF_CHUNK)
        v = _dot(h, w_ref[:, D_FF + c0:D_FF + c0 + FF_CHUNK]).reshape(tt, B, FF_CHUNK)
        gp = jnp.concatenate([prev_ref[:, :, cols], g], axis=0)
        y = gp[0:tt] * cw_ref[0:1, cols]
        for j in range(1, FFN_CONV):
            y = y + gp[j:j + tt] * cw_ref[j:j + 1, cols]
        y = y + cb_ref[:, cols]
        prev_ref[:, :, cols] = gp[tt:tt + FFN_CONV - 1]
        a_ref[:, :, cols] = (_gelu(y) * v).astype(BF16)
    buf_ref[...] = prev_ref[...]


def _ffn_up(x, sc, sh, w_up, conv_w, conv_b, prev0, tt):
    T, B, D = x.shape
    full2 = lambda shp: pl.BlockSpec(shp, lambda i: (0, 0))
    return pl.pallas_call(
        functools.partial(_ffn_up_body, tt=tt),
        grid=(T // tt,),
        in_specs=[pl.BlockSpec((tt, B, D), lambda i: (i, 0, 0)), full2((B, D)), full2((B, D)),
                  full2((D, 2 * D_FF)), full2((FFN_CONV, D_FF)), full2((1, D_FF)),
                  pl.BlockSpec((FFN_CONV - 1, B, D_FF), lambda i: (0, 0, 0))],
        out_specs=[pl.BlockSpec((tt, B, D_FF), lambda i: (i, 0, 0)),
                   pl.BlockSpec((FFN_CONV - 1, B, D_FF), lambda i: (0, 0, 0))],
        out_shape=[jax.ShapeDtypeStruct((T, B, D_FF), BF16), jax.ShapeDtypeStruct((FFN_CONV - 1, B, D_FF), F32)],
        scratch_shapes=[pltpu.VMEM((FFN_CONV - 1, B, D_FF), F32)],
        compiler_params=_cp(("arbitrary",)),
        name="ffn_up_conv",
    )(x, sc, sh, w_up, conv_w, conv_b.reshape(1, D_FF), prev0)


N_PAST_PAGES = PAST_LEN // PAGE_SIZE
N_CAND = 3 * N_PAST_PAGES
N_PAST_CMP = PAST_LEN // CMP_BLOCK
N_CMP_ALL = N_PAST_CMP + 2
N_PAST_SEL = PAST_LEN // SEL_BLOCK


def _softmax_rows(s, mask):
    if mask is not None:
        s = jnp.where(mask, s, NEG)
    m = jnp.max(s, axis=1, keepdims=True)
    e = jnp.exp(s - m)
    if mask is not None:
        e = jnp.where(mask, e, 0.0)
    den = jnp.sum(e, axis=1, keepdims=True)
    return e / jnp.where(den > 0, den, 1.0)


def _cmp_sample_body(q_ref, kcv_ref, tail_ref, o_ref, imp_ref):
    q = (q_ref[0] * SCALE).astype(BF16)
    n_pg = kcv_ref.shape[2]
    parts = []
    for blk in range(BLK_PER_PAGE):
        cols = slice(blk * HEAD_DIM, (blk + 1) * HEAD_DIM)
        parts.append(_by_group(_dot_nt(q, kcv_ref[0, 0, :, cols].astype(BF16)), _dot_nt(q, kcv_ref[1, 0, :, cols].astype(BF16))))
    parts.append(_by_group(_dot_nt(q, tail_ref[0, :, 0:HEAD_DIM].astype(BF16)),
                           _dot_nt(q, tail_ref[0, :, HEAD_DIM:2 * HEAD_DIM].astype(BF16))))
    s = jnp.concatenate(parts, axis=1)
    lane = lax.broadcasted_iota(jnp.int32, s.shape, 1)
    idx = jnp.where(lane < BLK_PER_PAGE * n_pg, BLK_PER_PAGE * (lane % n_pg) + lane // n_pg, lane)
    vis = (idx < N_CMP_ALL) & ((idx + 1) * CMP_BLOCK - 1 <= PAST_LEN)
    p = _softmax_rows(s, vis)
    pb = p.astype(BF16)
    o = jnp.zeros((N_HEADS, HEAD_DIM), F32)
    for blk in range(BLK_PER_PAGE):
        cols = slice(blk * HEAD_DIM, (blk + 1) * HEAD_DIM)
        pp = pb[:, blk * n_pg:(blk + 1) * n_pg]
        o = o + _by_group(_dot(pp, kcv_ref[2, 0, :, cols].astype(BF16)), _dot(pp, kcv_ref[3, 0, :, cols].astype(BF16)))
    pt = pb[:, BLK_PER_PAGE * n_pg:]
    o = o + _by_group(_dot(pt, tail_ref[0, :, 2 * HEAD_DIM:3 * HEAD_DIM].astype(BF16)),
                      _dot(pt, tail_ref[0, :, 3 * HEAD_DIM:4 * HEAD_DIM].astype(BF16)))
    o_ref[0] = o
    row = lax.broadcasted_iota(jnp.int32, s.shape, 0)
    g0 = jnp.sum(jnp.where(row < GROUP, p, 0.0), axis=0, keepdims=True)
    g1 = jnp.sum(jnp.where(row >= GROUP, p, 0.0), axis=0, keepdims=True)
    gs = jnp.concatenate([g0, g1], axis=0)
    gt = jnp.sum(gs[:, 4 * n_pg:], axis=1, keepdims=True)
    first = lax.broadcasted_iota(jnp.int32, (N_KV, n_pg), 1) == 0
    imp_ref[0] = jnp.concatenate([gs[:, 0:n_pg] + gs[:, n_pg:2 * n_pg], gs[:, 2 * n_pg:3 * n_pg] + gs[:, 3 * n_pg:4 * n_pg],
                                  jnp.where(first, gt, 0.0)], axis=1)


def _cmp_sample(q, kcv_pages, tail):
    B = q.shape[0]
    n_pg = kcv_pages.shape[2]
    return pl.pallas_call(
        _cmp_sample_body,
        grid=(B,),
        in_specs=[pl.BlockSpec((1, N_HEADS, HEAD_DIM), lambda b: (b, 0, 0)),
                  pl.BlockSpec((2 * N_KV, 1, n_pg, BLK_PER_PAGE * HEAD_DIM), lambda b: (0, b, 0, 0)),
                  pl.BlockSpec((1, n_pg, KV_COLS), lambda b: (b, 0, 0))],
        out_specs=[pl.BlockSpec((1, N_HEADS, HEAD_DIM), lambda b: (b, 0, 0)),
                   pl.BlockSpec((1, N_KV, 3 * n_pg), lambda b: (b, 0, 0))],
        out_shape=[jax.ShapeDtypeStruct((B, N_HEADS, HEAD_DIM), F32), jax.ShapeDtypeStruct((B, N_KV, 3 * n_pg), F32)],
        compiler_params=_cp(("arbitrary",)),
        name="nsa_sample_cmp",
    )(q, kcv_pages, tail)


def _topk_sample_body(imp_ref, idx_ref):
    imp = imp_ref[...]
    r = lax.broadcasted_iota(jnp.int32, imp.shape, 0)
    n_pg = N_PAST_PAGES
    big = 4 * N_PAST_SEL
    j = jnp.where(r < n_pg, 2 * r, jnp.where(r < 2 * n_pg, 2 * (r - n_pg) + 1, jnp.where(r == 2 * n_pg, N_PAST_SEL, big)))
    cur = N_PAST_SEL
    forced = (j == 0) | (j == cur) | (j == cur - 1)
    score = jnp.where(forced, FORCE_SCORE, jnp.where(j <= cur, imp, -jnp.inf))
    rows = []
    for _ in range(N_SEL):
        top = jnp.max(score, axis=0, keepdims=True)
        idx = jnp.min(jnp.where(score == top, j, big), axis=0, keepdims=True)
        rows.append(idx)
        score = jnp.where(j == idx, -jnp.inf, score)
    idx_ref[...] = jnp.concatenate(rows, axis=0)


def _topk_sample(imp_t):
    return pl.pallas_call(
        _topk_sample_body,
        out_shape=jax.ShapeDtypeStruct((N_SEL, imp_t.shape[1]), jnp.int32),
        name="nsa_sample_topk",
    )(imp_t)


def _shift_in(win_ref, new_ref, width):
    lane = lax.broadcasted_iota(jnp.int32, (KV_COLS, width), 1)
    return jnp.where(lane == width - 1, new_ref[0], pltpu.roll(win_ref[0], width - 1, axis=1))


def _by_group(a0, a1):
    row = lax.broadcasted_iota(jnp.int32, a0.shape, 0)
    return jnp.where(row < GROUP, a0, a1)


def _window_scores(q, buf):
    return _by_group(_dot(q, buf[0:HEAD_DIM].astype(BF16)), _dot(q, buf[HEAD_DIM:2 * HEAD_DIM].astype(BF16)))


def _window_values(p, buf):
    pb = p.astype(BF16)
    return _by_group(_dot_nt(pb, buf[2 * HEAD_DIM:3 * HEAD_DIM].astype(BF16)),
                     _dot_nt(pb, buf[3 * HEAD_DIM:4 * HEAD_DIM].astype(BF16)))


SEL_TAIL_HALF = (N_PAST_SEL - 1) % (PAGE_SIZE // SEL_BLOCK)


def _selwin_sample_body(idx_ref, page_ref, pool_ref, q_ref, newsel_ref, win_ref, newwin_ref, bsel_ref, bwin_ref, gate_ref,
                        ocmp_ref, o_ref, winout_ref, g_ref, bias_ref, sem):
    b = pl.program_id(0)
    nb = pl.num_programs(0)
    per_req = N_KV * N_SEL
    nl = N_SEL * PAGE_SIZE

    def page_copy(req, slot, k, n):
        page = page_ref[req * per_req + k * N_SEL + n]
        return pltpu.make_async_copy(pool_ref.at[page, :, k],
                                     g_ref.at[slot, k, :, :, pl.ds(n * PAGE_SIZE, PAGE_SIZE)],
                                     sem.at[slot, k * N_SEL + n])

    def start_all(req, slot):
        for k in range(N_KV):
            for n in range(N_SEL):
                page_copy(req, slot, k, n).start()

    @pl.when(b == 0)
    def _():
        start_all(0, 0)

    @pl.when(b + 1 < nb)
    def _():
        start_all(b + 1, (b + 1) % 2)

    slot = b % 2
    for k in range(N_KV):
        for n in range(N_SEL):
            page_copy(b, slot, k, n).wait()

    q = (q_ref[0] * SCALE).astype(BF16)
    lane = lax.broadcasted_iota(jnp.int32, (1, nl), 1)
    scores, masks, vts = [], [], []
    for k in range(N_KV):
        nvec = jnp.zeros((1, nl), jnp.int32)
        for n in range(N_SEL):
            blk = idx_ref[b * per_req + k * N_SEL + n]
            nvec = jnp.where(lane // PAGE_SIZE == n, blk, nvec)
            bias_ref[k, :, n * PAGE_SIZE:(n + 1) * PAGE_SIZE] = bsel_ref[blk]

            @pl.when(blk == N_PAST_SEL)
            def _():
                for s in range(2):
                    f0 = (s * N_KV + k) * HEAD_DIM
                    g_ref[slot, k, s, :, pl.ds(n * PAGE_SIZE + SEL_TAIL_HALF * SEL_BLOCK, 1)] = newsel_ref[0, f0:f0 + HEAD_DIM, :]

        half = jnp.minimum(nvec, N_PAST_SEL - 1) % (PAGE_SIZE // SEL_BLOCK)
        in_half = (lane % PAGE_SIZE) // SEL_BLOCK == half
        dist = PAST_LEN - (nvec * SEL_BLOCK + lane % SEL_BLOCK)
        masks.append(jnp.where(in_half & (nvec <= N_PAST_SEL) & (dist >= 0), 1.0, 0.0))
        scores.append(_dot(q, g_ref[slot, k, 0].astype(BF16)) + bias_ref[k])
        vts.append(g_ref[slot, k, 1].astype(BF16))
    row = lax.broadcasted_iota(jnp.int32, (N_HEADS, nl), 0)
    p = _softmax_rows(jnp.where(row < GROUP, scores[0], scores[1]), jnp.where(row < GROUP, masks[0], masks[1]) > 0.5)
    pb = p.astype(BF16)
    o_slc = _by_group(_dot_nt(pb, vts[0]), _dot_nt(pb, vts[1]))
    win = _shift_in(win_ref, newwin_ref, NSA_WINDOW)
    winout_ref[0] = win
    pw = _softmax_rows(_window_scores(q, win) + bwin_ref[...], None)
    o_win = _window_values(pw, win)
    g = jax.nn.sigmoid(gate_ref[0])
    o_ref[0] = g[:, 0:1] * ocmp_ref[0] + g[:, 1:2] * o_slc + g[:, 2:3] * o_win


def _selwin_sample(idx_flat, page_flat, pool5, q, new_sel, win, new_win, bias_sel, bias_win, gates, o_cmp):
    B = q.shape[0]
    nl = N_SEL * PAGE_SIZE
    spec3 = lambda shp: pl.BlockSpec(shp, lambda b, *_: (b, 0, 0))
    grid_spec = pltpu.PrefetchScalarGridSpec(
        num_scalar_prefetch=2,
        grid=(B,),
        in_specs=[pl.BlockSpec(memory_space=pl.ANY),
                  spec3((1, N_HEADS, HEAD_DIM)), spec3((1, KV_COLS, 1)), spec3((1, KV_COLS, NSA_WINDOW)),
                  spec3((1, KV_COLS, 1)),
                  pl.BlockSpec((N_PAST_SEL + 1, N_HEADS, PAGE_SIZE), lambda b, *_: (0, 0, 0)),
                  pl.BlockSpec((N_HEADS, NSA_WINDOW), lambda b, *_: (0, 0)),
                  spec3((1, N_HEADS, 3)), spec3((1, N_HEADS, HEAD_DIM))],
        out_specs=[spec3((1, N_HEADS, HEAD_DIM)), spec3((1, KV_COLS, NSA_WINDOW))],
        scratch_shapes=[pltpu.VMEM((2, N_KV, 2, HEAD_DIM, nl), F32), pltpu.VMEM((N_KV, N_HEADS, nl), F32),
                        pltpu.SemaphoreType.DMA((2, N_KV * N_SEL))],
    )
    return pl.pallas_call(
        _selwin_sample_body,
        grid_spec=grid_spec,
        out_shape=[jax.ShapeDtypeStruct((B, N_HEADS, HEAD_DIM), F32), jax.ShapeDtypeStruct((B, KV_COLS, NSA_WINDOW), F32)],
        compiler_params=_cp(("arbitrary",)),
        name="nsa_sample_sel_win",
    )(idx_flat, page_flat, pool5, q, new_sel, win, new_win, bias_sel, bias_win, gates, o_cmp)


def _swa_sample_body(q_ref, win_ref, new_ref, bias_ref, sink_ref, o_ref, winout_ref):
    q = (q_ref[0] * SCALE).astype(BF16)
    win = _shift_in(win_ref, new_ref, SWA_WINDOW)
    winout_ref[0] = win
    s = _window_scores(q, win) + bias_ref[...]
    sink = sink_ref[...]
    m = jnp.maximum(jnp.max(s, axis=1, keepdims=True), sink)
    e = jnp.exp(s - m)
    den = jnp.sum(e, axis=1, keepdims=True) + jnp.exp(sink - m)
    p = e / jnp.where(den > 0, den, 1.0)
    o_ref[0] = _window_values(p, win)


def _swa_sample(q, win, new_col, bias, sinks):
    B = q.shape[0]
    spec3 = lambda shp: pl.BlockSpec(shp, lambda b: (b, 0, 0))
    return pl.pallas_call(
        _swa_sample_body,
        grid=(B,),
        in_specs=[spec3((1, N_HEADS, HEAD_DIM)), spec3((1, KV_COLS, SWA_WINDOW)), spec3((1, KV_COLS, 1)),
                  pl.BlockSpec((N_HEADS, SWA_WINDOW), lambda b: (0, 0)),
                  pl.BlockSpec((N_HEADS, 1), lambda b: (0, 0))],
        out_specs=[spec3((1, N_HEADS, HEAD_DIM)), spec3((1, KV_COLS, SWA_WINDOW))],
        out_shape=[jax.ShapeDtypeStruct((B, N_HEADS, HEAD_DIM), F32), jax.ShapeDtypeStruct((B, KV_COLS, SWA_WINDOW), F32)],
        compiler_params=_cp(("arbitrary",)),
        name="swa_sample",
    )(q, win, new_col, bias, sinks)


def _window_to_device_view(cache, width):
    return jnp.transpose(cache, (0, 1, 3, 4, 5, 2)).reshape(cache.shape[0], KV_COLS, width)


def _window_from_device_view(buf, width):
    return jnp.transpose(buf.reshape(buf.shape[0], 1, 2, N_KV, HEAD_DIM, width), (0, 1, 5, 2, 3, 4))


def _split_mod(m, rows):
    sh, sc, gt = m[rows, 0:D_MODEL], m[rows, D_MODEL:2 * D_MODEL], m[rows, 2 * D_MODEL:]
    return sh, sc, gt


def _ffn(x_tm, mod, rows, B, T, w_up, conv_w, conv_b, w_down, prev0, ln_g, ln_b, tt, tm):
    sh, sc, gt = _split_mod(mod, rows)
    a, buf = _ffn_up(x_tm.reshape(T, B, D_MODEL), sc, sh, w_up, conv_w, conv_b, prev0, tt)
    gate = gt[:, None, :] if T > 1 else gt[None]
    nb, nt = (B, T) if T > 1 else (1, B)
    x = _mm_res_ln([a.reshape(nt, nb * D_FF)], [True], [w_down], x_tm.reshape(nt, nb * D_MODEL), True, gate,
                   ln_g, ln_b, False, nb, nt, tm)
    return x, buf


def kernel(x_prompt, x_sample, cache_kv_cmp, cache_kv_slc, cache_kv_win, state_lru_conv, state_lru_h, cache_kv_swa, state_s5_re, state_s5_im, state_ffn_conv, page_table, c_prompt, c_sample, rel_bias, w_mod, b_mod, ln_g, ln_b, w_in_ab, w_out_ab, nsa_cmp_pe, nsa_cmp_w, lru_conv_w, lru_conv_b, lru_w_r, lru_b_r, lru_w_i, lru_b_i, lru_lambda, w_in_cd, w_out_cd, swa_sinks, s5_lambda_re, s5_lambda_im, s5_log_dt, s5_b_re, s5_b_im, s5_c_re, s5_c_im, s5_d, s5_w_glu, s5_b_glu, w_ffn_up, w_ffn_conv, b_ffn_conv, w_ffn_down):
    assert N_AB == 1 and DEPTH == 2
    Bp, T, D = x_prompt.shape
    Bs = x_sample.shape[0]
    n_pool = cache_kv_cmp.shape[0]
    TM = 512 if T % 512 == 0 else T
    TT_REC = 64 if T % 64 == 0 else T
    TT_S5 = 32 if T % 32 == 0 else T
    TT_FF = 32 if T % 32 == 0 else T
    prow, srow = slice(0, Bp), slice(Bp, Bp + Bs)

    mod = _ada_mod_all(jnp.concatenate([c_prompt, c_sample], axis=0),
                       w_mod.reshape(2 * DEPTH, D, 3 * D), b_mod.reshape(2 * DEPTH, 1, 3 * D))
    btiles = _bias_tiles(rel_bias)
    xs = x_sample.reshape(1, Bs, D)

    w = w_in_ab[0].astype(BF16)
    o = np.cumsum([0, N_HEADS * HEAD_DIM, KV_COLS, KV_COLS, KV_COLS, 3 * N_HEADS, LRU_WIDTH, LRU_WIDTH])
    w_gates = jnp.pad(w[:, o[4]:o[5]], ((0, 0), (0, LANES - 3 * N_HEADS)))
    ws = [w[:, o[0]:o[1]], w[:, o[1]:o[2]], w[:, o[2]:o[3]], w[:, o[3]:o[4]], w_gates, w[:, o[5]:o[6]], w[:, o[6]:o[7]]]
    kinds_p = [['bm'], ['pt'], ['bm', 'pt'], ['bm', 'pt'], ['bm'], ['tm'], ['tm']]
    kinds_s = [['bm'], ['bm'], ['bm'], ['bm'], ['bm'], ['tm'], ['tm']]
    w2, pe2 = _compress_pool_weights(nsa_cmp_w[0], nsa_cmp_pe[0])
    lru = {'conv_w': lru_conv_w[0], 'conv_b': lru_conv_b[0].reshape(1, -1),
           'w_r': _block_diag(lru_w_r[0]).astype(BF16), 'b_r': lru_b_r[0].reshape(1, -1),
           'w_i': _block_diag(lru_w_i[0]).astype(BF16), 'b_i': lru_b_i[0].reshape(1, -1),
           'lam': lru_lambda[0].reshape(1, -1)}
    w_out = w_out_ab[0].astype(BF16)
    w_out_parts = [w_out[0:N_HEADS * HEAD_DIM], w_out[N_HEADS * HEAD_DIM:]]

    shp, scp, gtp = _split_mod(mod[0], prow)
    q, kvc_pt, kvs, kvs_pt, kvw, kvw_pt, gates, xb, gb = _modmm(x_prompt, scp[:, None], shp[:, None],
                                                                list(zip(ws, kinds_p)), TM)
    nc = T // CMP_BLOCK
    npg = T // PAGE_SIZE
    kcv = _compress_pool(kvc_pt.reshape(Bp * npg, KV_COLS, PAGE_SIZE), w2, pe2)
    kcv = _blocks_by_page(kcv, (Bp, npg)).reshape(Bp, nc, KV_COLS)
    kcv = jnp.concatenate([kcv[:, 0::2], kcv[:, 1::2]], axis=1)
    o_att = _nsa_prompt_attn(q, kvs, kvs_pt, kvw, kvw_pt, kcv[:, :, 0:LANES], jnp.swapaxes(kcv[:, :, LANES:], 1, 2),
                             gates, btiles)
    y_lru, p_lru_buf, p_lru_h = _rglru(xb.reshape(T, Bp, LRU_WIDTH), gb.reshape(T, Bp, LRU_WIDTH),
                                       jnp.zeros((LRU_CONV - 1, Bp, LRU_WIDTH), F32), jnp.zeros((Bp, LRU_WIDTH), F32),
                                       lru, TT_REC)
    xp1 = _mm_res_ln([o_att, y_lru.reshape(T, Bp * LRU_WIDTH)], [False, True], w_out_parts, x_prompt, False,
                     gtp[:, None], ln_g[0, 0], ln_b[0, 0], True, Bp, T, TM)
    paged_out = lambda a: jnp.transpose(a.reshape(Bp, npg, 1, 2, N_KV, HEAD_DIM, PAGE_SIZE), (0, 1, 6, 2, 3, 4, 5))
    p_kv_cmp, p_kv_slc = paged_out(kvc_pt), paged_out(kvs_pt)
    wk = min(NSA_WINDOW, T)
    p_kv_win = _window_from_device_view(
        jnp.swapaxes(kvw_pt[:, npg - wk // PAGE_SIZE:], 1, 2).reshape(Bp, KV_COLS, wk), wk)

    shs, scs, gts = _split_mod(mod[0], srow)
    q_s, kvc_s, kvs_s, kvw_s, gates_s, xb_s, gb_s = _modmm(xs, scs[None], shs[None], list(zip(ws, kinds_s)), Bs)
    q_s, kvc_s, kvs_s, kvw_s, gates_s = q_s[0], kvc_s[0], kvs_s[0], kvw_s[0], gates_s[0]
    pool_cmp_t = jnp.transpose(cache_kv_cmp, (0, 2, 3, 4, 5, 1)).reshape(n_pool, KV_COLS, PAGE_SIZE)
    pool_slc_t = jnp.transpose(cache_kv_slc, (0, 2, 3, 4, 5, 1)).reshape(n_pool, 2, N_KV, HEAD_DIM, PAGE_SIZE)
    kcv_pool = _compress_pool(pool_cmp_t, w2, pe2)
    tail_pages = jnp.pad(kvc_s[:, :, None], ((0, 0), (0, 0), (0, PAGE_SIZE - 1)))
    kcv_tail = _blocks_by_page(_compress_pool(tail_pages, w2, pe2), (Bs,))[:, 0:SEL_BLOCK // CMP_BLOCK]
    kcv_tail = jnp.pad(kcv_tail, ((0, 0), (0, N_PAST_PAGES - SEL_BLOCK // CMP_BLOCK), (0, 0)))
    q_s3 = q_s.reshape(Bs, N_HEADS, HEAD_DIM)
    o_cmp_s, imp_s = _cmp_sample(q_s3, kcv_pool[:, page_table], kcv_tail)
    idx_t = _topk_sample(imp_s.reshape(Bs * N_KV, N_CAND).T)
    idx = idx_t.T.reshape(Bs, N_KV, N_SEL)
    per_page_s = PAGE_SIZE // SEL_BLOCK
    idx_p = jnp.minimum(idx, N_PAST_SEL - 1)
    pages = page_table[jnp.arange(Bs)[:, None, None], idx_p // per_page_s]
    blk_id = jnp.arange(N_PAST_SEL + 1, dtype=jnp.int32)
    kpos = blk_id[:, None] * SEL_BLOCK + jnp.arange(SEL_BLOCK, dtype=jnp.int32)[None, :]
    bias_half = jnp.swapaxes(_rel_bias_at(rel_bias, PAST_LEN - kpos), 1, 2)
    in_upper = (jnp.minimum(blk_id, N_PAST_SEL - 1) % per_page_s == 1)[:, None, None]
    zeros_half = jnp.zeros_like(bias_half)
    bias_sel = jnp.where(in_upper, jnp.concatenate([zeros_half, bias_half], axis=2),
                         jnp.concatenate([bias_half, zeros_half], axis=2))
    bias_win = _rel_bias_at(rel_bias, NSA_WINDOW - 1 - jnp.arange(NSA_WINDOW, dtype=jnp.int32)).T
    gates3 = jnp.swapaxes(gates_s[:, 0:3 * N_HEADS].reshape(Bs, 3, N_HEADS), 1, 2)
    o_att_s, win_new = _selwin_sample(idx.reshape(-1), pages.reshape(-1).astype(jnp.int32), pool_slc_t,
                                      q_s3, kvs_s[:, :, None],
                                      _window_to_device_view(cache_kv_win, NSA_WINDOW), kvw_s[:, :, None],
                                      bias_sel, bias_win, gates3, o_cmp_s)
    y_lru_s, s_lru_buf, s_lru_h = _rglru(xb_s.reshape(1, Bs, LRU_WIDTH), gb_s.reshape(1, Bs, LRU_WIDTH),
                                         jnp.swapaxes(state_lru_conv[:, 0], 0, 1), state_lru_h[:, 0], lru, 1)
    xs1 = _mm_res_ln([o_att_s.reshape(1, Bs, N_HEADS * HEAD_DIM), y_lru_s], [False, False], w_out_parts, xs, False,
                     gts[None], ln_g[0, 0], ln_b[0, 0], True, 1, Bs, Bs)

    ffn_w = [(w_ffn_up[li].astype(BF16), w_ffn_conv[li], b_ffn_conv[li], w_ffn_down[li].astype(BF16)) for li in range(DEPTH)]
    zero_ff = jnp.zeros((FFN_CONV - 1, Bp, D_FF), F32)
    xp2, p_ff0 = _ffn(xp1, mod[1], prow, Bp, T, *ffn_w[0], zero_ff, ln_g[0, 1], ln_b[0, 1], TT_FF, TM)
    xs2, s_ff0 = _ffn(xs1, mod[1], srow, Bs, 1, *ffn_w[0], jnp.swapaxes(state_ffn_conv[:, 0], 0, 1),
                      ln_g[0, 1], ln_b[0, 1], 1, Bs)

    w = w_in_cd[0].astype(BF16)
    ws = [w[:, 0:N_HEADS * HEAD_DIM], w[:, N_HEADS * HEAD_DIM:N_HEADS * HEAD_DIM + KV_COLS], w[:, N_HEADS * HEAD_DIM + KV_COLS:]]
    s5p = _s5_prepare(s5_lambda_re[0], s5_lambda_im[0], s5_log_dt[0], s5_b_re[0], s5_b_im[0], s5_c_re[0], s5_c_im[0],
                      s5_d[0], s5_w_glu[0], s5_b_glu[0])
    w_out = w_out_cd[0].astype(BF16)
    w_out_parts = [w_out[0:N_HEADS * HEAD_DIM], w_out[N_HEADS * HEAD_DIM:]]
    sink_rows = jnp.repeat(swa_sinks[0].reshape(N_KV, 1, GROUP), TQ, axis=2)

    shp, scp, gtp = _split_mod(mod[2], prow)
    q, kv, kv_pt, u = _modmm(xp2, scp[:, None], shp[:, None], list(zip(ws, [['bm'], ['bm', 'pt'], ['tm']])), TM)
    o_att = _swa_prompt_attn(q, kv, kv_pt, sink_rows, btiles)
    zst = jnp.zeros((Bp, S5_NSTATE), F32)
    y_s5, p_s5_re, p_s5_im = _s5(u.reshape(T, Bp, S5_WIDTH), zst, zst, s5p, TT_S5)
    xp3 = _mm_res_ln([o_att, y_s5.reshape(T, Bp * S5_WIDTH)], [False, True], w_out_parts, xp2, False,
                     gtp[:, None], ln_g[1, 0], ln_b[1, 0], True, Bp, T, TM)
    assert SWA_WINDOW == PAGE_SIZE and T >= SWA_WINDOW
    p_kv_swa = _window_from_device_view(kv_pt[:, npg - 1], SWA_WINDOW)

    shs, scs, gts = _split_mod(mod[2], srow)
    q_s, kv_s, u_s = _modmm(xs2, scs[None], shs[None], list(zip(ws, [['bm'], ['bm'], ['tm']])), Bs)
    bias_swa = _rel_bias_at(rel_bias, SWA_WINDOW - 1 - jnp.arange(SWA_WINDOW, dtype=jnp.int32)).T
    o_att_s, swa_new = _swa_sample(q_s[0].reshape(Bs, N_HEADS, HEAD_DIM), _window_to_device_view(cache_kv_swa, SWA_WINDOW),
                                   kv_s[0][:, :, None], bias_swa, swa_sinks[0].reshape(N_HEADS, 1))
    y_s5_s, s_s5_re, s_s5_im = _s5(u_s.reshape(1, Bs, S5_WIDTH), state_s5_re[:, 0].reshape(Bs, S5_NSTATE),
                                   state_s5_im[:, 0].reshape(Bs, S5_NSTATE), s5p, 1)
    xs3 = _mm_res_ln([o_att_s.reshape(1, Bs, N_HEADS * HEAD_DIM), y_s5_s], [False, False], w_out_parts, xs2, False,
                     gts[None], ln_g[1, 0], ln_b[1, 0], True, 1, Bs, Bs)

    xp4, p_ff1 = _ffn(xp3, mod[3], prow, Bp, T, *ffn_w[1], zero_ff, ln_g[1, 1], ln_b[1, 1], TT_FF, TM)
    xs4, s_ff1 = _ffn(xs3, mod[3], srow, Bs, 1, *ffn_w[1], jnp.swapaxes(state_ffn_conv[:, 1], 0, 1),
                      ln_g[1, 1], ln_b[1, 1], 1, Bs)

    kv6 = lambda a, n: a.reshape(n, 1, 1, 2, N_KV, HEAD_DIM)
    return (xp4, xs4.reshape(Bs, 1, D),
            p_kv_cmp, p_kv_slc, p_kv_win,
            jnp.swapaxes(p_lru_buf, 0, 1)[:, None], p_lru_h[:, None], p_kv_swa,
            p_s5_re.reshape(Bp, 1, S5_GROUPS, S5_STATE), p_s5_im.reshape(Bp, 1, S5_GROUPS, S5_STATE),
            jnp.stack([jnp.swapaxes(p_ff0, 0, 1), jnp.swapaxes(p_ff1, 0, 1)], axis=1),
            kv6(kvc_s, Bs), kv6(kvs_s, Bs), _window_from_device_view(win_new, NSA_WINDOW),
            jnp.swapaxes(s_lru_buf, 0, 1)[:, None], s_lru_h[:, None],
            _window_from_device_view(swa_new, SWA_WINDOW),
            s_s5_re.reshape(Bs, 1, S5_GROUPS, S5_STATE), s_s5_im.reshape(Bs, 1, S5_GROUPS, S5_STATE),
            jnp.stack([jnp.swapaxes(s_ff0, 0, 1), jnp.swapaxes(s_ff1, 0, 1)], axis=1))
```

```python
import functools
import math

import numpy as np
import jax
import jax.numpy as jnp
from jax import lax
from jax.experimental import pallas as pl
from jax.experimental.pallas import tpu as pltpu

D_MODEL = 1024
DEPTH = 2
PAST_LEN = 16384
PAGE_SIZE = 128
HEAD_DIM = 64
N_AB = (DEPTH + 1) // 2
N_HEADS = 8
N_KV = 2
GROUP = N_HEADS // N_KV
CMP_BLOCK = 32
SEL_BLOCK = 64
N_SEL = 16
NSA_WINDOW = 512
LRU_WIDTH = D_MODEL // 2
LRU_BLOCKS = 8
LRU_CONV = 4
LRU_C = 8.0
SWA_WINDOW = 128
S5_WIDTH = D_MODEL // 2
S5_GROUP_CH = 16
S5_GROUPS = S5_WIDTH // S5_GROUP_CH
S5_STATE = 64
D_FF = (8 * D_MODEL // 3) // 128 * 128
FFN_CONV = 3
N_BUCKETS = 32
MAX_DISTANCE = 128
DN_ALPHA = (2 * DEPTH) ** 0.25
LN_EPS = 1e-5
SCALE = HEAD_DIM ** -0.5
NEG = -1e30
FORCE_SCORE = 1e4
KV_COLS = 2 * N_KV * HEAD_DIM
F32 = jnp.float32
BF16 = jnp.bfloat16

LANES = 128
TQ = 128
VMEM_LIMIT = 56 * 1024 * 1024


def _cp(sem, vmem=VMEM_LIMIT):
    return pltpu.CompilerParams(dimension_semantics=sem, vmem_limit_bytes=vmem)


def _dot(a, b):
    return jnp.dot(a, b, preferred_element_type=F32)


def _dot_nt(a, b):
    return lax.dot_general(a, b, (((1,), (1,)), ((), ())), preferred_element_type=F32)


def _gelu(x):
    cdf = 0.5 * (1.0 + jnp.tanh(math.sqrt(2.0 / math.pi) * (x + 0.044715 * (x * x * x))))
    return x * cdf


def _softplus(x):
    return jnp.maximum(x, 0.0) + jnp.log1p(jnp.exp(-jnp.abs(x)))


def _expm1(x):
    u = jnp.exp(x)
    um1 = u - 1.0
    edge = (u == 1.0) | (um1 == -1.0)
    r = um1 * x / jnp.log(jnp.where(edge, 2.0, u))
    return jnp.where(u == 1.0, x, jnp.where(um1 == -1.0, -1.0, r))


def _t5_bucket(dist):
    n = jnp.maximum(dist, 0)
    max_exact = N_BUCKETS // 2
    nf = jnp.maximum(n, 1).astype(F32)
    steps = jnp.log(nf / max_exact) / math.log(MAX_DISTANCE / max_exact) * (N_BUCKETS - max_exact)
    large = max_exact + jnp.floor(steps).astype(jnp.int32)
    return jnp.where(n < max_exact, n, jnp.minimum(large, N_BUCKETS - 1))


def _rel_bias_at(rel_bias, dist):
    onehot = jax.nn.one_hot(_t5_bucket(dist), N_BUCKETS, dtype=F32)
    return jnp.matmul(onehot, rel_bias.astype(F32), precision=lax.Precision.HIGHEST)


def _bucket_saturates_from(n0, n1):
    n = np.arange(n0, n1, dtype=np.float64)
    b = 16 + np.floor(np.log(n / 16.0) / math.log(8.0) * 16.0)
    return bool(np.all(b >= N_BUCKETS - 1 + 0.5))


assert _bucket_saturates_from(TQ + 1, PAST_LEN + 2 * TQ)


def _mod_body(c_ref, w_ref, b_ref, o_ref):
    c = c_ref[...]
    h = (c * jax.nn.sigmoid(c)).astype(BF16)
    o_ref[0] = _dot(h, w_ref[0].astype(BF16)) + b_ref[0]


def _ada_mod_all(c_all, w_mod, b_mod):
    n_sub = w_mod.shape[0]
    rows = c_all.shape[0]
    tn = 1024
    return pl.pallas_call(
        _mod_body,
        grid=(n_sub, 3 * D_MODEL // tn),
        in_specs=[pl.BlockSpec((rows, D_MODEL), lambda s, n: (0, 0)),
                  pl.BlockSpec((1, D_MODEL, tn), lambda s, n: (s, 0, n)),
                  pl.BlockSpec((1, 1, tn), lambda s, n: (s, 0, n))],
        out_specs=pl.BlockSpec((1, rows, tn), lambda s, n: (s, 0, n)),
        out_shape=jax.ShapeDtypeStruct((n_sub, rows, 3 * D_MODEL), F32),
        compiler_params=_cp(("arbitrary", "arbitrary")),
        name="ada_mod",
    )(c_all, w_mod, b_mod)


def _modmm_body(x_ref, sc_ref, sh_ref, *refs, w_of_out, kinds):
    n_w = max(w_of_out) + 1
    h = (x_ref[0] * (1.0 + sc_ref[0]) + sh_ref[0]).astype(BF16)
    res = {}
    for o_ref, wi, kind in zip(refs[n_w:], w_of_out, kinds):
        if wi not in res:
            res[wi] = _dot(h, refs[wi][...])
        r = res[wi]
        if kind == 'tm':
            o_ref[...] = r
        elif kind == 'bm':
            o_ref[0] = r
        else:
            for p in range(r.shape[0] // PAGE_SIZE):
                o_ref[0, p] = r[p * PAGE_SIZE:(p + 1) * PAGE_SIZE, :].T


def _modmm(x, sc, sh, outs, tm):
    B, T, D = x.shape
    per_row = sc.shape[1] != 1
    mod_spec = pl.BlockSpec((1, tm if per_row else 1, D), (lambda b, t: (b, t, 0)) if per_row else (lambda b, t: (b, 0, 0)))
    in_specs = [pl.BlockSpec((1, tm, D), lambda b, t: (b, t, 0)), mod_spec, mod_spec]
    out_specs, out_shape, w_of_out, kinds = [], [], [], []
    for wi, (w, ks) in enumerate(outs):
        n = w.shape[1]
        in_specs.append(pl.BlockSpec((D, n), lambda b, t: (0, 0)))
        for kind in ks:
            w_of_out.append(wi)
            kinds.append(kind)
            if kind == 'tm':
                out_specs.append(pl.BlockSpec((tm, n), lambda b, t: (t, b)))
                out_shape.append(jax.ShapeDtypeStruct((T, B * n), F32))
            elif kind == 'bm':
                out_specs.append(pl.BlockSpec((1, tm, n), lambda b, t: (b, t, 0)))
                out_shape.append(jax.ShapeDtypeStruct((B, T, n), F32))
            else:
                out_specs.append(pl.BlockSpec((1, tm // PAGE_SIZE, n, PAGE_SIZE), lambda b, t: (b, t, 0, 0)))
                out_shape.append(jax.ShapeDtypeStruct((B, T // PAGE_SIZE, n, PAGE_SIZE), F32))
    return pl.pallas_call(
        functools.partial(_modmm_body, w_of_out=tuple(w_of_out), kinds=tuple(kinds)),
        grid=(B, T // tm),
        in_specs=in_specs, out_specs=out_specs, out_shape=out_shape,
        compiler_params=_cp(("arbitrary", "arbitrary")),
        name="mod_proj",
    )(x, sc, sh, *[w for w, _ in outs])


def _mm_res_ln_body(*refs, n_a, a_tm, x_tm, o_tm):
    a_refs, w_refs = refs[:n_a], refs[n_a:2 * n_a]
    x_ref, gt_ref, g_ref, b_ref, o_ref = refs[2 * n_a:]
    y = None
    for a_ref, w_ref, tmj in zip(a_refs, w_refs, a_tm):
        a = (a_ref[...] if tmj else a_ref[0]).astype(BF16)
        r = _dot(a, w_ref[...])
        y = r if y is None else y + r
    x = x_ref[...] if x_tm else x_ref[0]
    z = DN_ALPHA * x + (1.0 + gt_ref[0]) * y
    mu = jnp.mean(z, axis=-1, keepdims=True)
    var = jnp.mean(jnp.square(z - mu), axis=-1, keepdims=True)
    out = (z - mu) * lax.rsqrt(var + LN_EPS) * g_ref[...] + b_ref[...]
    if o_tm:
        o_ref[...] = out
    else:
        o_ref[0] = out


def _mm_res_ln(a_list, a_tm, ws, x, x_tm, gate, ln_g, ln_b, o_tm, B, T, tm):
    D = D_MODEL
    per_row = gate.shape[1] != 1

    def spec(k, tmj):
        if tmj:
            return pl.BlockSpec((tm, k), lambda b, t: (t, b))
        return pl.BlockSpec((1, tm, k), lambda b, t: (b, t, 0))

    in_specs = [spec(w.shape[0], tmj) for w, tmj in zip(ws, a_tm)]
    in_specs += [pl.BlockSpec(w.shape, lambda b, t: (0, 0)) for w in ws]
    in_specs += [spec(D, x_tm),
                 pl.BlockSpec((1, tm if per_row else 1, D), (lambda b, t: (b, t, 0)) if per_row else (lambda b, t: (b, 0, 0))),
                 pl.BlockSpec((1, D), lambda b, t: (0, 0)),
                 pl.BlockSpec((1, D), lambda b, t: (0, 0))]
    out_shape = jax.ShapeDtypeStruct((T, B * D) if o_tm else (B, T, D), F32)
    return pl.pallas_call(
        functools.partial(_mm_res_ln_body, n_a=len(ws), a_tm=tuple(a_tm), x_tm=x_tm, o_tm=o_tm),
        grid=(B, T // tm),
        in_specs=in_specs, out_specs=spec(D, o_tm), out_shape=out_shape,
        compiler_params=_cp(("arbitrary", "arbitrary")),
        name="proj_res_ln",
    )(*a_list, *ws, x, gate, ln_g.reshape(1, D), ln_b.reshape(1, D))


POOL_PAGES = 128
POOL_PITCH = 264
BLK_PER_PAGE = PAGE_SIZE // CMP_BLOCK


def _compress_pool_body(ids_ref, pool_ref, w_ref, pe_ref, o_ref, buf_ref, sem, *, P):
    i = pl.program_id(0)
    n = pl.num_programs(0)

    def page_copy(step, slot, p):
        dst = pl.multiple_of((slot * P + p) * POOL_PITCH, 8)
        return pltpu.make_async_copy(pool_ref.at[ids_ref[step * P + p]], buf_ref.at[pl.ds(dst, KV_COLS), :], sem.at[slot])

    def start_pages(step, slot):
        def go(p, c):
            page_copy(step, slot, p).start()
            return c
        lax.fori_loop(0, P, go, 0)

    @pl.when(i == 0)
    def _():
        start_pages(0, 0)

    @pl.when(i + 1 < n)
    def _():
        start_pages(i + 1, (i + 1) % 2)

    slot = i % 2

    def wait_page(p, c):
        page_copy(i, slot, p).wait()
        return c

    lax.fori_loop(0, P, wait_page, 0)
    base = slot * (P * POOL_PITCH)

    def feature_rows(f):
        return buf_ref[pl.ds(base + f, P, stride=POOL_PITCH), :]

    DC = 8
    for s in range(2):
        out = _dot(pe_ref[s].astype(BF16), w_ref[s])[0:1]
        for d0 in range(0, HEAD_DIM, DC):
            lhs = jnp.concatenate(
                [jnp.concatenate([feature_rows((s * N_KV + k) * HEAD_DIM + d) for d in range(d0, d0 + DC)], axis=1)
                 for k in range(N_KV)], axis=0).astype(BF16)
            out = out + _dot(lhs, w_ref[s, d0 * PAGE_SIZE:(d0 + DC) * PAGE_SIZE, :])
        for k in range(N_KV):
            o_ref[s * N_KV + k] = out[k * P:(k + 1) * P]


def _compress_pool(page_ids, pool_t, w2, pe2):
    n = page_ids.shape[0]
    P = math.gcd(n, POOL_PAGES)
    assert P % 8 == 0
    kdim = HEAD_DIM * PAGE_SIZE
    ncol = BLK_PER_PAGE * HEAD_DIM
    grid_spec = pltpu.PrefetchScalarGridSpec(
        num_scalar_prefetch=1,
        grid=(n // P,),
        in_specs=[pl.BlockSpec(memory_space=pl.ANY),
                  pl.BlockSpec((2, kdim, ncol), lambda i, ids: (0, 0, 0), pipeline_mode=pl.Buffered(1)),
                  pl.BlockSpec((2, 8, kdim), lambda i, ids: (0, 0, 0), pipeline_mode=pl.Buffered(1))],
        out_specs=pl.BlockSpec((2 * N_KV, P, ncol), lambda i, ids: (0, i, 0)),
        scratch_shapes=[pltpu.VMEM((2 * P * POOL_PITCH, PAGE_SIZE), F32), pltpu.SemaphoreType.DMA((2,))],
    )
    return pl.pallas_call(
        functools.partial(_compress_pool_body, P=P),
        grid_spec=grid_spec,
        out_shape=jax.ShapeDtypeStruct((2 * N_KV, n, ncol), F32),
        compiler_params=_cp(("arbitrary",)),
        name="nsa_compress_pool",
    )(page_ids, pool_t, w2, pe2)


def _blocks_by_page(kcv, lead):
    kcv = kcv.reshape(2, N_KV, *lead, BLK_PER_PAGE, HEAD_DIM)
    n = len(lead)
    perm = tuple(range(2, 2 + n)) + (2 + n, 0, 1, 3 + n)
    return jnp.transpose(kcv, perm).reshape(*lead, BLK_PER_PAGE, KV_COLS)


def _compress_pool_weights(cmp_w, cmp_pe):
    eye = jnp.eye(BLK_PER_PAGE, dtype=F32)
    w2 = jnp.einsum('scde,bB->sdbcBe', cmp_w, eye).reshape(2, HEAD_DIM * PAGE_SIZE, BLK_PER_PAGE * HEAD_DIM).astype(BF16)
    pe = jnp.transpose(cmp_pe, (0, 2, 1))[:, :, None, :]
    pe = jnp.broadcast_to(pe, (2, HEAD_DIM, BLK_PER_PAGE, CMP_BLOCK)).reshape(2, 1, HEAD_DIM * PAGE_SIZE)
    return w2, jnp.broadcast_to(pe, (2, 8, HEAD_DIM * PAGE_SIZE))


def _padded_queries(q):
    lo = lax.broadcasted_iota(jnp.int32, (TQ, LANES), 1) < HEAD_DIM
    out = []
    for h in range(N_HEADS):
        x = q[:, (h // 2) * LANES:(h // 2 + 1) * LANES]
        k = h // GROUP
        if h % 2 != k:
            x = pltpu.roll(x, HEAD_DIM, axis=1)
        out.append(jnp.where(lo if k == 0 else ~lo, x, 0.0).astype(BF16))
    return out


T_BEYOND, T_DIAG, T_NEAR, T_FAR, T_FAR_EDGE, T_NEAR_EDGE = range(6)
W_LANES = GROUP * TQ


def _col_max(mv, s):
    return jnp.maximum(mv, jnp.max(s.reshape(s.shape[0] // 8, 8, s.shape[1]), axis=0))


def _col_sum(lv, p):
    return lv + jnp.sum(p.reshape(p.shape[0] // 8, 8, p.shape[1]), axis=0)


def _scores(kv_ref, qg, page, pages):
    start = pl.multiple_of(page * TQ, TQ)
    return _dot_nt(kv_ref[0, pl.ds(start, pages * TQ), 0:LANES].astype(BF16), qg)


def _values_t(kvt_ref, k, page, pages):
    vts = [kvt_ref[0, page + p, LANES + k * HEAD_DIM:LANES + (k + 1) * HEAD_DIM, :] for p in range(pages)]
    return (vts[0] if pages == 1 else jnp.concatenate(vts, axis=1)).astype(BF16)


def _softmax_pass(s_ref, kvt_ref, row0, page0, n_steps, pages, m_rows, init):
    rows = pages * TQ

    def body(t, carry):
        out = []
        for k in range(N_KV):
            lv, acc = carry[k]
            start = pl.multiple_of(row0 + t * rows, TQ)
            p = jnp.exp(s_ref[k, pl.ds(start, rows), :] - m_rows[k])
            vt = _values_t(kvt_ref, k, page0 + t * pages, pages)
            out.append((_col_sum(lv, p), acc + _dot(vt, p.astype(BF16))))
        return tuple(out)

    return lax.fori_loop(0, n_steps, body, init)


def _normalise(m_row, lv, acc, extra=None):
    l = jnp.sum(lv, axis=0, keepdims=True)
    if extra is not None:
        l = l + extra
    valid = m_row > 0.5 * NEG
    return jnp.where(valid, acc, 0.0) / jnp.where(valid & (l > 0), l, 1.0)


def _window_branch(qgs, kv_ref, kvt_ref, s_ref, tiles_ref, i, n_win, m_init):
    lo = jnp.maximum(i - n_win, 0)

    def pass1(j, mvs):
        d = i - j
        if n_win == 1:
            t = jnp.where(d == 0, T_DIAG, T_NEAR_EDGE)
        else:
            t = jnp.where(d == n_win, T_FAR_EDGE, jnp.minimum(d, 2) + T_DIAG)
        out = []
        for k in range(N_KV):
            s = _scores(kv_ref, qgs[k], j, 1) + tiles_ref[t, k]
            s_ref[k, pl.ds(pl.multiple_of((j - lo) * TQ, TQ), TQ), :] = s
            out.append(_col_max(mvs[k], s))
        return tuple(out)

    mvs = lax.fori_loop(lo, i + 1, pass1, tuple(m_init))
    m_rows = [jnp.max(mv, axis=0, keepdims=True) for mv in mvs]
    zero = (jnp.zeros((8, W_LANES), F32), jnp.zeros((HEAD_DIM, W_LANES), F32))
    res = _softmax_pass(s_ref, kvt_ref, 0, lo, i + 1 - lo, 1, m_rows, (zero,) * N_KV)
    return m_rows, res


def _store_heads(o_ref, o_groups):
    for k in range(N_KV):
        for pair in range(GROUP // 2):
            a = o_groups[k][:, (2 * pair) * TQ:(2 * pair + 1) * TQ]
            b = o_groups[k][:, (2 * pair + 1) * TQ:(2 * pair + 2) * TQ]
            col = (k * GROUP + 2 * pair) * HEAD_DIM
            o_ref[0, :, col:col + 2 * HEAD_DIM] = jnp.concatenate([a, b], axis=0).T


def _nsa_prompt_body(q_ref, kvs_ref, vst_ref, kvw_ref, vwt_ref, kc_ref, vct_ref, gates_ref, tiles_ref, o_ref,
                     sel_ref, ssel_ref, swin_ref, *, T):
    i = pl.program_id(1)
    nc = T // CMP_BLOCK
    nsel = nc // 2
    t0 = i * TQ
    qpads = _padded_queries(q_ref[0] * SCALE)
    gt = jax.nn.sigmoid(gates_ref[0]).T
    W = GROUP * TQ
    c_lane = lax.broadcasted_iota(jnp.int32, (1, W), 1) & (TQ - 1)
    tpos = t0 + c_lane
    r_c = lax.broadcasted_iota(jnp.int32, (nc, W), 0)
    blk = 2 * (r_c % nsel) + r_c // nsel
    vis = (blk + 1) * CMP_BLOCK - 1 <= tpos
    j_s = lax.broadcasted_iota(jnp.int32, (nsel, TQ), 0)
    cur = (t0 + lax.broadcasted_iota(jnp.int32, (nsel, TQ), 1)) // SEL_BLOCK
    qgs, o_cmps = [], []
    for k in range(N_KV):
        qg = jnp.concatenate(qpads[GROUP * k:GROUP * (k + 1)], axis=0)
        qgs.append(qg)
        s = jnp.where(vis, _dot_nt(kc_ref[0].astype(BF16), qg), NEG)
        m = jnp.max(s, axis=0, keepdims=True)
        e = jnp.where(vis, jnp.exp(s - m), 0.0)
        den = jnp.sum(e, axis=0, keepdims=True)
        p = e / jnp.where(den > 0, den, 1.0)
        o_cmps.append(_dot(vct_ref[0, k * HEAD_DIM:(k + 1) * HEAD_DIM, :].astype(BF16), p.astype(BF16)))
        ps = p[:, 0:TQ] + p[:, TQ:2 * TQ] + p[:, 2 * TQ:3 * TQ] + p[:, 3 * TQ:4 * TQ]
        imp = ps[0:nsel] + ps[nsel:nc]
        forced = (j_s == 0) | (j_s == cur) | (j_s == cur - 1)
        score = jnp.where(forced, FORCE_SCORE, jnp.where(j_s <= cur, imp, NEG))
        rank = jnp.zeros((nsel, TQ), F32)
        for r in range(nsel):
            row = score[r:r + 1, :]
            ahead = jnp.where(j_s > r, jnp.where(row >= score, 1.0, 0.0), jnp.where(row > score, 1.0, 0.0))
            rank = rank + ahead
        chosen = (rank < N_SEL) & (j_s <= cur)
        sel_ref[k] = jnp.concatenate([jnp.where(chosen, 0.0, NEG)] * GROUP, axis=1)

    n_steps = (i + 2) // 2

    def sel_pass1(t, mvs):
        out = []
        for k in range(N_KV):
            s = _scores(kvs_ref, qgs[k], 2 * t, 2)
            parts = []
            for h in range(4):
                tile = jnp.clip(i - (2 * t + h // 2), -1, 2) + T_DIAG
                half = slice((h % 2) * SEL_BLOCK, (h % 2 + 1) * SEL_BLOCK)
                parts.append(s[h * SEL_BLOCK:(h + 1) * SEL_BLOCK] + sel_ref[k, pl.ds(4 * t + h, 1), :] + tiles_ref[tile, k, half, :])
            s = jnp.concatenate(parts, axis=0)
            ssel_ref[k, pl.ds(pl.multiple_of(t * 2 * TQ, 2 * TQ), 2 * TQ), :] = s
            out.append(_col_max(mvs[k], s))
        return tuple(out)

    m0 = jnp.full((8, W), NEG, F32)
    mvs = lax.fori_loop(0, n_steps, sel_pass1, (m0, m0))
    m_sel = [jnp.max(mv, axis=0, keepdims=True) for mv in mvs]
    zero = (jnp.zeros((8, W), F32), jnp.zeros((HEAD_DIM, W), F32))
    r_sel = _softmax_pass(ssel_ref, vst_ref, 0, 0, n_steps, 2, m_sel, (zero, zero))
    m_win, r_win = _window_branch(qgs, kvw_ref, vwt_ref, swin_ref, tiles_ref, i, NSA_WINDOW // TQ, (m0, m0))

    o_groups = []
    for k in range(N_KV):
        def gate_row(branch):
            return jnp.concatenate([gt[branch * N_HEADS + GROUP * k + hh:branch * N_HEADS + GROUP * k + hh + 1, :]
                                    for hh in range(GROUP)], axis=1)

        o_slc = _normalise(m_sel[k], *r_sel[k])
        o_win = _normalise(m_win[k], *r_win[k])
        o_groups.append(gate_row(0) * o_cmps[k] + gate_row(1) * o_slc + gate_row(2) * o_win)
    _store_heads(o_ref, o_groups)


def _bias_tiles(rel_bias):
    r = jnp.arange(TQ, dtype=jnp.int32)[:, None]
    c = jnp.arange(TQ, dtype=jnp.int32)[None, :]
    near = jnp.stack([_rel_bias_at(rel_bias, d * TQ + c - r) for d in range(2)])
    near = jnp.transpose(near.reshape(2, TQ, TQ, N_KV, GROUP), (0, 3, 1, 4, 2)).reshape(2, N_KV, TQ, GROUP * TQ)
    far = jnp.repeat(rel_bias[N_BUCKETS - 1].reshape(N_KV, 1, GROUP), TQ, axis=2).astype(F32)
    causal = jnp.tile(jnp.where(c >= r, 0.0, NEG), (1, GROUP))
    inside = jnp.tile(jnp.where(c < r, 0.0, NEG), (1, GROUP))
    full = jnp.zeros((N_KV, TQ, GROUP * TQ), F32)
    tiles = jnp.stack([full + NEG, near[0] + causal, near[1] + full, far + full, far + inside, near[1] + inside])
    return tiles.astype(F32)


def _nsa_prompt_attn(q, kvs, kvs_pt, kvw, kvw_pt, kc, vct, gates, btiles):
    B, T, _ = q.shape
    nc = T // CMP_BLOCK
    W = GROUP * TQ
    n_tiles = btiles.shape[0]
    seq = pl.BlockSpec((1, T, KV_COLS), lambda b, i: (b, 0, 0))
    seq_pt = pl.BlockSpec((1, T // PAGE_SIZE, KV_COLS, PAGE_SIZE), lambda b, i: (b, 0, 0, 0))
    return pl.pallas_call(
        functools.partial(_nsa_prompt_body, T=T),
        grid=(B, T // TQ),
        in_specs=[pl.BlockSpec((1, TQ, N_HEADS * HEAD_DIM), lambda b, i: (b, i, 0)),
                  seq, seq_pt, seq, seq_pt,
                  pl.BlockSpec((1, nc, LANES), lambda b, i: (b, 0, 0)),
                  pl.BlockSpec((1, LANES, nc), lambda b, i: (b, 0, 0)),
                  pl.BlockSpec((1, TQ, LANES), lambda b, i: (b, i, 0)),
                  pl.BlockSpec((n_tiles, N_KV, TQ, W), lambda b, i: (0, 0, 0, 0))],
        out_specs=pl.BlockSpec((1, TQ, N_HEADS * HEAD_DIM), lambda b, i: (b, i, 0)),
        out_shape=jax.ShapeDtypeStruct((B, T, N_HEADS * HEAD_DIM), F32),
        scratch_shapes=[pltpu.VMEM((N_KV, nc // 2, W), F32), pltpu.VMEM((N_KV, T, W), F32),
                        pltpu.VMEM((N_KV, NSA_WINDOW + TQ, W), F32)],
        compiler_params=_cp(("arbitrary", "arbitrary")),
        name="nsa_prompt_attn",
    )(q, kvs, kvs_pt, kvw, kvw_pt, kc, vct, gates, btiles)


def _swa_prompt_body(q_ref, kv_ref, vt_ref, sink_ref, tiles_ref, o_ref, s_ref):
    i = pl.program_id(1)
    qpads = _padded_queries(q_ref[0] * SCALE)
    qgs = [jnp.concatenate(qpads[GROUP * k:GROUP * (k + 1)], axis=0) for k in range(N_KV)]
    m_init = [jnp.broadcast_to(sink_ref[k], (8, W_LANES)) for k in range(N_KV)]
    m_rows, res = _window_branch(qgs, kv_ref, vt_ref, s_ref, tiles_ref, i, SWA_WINDOW // TQ, m_init)
    _store_heads(o_ref, [_normalise(m_rows[k], *res[k], extra=jnp.exp(sink_ref[k] - m_rows[k])) for k in range(N_KV)])


def _swa_prompt_attn(q, kv, kv_pt, sink_rows, btiles):
    B, T, _ = q.shape
    W = GROUP * TQ
    n_tiles = btiles.shape[0]
    return pl.pallas_call(
        _swa_prompt_body,
        grid=(B, T // TQ),
        in_specs=[pl.BlockSpec((1, TQ, N_HEADS * HEAD_DIM), lambda b, i: (b, i, 0)),
                  pl.BlockSpec((1, T, KV_COLS), lambda b, i: (b, 0, 0)),
                  pl.BlockSpec((1, T // PAGE_SIZE, KV_COLS, PAGE_SIZE), lambda b, i: (b, 0, 0, 0)),
                  pl.BlockSpec((N_KV, 1, W), lambda b, i: (0, 0, 0)),
                  pl.BlockSpec((n_tiles, N_KV, TQ, W), lambda b, i: (0, 0, 0, 0))],
        out_specs=pl.BlockSpec((1, TQ, N_HEADS * HEAD_DIM), lambda b, i: (b, i, 0)),
        out_shape=jax.ShapeDtypeStruct((B, T, N_HEADS * HEAD_DIM), F32),
        scratch_shapes=[pltpu.VMEM((N_KV, SWA_WINDOW + TQ, W), F32)],
        compiler_params=_cp(("arbitrary", "arbitrary")),
        name="swa_prompt_attn",
    )(q, kv, kv_pt, sink_rows, btiles)


def _rglru_body(xb_ref, gb_ref, prev0_ref, h0_ref, cw_ref, cb_ref, wr_ref, br_ref, wi_ref, bi_ref, lam_ref,
                y_ref, buf_ref, hl_ref, prev_ref, h_ref, a_ref, u_ref, *, tt):
    i = pl.program_id(0)

    @pl.when(i == 0)
    def _():
        prev_ref[...] = prev0_ref[...]
        h_ref[...] = h0_ref[...]

    x = xb_ref[...]
    B, W = x.shape[1], x.shape[2]
    xp = jnp.concatenate([prev_ref[...], x], axis=0)
    xc = xp[0:tt] * cw_ref[0:1, :]
    for j in range(1, LRU_CONV):
        xc = xc + xp[j:j + tt] * cw_ref[j:j + 1, :]
    xc = xc + cb_ref[...]
    prev_ref[...] = xp[tt:tt + LRU_CONV - 1]
    xf = xc.reshape(tt * B, W)
    xh = xf.astype(BF16)
    r = jax.nn.sigmoid(_dot(xh, wr_ref[...]) + br_ref[...])
    ig = jax.nn.sigmoid(_dot(xh, wi_ref[...]) + bi_ref[...])
    log_a = -LRU_C * r * _softplus(-lam_ref[...])
    a_ref[...] = jnp.exp(log_a).reshape(tt, B, W)
    u_ref[...] = (jnp.sqrt(-_expm1(2.0 * log_a)) * (ig * xf)).reshape(tt, B, W)

    def step(t, h):
        h = a_ref[t] * h + u_ref[t]
        y_ref[t] = h
        return h

    h = lax.fori_loop(0, tt, step, h_ref[...])
    h_ref[...] = h
    y_ref[...] = y_ref[...] * _gelu(gb_ref[...])
    buf_ref[...] = prev_ref[...]
    hl_ref[...] = h


def _rglru(xb, gb, prev0, h0, prm, tt):
    T, B, W = xb.shape
    full2 = lambda shp: pl.BlockSpec(shp, lambda i: (0, 0))
    blk = pl.BlockSpec((tt, B, W), lambda i: (i, 0, 0))
    return pl.pallas_call(
        functools.partial(_rglru_body, tt=tt),
        grid=(T // tt,),
        in_specs=[blk, blk, pl.BlockSpec((LRU_CONV - 1, B, W), lambda i: (0, 0, 0)), full2((B, W)),
                  full2((LRU_CONV, W)), full2((1, W)), full2((W, W)), full2((1, W)), full2((W, W)), full2((1, W)),
                  full2((1, W))],
        out_specs=[blk, pl.BlockSpec((LRU_CONV - 1, B, W), lambda i: (0, 0, 0)), full2((B, W))],
        out_shape=[jax.ShapeDtypeStruct((T, B, W), F32), jax.ShapeDtypeStruct((LRU_CONV - 1, B, W), F32),
                   jax.ShapeDtypeStruct((B, W), F32)],
        scratch_shapes=[pltpu.VMEM((LRU_CONV - 1, B, W), F32), pltpu.VMEM((B, W), F32),
                        pltpu.VMEM((tt, B, W), F32), pltpu.VMEM((tt, B, W), F32)],
        compiler_params=_cp(("arbitrary",)),
        name="rglru",
    )(xb, gb, prev0, h0, prm['conv_w'], prm['conv_b'], prm['w_r'], prm['b_r'], prm['w_i'], prm['b_i'], prm['lam'])


def _block_diag(w):
    n, d, e = w.shape
    return jnp.einsum('nde,nm->ndme', w, jnp.eye(n, dtype=w.dtype)).reshape(n * d, n * e)


S5_CHUNKS = 4
S5_CH_IN = S5_WIDTH // S5_CHUNKS
S5_CH_ST = S5_GROUPS * S5_STATE // S5_CHUNKS
S5_NSTATE = S5_GROUPS * S5_STATE


def _s5_disc(lr, li, dt):
    mag = jnp.exp(lr * dt)
    ab_re, ab_im = mag * jnp.cos(li * dt), mag * jnp.sin(li * dt)
    den = lr * lr + li * li
    f_re = ((ab_re - 1.0) * lr + ab_im * li) / den
    f_im = (ab_im * lr - (ab_re - 1.0) * li) / den
    return ab_re, ab_im, f_re, f_im


def _s5_param_body(lr_ref, li_ref, ldt_ref, lre_ref, lie_ref, bre_ref, bim_ref, are_ref, aim_ref, bbre_ref, bbim_ref):
    dt = jnp.exp(ldt_ref[...])
    ab_re, ab_im, _, _ = _s5_disc(lr_ref[...], li_ref[...], dt)
    are_ref[...] = ab_re
    aim_ref[...] = ab_im
    _, _, f_re, f_im = _s5_disc(lre_ref[...], lie_ref[...], dt)
    br, bim = bre_ref[...], bim_ref[...]
    bbre_ref[...] = f_re * br - f_im * bim
    bbim_ref[...] = f_re * bim + f_im * br


def _s5_params(lam_re, lam_im, log_dt, b_re, b_im):
    G, P, C = b_re.shape
    rep = lambda a: jnp.repeat(a, C, axis=1)
    shapes = [jax.ShapeDtypeStruct((G, P), F32)] * 2 + [jax.ShapeDtypeStruct((G, P * C), F32)] * 2
    return pl.pallas_call(_s5_param_body, out_shape=shapes, name="s5_discretise")(
        lam_re, lam_im, log_dt.reshape(G, 1), rep(lam_re), rep(lam_im), b_re.reshape(G, P * C), b_im.reshape(G, P * C))


def _s5_body(u_ref, h0r_ref, h0i_ref, are_ref, aim_ref, bre_ref, bim_ref, cre_ref, cim_ref, d_ref, wg_ref, bg_ref,
             y_ref, sr_ref, si_ref, hr_ref, hi_ref, xr_ref, xi_ref, sr_st_ref, si_st_ref, *, tt):
    i = pl.program_id(0)

    @pl.when(i == 0)
    def _():
        hr_ref[...] = h0r_ref[...]
        hi_ref[...] = h0i_ref[...]

    u3 = u_ref[...]
    B, W = u3.shape[1], u3.shape[2]
    uf = u3.reshape(tt * B, W)
    y_parts = []
    for ck in range(S5_CHUNKS):
        uc = uf[:, ck * S5_CH_IN:(ck + 1) * S5_CH_IN].astype(BF16)
        xr_ref[...] = _dot(uc, bre_ref[ck]).reshape(tt, B, S5_CH_ST)
        xi_ref[...] = _dot(uc, bim_ref[ck]).reshape(tt, B, S5_CH_ST)
        lanes = slice(ck * S5_CH_ST, (ck + 1) * S5_CH_ST)
        a_re = jnp.broadcast_to(are_ref[:, lanes], (B, S5_CH_ST))
        a_im = jnp.broadcast_to(aim_ref[:, lanes], (B, S5_CH_ST))

        def step(t, h):
            h_re, h_im = h
            n_re = a_re * h_re - a_im * h_im + xr_ref[t]
            n_im = a_re * h_im + a_im * h_re + xi_ref[t]
            sr_st_ref[t] = n_re
            si_st_ref[t] = n_im
            return n_re, n_im

        h_re, h_im = lax.fori_loop(0, tt, step, (hr_ref[:, lanes], hi_ref[:, lanes]))
        hr_ref[:, lanes] = h_re
        hi_ref[:, lanes] = h_im
        hre = sr_st_ref[...].reshape(tt * B, S5_CH_ST).astype(BF16)
        him = si_st_ref[...].reshape(tt * B, S5_CH_ST).astype(BF16)
        y_parts.append(_dot(hre, cre_ref[ck]) - _dot(him, cim_ref[ck]))
    y = jnp.concatenate(y_parts, axis=1) + d_ref[...] * uf
    g = _gelu(y)
    out = g * jax.nn.sigmoid(_dot(g.astype(BF16), wg_ref[...]) + bg_ref[...])
    y_ref[...] = out.reshape(tt, B, W)
    sr_ref[...] = hr_ref[...]
    si_ref[...] = hi_ref[...]


def _s5(u, h0_re, h0_im, prm, tt):
    T, B, W = u.shape
    full2 = lambda shp: pl.BlockSpec(shp, lambda i: (0, 0))
    full3 = lambda shp: pl.BlockSpec(shp, lambda i: (0, 0, 0))
    blk = pl.BlockSpec((tt, B, W), lambda i: (i, 0, 0))
    st = jax.ShapeDtypeStruct((B, S5_NSTATE), F32)
    return pl.pallas_call(
        functools.partial(_s5_body, tt=tt),
        grid=(T // tt,),
        in_specs=[blk, full2((B, S5_NSTATE)), full2((B, S5_NSTATE)), full2((1, S5_NSTATE)), full2((1, S5_NSTATE)),
                  full3((S5_CHUNKS, S5_CH_IN, S5_CH_ST)), full3((S5_CHUNKS, S5_CH_IN, S5_CH_ST)),
                  full3((S5_CHUNKS, S5_CH_ST, S5_CH_IN)), full3((S5_CHUNKS, S5_CH_ST, S5_CH_IN)),
                  full2((1, W)), full2((W, W)), full2((1, W))],
        out_specs=[blk, full2((B, S5_NSTATE)), full2((B, S5_NSTATE))],
        out_shape=[jax.ShapeDtypeStruct((T, B, W), F32), st, st],
        scratch_shapes=[pltpu.VMEM((B, S5_NSTATE), F32), pltpu.VMEM((B, S5_NSTATE), F32),
                        pltpu.VMEM((tt, B, S5_CH_ST), F32), pltpu.VMEM((tt, B, S5_CH_ST), F32),
                        pltpu.VMEM((tt, B, S5_CH_ST), F32), pltpu.VMEM((tt, B, S5_CH_ST), F32)],
        compiler_params=_cp(("arbitrary",)),
        name="s5",
    )(u, h0_re, h0_im, prm['a_re'], prm['a_im'], prm['b_re'], prm['b_im'], prm['c_re'], prm['c_im'],
      prm['d'], prm['w_glu'], prm['b_glu'])


def _s5_prepare(lam_re, lam_im, log_dt, b_re, b_im, c_re, c_im, d, w_glu, b_glu):
    a_re, a_im, bb_re, bb_im = _s5_params(lam_re, lam_im, log_dt, b_re, b_im)
    gpc = S5_GROUPS // S5_CHUNKS
    eye = jnp.eye(gpc, dtype=F32)

    def in_mat(bb):
        bb = bb.reshape(S5_CHUNKS, gpc, S5_STATE, S5_GROUP_CH)
        return jnp.einsum('kgpc,gh->kgchp', bb, eye).reshape(S5_CHUNKS, S5_CH_IN, S5_CH_ST).astype(BF16)

    def out_mat(c):
        c = c.reshape(S5_CHUNKS, gpc, S5_GROUP_CH, S5_STATE)
        return jnp.einsum('kgcp,gh->kgphc', c, eye).reshape(S5_CHUNKS, S5_CH_ST, S5_CH_IN).astype(BF16)

    return {'a_re': a_re.reshape(1, S5_NSTATE), 'a_im': a_im.reshape(1, S5_NSTATE),
            'b_re': in_mat(bb_re), 'b_im': in_mat(bb_im), 'c_re': out_mat(c_re), 'c_im': out_mat(c_im),
            'd': d.reshape(1, S5_WIDTH), 'w_glu': w_glu.astype(BF16), 'b_glu': b_glu.reshape(1, S5_WIDTH)}


FF_CHUNK = 896


def _ffn_up_body(x_ref, sc_ref, sh_ref, w_ref, cw_ref, cb_ref, prev0_ref, a_ref, buf_ref, prev_ref, *, tt):
    i = pl.program_id(0)

    @pl.when(i == 0)
    def _():
        prev_ref[...] = prev0_ref[...]

    x = x_ref[...]
    B, D = x.shape[1], x.shape[2]
    h = (x * (1.0 + sc_ref[...]) + sh_ref[...]).reshape(tt * B, D).astype(BF16)
    for c0 in range(0, D_FF, FF_CHUNK):
        cols = slice(c0, c0 + FF_CHUNK)
        g = _dot(h, w_ref[:, cols]).reshape(tt, B, FF_CHUNK)
        v = _dot(h, w_ref[:, D_FF + c0:D_FF + c0 + FF_CHUNK]).reshape(tt, B, FF_CHUNK)
        gp = jnp.concatenate([prev_ref[:, :, cols], g], axis=0)
        y = gp[0:tt] * cw_ref[0:1, cols]
        for j in range(1, FFN_CONV):
            y = y + gp[j:j + tt] * cw_ref[j:j + 1, cols]
        y = y + cb_ref[:, cols]
        prev_ref[:, :, cols] = gp[tt:tt + FFN_CONV - 1]
        a_ref[:, :, cols] = (_gelu(y) * v).astype(BF16)
    buf_ref[...] = prev_ref[...]


def _ffn_up(x, sc, sh, w_up, conv_w, conv_b, prev0, tt):
    T, B, D = x.shape
    full2 = lambda shp: pl.BlockSpec(shp, lambda i: (0, 0))
    return pl.pallas_call(
        functools.partial(_ffn_up_body, tt=tt),
        grid=(T // tt,),
        in_specs=[pl.BlockSpec((tt, B, D), lambda i: (i, 0, 0)), full2((B, D)), full2((B, D)),
                  full2((D, 2 * D_FF)), full2((FFN_CONV, D_FF)), full2((1, D_FF)),
                  pl.BlockSpec((FFN_CONV - 1, B, D_FF), lambda i: (0, 0, 0))],
        out_specs=[pl.BlockSpec((tt, B, D_FF), lambda i: (i, 0, 0)),
                   pl.BlockSpec((FFN_CONV - 1, B, D_FF), lambda i: (0, 0, 0))],
        out_shape=[jax.ShapeDtypeStruct((T, B, D_FF), BF16), jax.ShapeDtypeStruct((FFN_CONV - 1, B, D_FF), F32)],
        scratch_shapes=[pltpu.VMEM((FFN_CONV - 1, B, D_FF), F32)],
        compiler_params=_cp(("arbitrary",)),
        name="ffn_up_conv",
    )(x, sc, sh, w_up, conv_w, conv_b.reshape(1, D_FF), prev0)


N_PAST_PAGES = PAST_LEN // PAGE_SIZE
N_CAND = 3 * N_PAST_PAGES
N_PAST_CMP = PAST_LEN // CMP_BLOCK
N_CMP_ALL = N_PAST_CMP + 2
N_PAST_SEL = PAST_LEN // SEL_BLOCK


def _softmax_rows(s, mask):
    if mask is not None:
        s = jnp.where(mask, s, NEG)
    m = jnp.max(s, axis=1, keepdims=True)
    e = jnp.exp(s - m)
    if mask is not None:
        e = jnp.where(mask, e, 0.0)
    den = jnp.sum(e, axis=1, keepdims=True)
    return e / jnp.where(den > 0, den, 1.0)


def _cmp_sample_body(q_ref, kcv_ref, tail_ref):
    q = (q_ref[0] * SCALE).astype(BF16)
    n_pg = kcv_ref.shape[2]
    parts = []
    for blk in range(BLK_PER_PAGE):
        cols = slice(blk * HEAD_DIM, (blk + 1) * HEAD_DIM)
        parts.append(_by_group(_dot_nt(q, kcv_ref[0, 0, :, cols].astype(BF16)), _dot_nt(q, kcv_ref[1, 0, :, cols].astype(BF16))))
    parts.append(_by_group(_dot_nt(q, tail_ref[0, :, 0:HEAD_DIM].astype(BF16)),
                           _dot_nt(q, tail_ref[0, :, HEAD_DIM:2 * HEAD_DIM].astype(BF16))))
    s = jnp.concatenate(parts, axis=1)
    lane = lax.broadcasted_iota(jnp.int32, s.shape, 1)
    idx = jnp.where(lane < BLK_PER_PAGE * n_pg, BLK_PER_PAGE * (lane % n_pg) + lane // n_pg, lane)
    vis = (idx < N_CMP_ALL) & ((idx + 1) * CMP_BLOCK - 1 <= PAST_LEN)
    p = _softmax_rows(s, vis)
    pb = p.astype(BF16)
    o = jnp.zeros((N_HEADS, HEAD_DIM), F32)
    for blk in range(BLK_PER_PAGE):
        cols = slice(blk * HEAD_DIM, (blk + 1) * HEAD_DIM)
        pp = pb[:, blk * n_pg:(blk + 1) * n_pg]
        o = o + _by_group(_dot(pp, kcv_ref[2, 0, :, cols].astype(BF16)), _dot(pp, kcv_ref[3, 0, :, cols].astype(BF16)))
    pt = pb[:, BLK_PER_PAGE * n_pg:]
    o = o + _by_group(_dot(pt, tail_ref[0, :, 2 * HEAD_DIM:3 * HEAD_DIM].astype(BF16)),
                      _dot(pt, tail_ref[0, :, 3 * HEAD_DIM:4 * HEAD_DIM].astype(BF16)))
    row = lax.broadcasted_iota(jnp.int32, s.shape, 0)
    g0 = jnp.sum(jnp.where(row < GROUP, p, 0.0), axis=0, keepdims=True)
    g1 = jnp.sum(jnp.where(row >= GROUP, p, 0.0), axis=0, keepdims=True)
    gs = jnp.concatenate([g0, g1], axis=0)
    gt = jnp.sum(gs[:, 4 * n_pg:], axis=1, keepdims=True)
    first = lax.broadcasted_iota(jnp.int32, (N_KV, n_pg), 1) == 0
    imp = jnp.concatenate([gs[:, 0:n_pg] + gs[:, n_pg:2 * n_pg], gs[:, 2 * n_pg:3 * n_pg] + gs[:, 3 * n_pg:4 * n_pg],
                           jnp.where(first, gt, 0.0)], axis=1)
    return o, imp


SAMPLE_REQS = 4


def _cmp_sample_batch(q_ref, kcv_ref, tail_ref, o_ref, imp_ref, *, R):
    res = [_cmp_sample_body(q_ref.at[pl.ds(r, 1)], kcv_ref.at[:, pl.ds(r, 1)], tail_ref.at[pl.ds(r, 1)]) for r in range(R)]
    for r in range(R):
        o_ref[r], imp_ref[r] = res[r]


def _cmp_sample(q, kcv_pages, tail):
    B = q.shape[0]
    R = math.gcd(B, SAMPLE_REQS)
    n_pg = kcv_pages.shape[2]
    return pl.pallas_call(
        functools.partial(_cmp_sample_batch, R=R),
        grid=(B // R,),
        in_specs=[pl.BlockSpec((R, N_HEADS, HEAD_DIM), lambda b: (b, 0, 0)),
                  pl.BlockSpec((2 * N_KV, R, n_pg, BLK_PER_PAGE * HEAD_DIM), lambda b: (0, b, 0, 0)),
                  pl.BlockSpec((R, n_pg, KV_COLS), lambda b: (b, 0, 0))],
        out_specs=[pl.BlockSpec((R, N_HEADS, HEAD_DIM), lambda b: (b, 0, 0)),
                   pl.BlockSpec((R, N_KV, 3 * n_pg), lambda b: (b, 0, 0))],
        out_shape=[jax.ShapeDtypeStruct((B, N_HEADS, HEAD_DIM), F32), jax.ShapeDtypeStruct((B, N_KV, 3 * n_pg), F32)],
        compiler_params=_cp(("arbitrary",)),
        name="nsa_sample_cmp",
    )(q, kcv_pages, tail)


def _topk_sample_body(imp_ref, idx_ref):
    imp = imp_ref[...]
    r = lax.broadcasted_iota(jnp.int32, imp.shape, 0)
    n_pg = N_PAST_PAGES
    big = 4 * N_PAST_SEL
    j = jnp.where(r < n_pg, 2 * r, jnp.where(r < 2 * n_pg, 2 * (r - n_pg) + 1, jnp.where(r == 2 * n_pg, N_PAST_SEL, big)))
    cur = N_PAST_SEL
    forced = (j == 0) | (j == cur) | (j == cur - 1)
    score = jnp.where(forced, FORCE_SCORE, jnp.where(j <= cur, imp, -jnp.inf))
    rows = []
    for _ in range(N_SEL):
        top = jnp.max(score, axis=0, keepdims=True)
        idx = jnp.min(jnp.where(score == top, j, big), axis=0, keepdims=True)
        rows.append(idx)
        score = jnp.where(j == idx, -jnp.inf, score)
    idx_ref[...] = jnp.concatenate(rows, axis=0)


def _topk_sample(imp_t):
    return pl.pallas_call(
        _topk_sample_body,
        out_shape=jax.ShapeDtypeStruct((N_SEL, imp_t.shape[1]), jnp.int32),
        name="nsa_sample_topk",
    )(imp_t)


def _shift_in(win_ref, new_ref, width):
    lane = lax.broadcasted_iota(jnp.int32, (KV_COLS, width), 1)
    return jnp.where(lane == width - 1, new_ref[0], pltpu.roll(win_ref[0], width - 1, axis=1))


def _by_group(a0, a1):
    row = lax.broadcasted_iota(jnp.int32, a0.shape, 0)
    return jnp.where(row < GROUP, a0, a1)


def _window_scores(q, buf):
    return _by_group(_dot(q, buf[0:HEAD_DIM].astype(BF16)), _dot(q, buf[HEAD_DIM:2 * HEAD_DIM].astype(BF16)))


def _window_values(p, buf):
    pb = p.astype(BF16)
    return _by_group(_dot_nt(pb, buf[2 * HEAD_DIM:3 * HEAD_DIM].astype(BF16)),
                     _dot_nt(pb, buf[3 * HEAD_DIM:4 * HEAD_DIM].astype(BF16)))


def _selwin_sample_body(idx_ref, page_ref, pool_ref, q_ref, newrow_ref, win_ref, newwin_ref, bsel_ref, bwin_ref, b0_ref,
                        gate_ref, ocmp_ref, o_ref, winout_ref, g_ref, bias_ref, sem, *, R):
    step = pl.program_id(0)
    nsteps = pl.num_programs(0)
    per_req = N_KV * N_SEL
    nl = N_SEL * PAGE_SIZE

    def page_copy(st, slot, r, k, n):
        page = page_ref[(st * R + r) * per_req + k * N_SEL + n]
        return pltpu.make_async_copy(pool_ref.at[page, :, k],
                                     g_ref.at[slot, r, k, :, :, pl.ds(n * PAGE_SIZE, PAGE_SIZE)],
                                     sem.at[slot, (r * N_KV + k) * N_SEL + n])

    def start_all(st, slot):
        for r in range(R):
            for k in range(N_KV):
                for n in range(N_SEL):
                    page_copy(st, slot, r, k, n).start()

    @pl.when(step == 0)
    def _():
        start_all(0, 0)

    @pl.when(step + 1 < nsteps)
    def _():
        start_all(step + 1, (step + 1) % 2)

    slot = step % 2
    for r in range(R):
        for k in range(N_KV):
            for n in range(N_SEL):
                page_copy(step, slot, r, k, n).wait()

    lane = lax.broadcasted_iota(jnp.int32, (1, nl), 1)
    row = lax.broadcasted_iota(jnp.int32, (N_HEADS, nl), 0)
    row1 = lax.broadcasted_iota(jnp.int32, (N_HEADS, 1), 0)
    results = []
    for r in range(R):
        q = (q_ref[r] * SCALE).astype(BF16)
        scores, masks, vts, has_new = [], [], [], []
        for k in range(N_KV):
            nvec = jnp.zeros((1, nl), jnp.int32)
            seen = jnp.int32(0)
            for n in range(N_SEL):
                blk = idx_ref[(step * R + r) * per_req + k * N_SEL + n]
                nvec = jnp.where(lane // PAGE_SIZE == n, blk, nvec)
                seen = jnp.maximum(seen, jnp.where(blk == N_PAST_SEL, 1, 0))
                bias_ref[r, k, :, n * PAGE_SIZE:(n + 1) * PAGE_SIZE] = bsel_ref[blk]
            has_new.append(seen)
            half = jnp.minimum(nvec, N_PAST_SEL - 1) % (PAGE_SIZE // SEL_BLOCK)
            in_half = (lane % PAGE_SIZE) // SEL_BLOCK == half
            masks.append(jnp.where(in_half & (nvec < N_PAST_SEL), 1.0, 0.0))
            scores.append(_dot(q, g_ref[slot, r, k, 0].astype(BF16)) + bias_ref[r, k])
            vts.append(g_ref[slot, r, k, 1].astype(BF16))
        mask = jnp.where(row < GROUP, masks[0], masks[1]) > 0.5
        s = jnp.where(mask, jnp.where(row < GROUP, scores[0], scores[1]), NEG)
        new = newrow_ref[r].astype(BF16).astype(F32)
        own = lambda a, b: jnp.where(row1 < GROUP, jnp.broadcast_to(a, (N_HEADS, HEAD_DIM)), jnp.broadcast_to(b, (N_HEADS, HEAD_DIM)))
        k_new = own(new[:, 0:HEAD_DIM], new[:, HEAD_DIM:2 * HEAD_DIM])
        v_new = own(new[:, 2 * HEAD_DIM:3 * HEAD_DIM], new[:, 3 * HEAD_DIM:4 * HEAD_DIM])
        valid_new = jnp.where(row1 < GROUP, has_new[0], has_new[1]) > 0
        s_new = jnp.sum(q.astype(F32) * k_new, axis=1, keepdims=True) + b0_ref[...]
        s_new = jnp.where(valid_new, s_new, NEG)
        m = jnp.maximum(jnp.max(s, axis=1, keepdims=True), s_new)
        e = jnp.where(mask, jnp.exp(s - m), 0.0)
        e_new = jnp.where(valid_new, jnp.exp(s_new - m), 0.0)
        den = jnp.sum(e, axis=1, keepdims=True) + e_new
        den = jnp.where(den > 0, den, 1.0)
        pb = (e / den).astype(BF16)
        p_new = (e_new / den).astype(BF16).astype(F32)
        o_slc = _by_group(_dot_nt(pb, vts[0]), _dot_nt(pb, vts[1])) + p_new * v_new
        win = _shift_in(win_ref.at[pl.ds(r, 1)], newwin_ref.at[pl.ds(r, 1)], NSA_WINDOW)
        pw = _softmax_rows(_window_scores(q, win) + bwin_ref[...], None)
        o_win = _window_values(pw, win)
        g = jax.nn.sigmoid(gate_ref[r])
        results.append((win, g[:, 0:1] * ocmp_ref[r] + g[:, 1:2] * o_slc + g[:, 2:3] * o_win))
    for r in range(R):
        winout_ref[r], o_ref[r] = results[r]


def _selwin_sample(idx_flat, page_flat, pool5, q, new_row, win, new_win, bias_sel, bias_win, bias0, gates, o_cmp):
    B = q.shape[0]
    R = math.gcd(B, SAMPLE_REQS)
    nl = N_SEL * PAGE_SIZE
    spec3 = lambda shp: pl.BlockSpec(shp, lambda b, *_: (b, 0, 0))
    grid_spec = pltpu.PrefetchScalarGridSpec(
        num_scalar_prefetch=2,
        grid=(B // R,),
        in_specs=[pl.BlockSpec(memory_space=pl.ANY),
                  spec3((R, N_HEADS, HEAD_DIM)), spec3((R, 1, KV_COLS)), spec3((R, KV_COLS, NSA_WINDOW)),
                  spec3((R, KV_COLS, 1)),
                  pl.BlockSpec((N_PAST_SEL + 1, N_HEADS, PAGE_SIZE), lambda b, *_: (0, 0, 0)),
                  pl.BlockSpec((N_HEADS, NSA_WINDOW), lambda b, *_: (0, 0)),
                  pl.BlockSpec((N_HEADS, 1), lambda b, *_: (0, 0)),
                  spec3((R, N_HEADS, 3)), spec3((R, N_HEADS, HEAD_DIM))],
        out_specs=[spec3((R, N_HEADS, HEAD_DIM)), spec3((R, KV_COLS, NSA_WINDOW))],
        scratch_shapes=[pltpu.VMEM((2, R, N_KV, 2, HEAD_DIM, nl), F32), pltpu.VMEM((R, N_KV, N_HEADS, nl), F32),
                        pltpu.SemaphoreType.DMA((2, R * N_KV * N_SEL))],
    )
    return pl.pallas_call(
        functools.partial(_selwin_sample_body, R=R),
        grid_spec=grid_spec,
        out_shape=[jax.ShapeDtypeStruct((B, N_HEADS, HEAD_DIM), F32), jax.ShapeDtypeStruct((B, KV_COLS, NSA_WINDOW), F32)],
        compiler_params=_cp(("arbitrary",)),
        name="nsa_sample_sel_win",
    )(idx_flat, page_flat, pool5, q, new_row, win, new_win, bias_sel, bias_win, bias0, gates, o_cmp)


def _swa_sample_one(q, win, bias, sink):
    q = (q * SCALE).astype(BF16)
    s = _window_scores(q, win) + bias
    m = jnp.maximum(jnp.max(s, axis=1, keepdims=True), sink)
    e = jnp.exp(s - m)
    den = jnp.sum(e, axis=1, keepdims=True) + jnp.exp(sink - m)
    p = e / jnp.where(den > 0, den, 1.0)
    return _window_values(p, win)


def _swa_sample_batch(q_ref, win_ref, new_ref, bias_ref, sink_ref, o_ref, winout_ref, *, R):
    wins = [_shift_in(win_ref.at[pl.ds(r, 1)], new_ref.at[pl.ds(r, 1)], SWA_WINDOW) for r in range(R)]
    outs = [_swa_sample_one(q_ref[r], wins[r], bias_ref[...], sink_ref[...]) for r in range(R)]
    for r in range(R):
        winout_ref[r] = wins[r]
        o_ref[r] = outs[r]


def _swa_sample(q, win, new_col, bias, sinks):
    B = q.shape[0]
    R = math.gcd(B, SAMPLE_REQS)
    spec3 = lambda shp: pl.BlockSpec(shp, lambda b: (b, 0, 0))
    return pl.pallas_call(
        functools.partial(_swa_sample_batch, R=R),
        grid=(B // R,),
        in_specs=[spec3((R, N_HEADS, HEAD_DIM)), spec3((R, KV_COLS, SWA_WINDOW)), spec3((R, KV_COLS, 1)),
                  pl.BlockSpec((N_HEADS, SWA_WINDOW), lambda b: (0, 0)),
                  pl.BlockSpec((N_HEADS, 1), lambda b: (0, 0))],
        out_specs=[spec3((R, N_HEADS, HEAD_DIM)), spec3((R, KV_COLS, SWA_WINDOW))],
        out_shape=[jax.ShapeDtypeStruct((B, N_HEADS, HEAD_DIM), F32), jax.ShapeDtypeStruct((B, KV_COLS, SWA_WINDOW), F32)],
        compiler_params=_cp(("arbitrary",)),
        name="swa_sample",
    )(q, win, new_col, bias, sinks)


def _window_to_device_view(cache, width):
    return jnp.transpose(cache, (0, 1, 3, 4, 5, 2)).reshape(cache.shape[0], KV_COLS, width)


def _window_from_device_view(buf, width):
    return jnp.transpose(buf.reshape(buf.shape[0], 1, 2, N_KV, HEAD_DIM, width), (0, 1, 5, 2, 3, 4))


def _split_mod(m, rows):
    sh, sc, gt = m[rows, 0:D_MODEL], m[rows, D_MODEL:2 * D_MODEL], m[rows, 2 * D_MODEL:]
    return sh, sc, gt


def _ffn(x_tm, mod, rows, B, T, w_up, conv_w, conv_b, w_down, prev0, ln_g, ln_b, tt, tm):
    sh, sc, gt = _split_mod(mod, rows)
    a, buf = _ffn_up(x_tm.reshape(T, B, D_MODEL), sc, sh, w_up, conv_w, conv_b, prev0, tt)
    gate = gt[:, None, :] if T > 1 else gt[None]
    nb, nt = (B, T) if T > 1 else (1, B)
    x = _mm_res_ln([a.reshape(nt, nb * D_FF)], [True], [w_down], x_tm.reshape(nt, nb * D_MODEL), True, gate,
                   ln_g, ln_b, False, nb, nt, tm)
    return x, buf


def kernel(x_prompt, x_sample, cache_kv_cmp, cache_kv_slc, cache_kv_win, state_lru_conv, state_lru_h, cache_kv_swa, state_s5_re, state_s5_im, state_ffn_conv, page_table, c_prompt, c_sample, rel_bias, w_mod, b_mod, ln_g, ln_b, w_in_ab, w_out_ab, nsa_cmp_pe, nsa_cmp_w, lru_conv_w, lru_conv_b, lru_w_r, lru_b_r, lru_w_i, lru_b_i, lru_lambda, w_in_cd, w_out_cd, swa_sinks, s5_lambda_re, s5_lambda_im, s5_log_dt, s5_b_re, s5_b_im, s5_c_re, s5_c_im, s5_d, s5_w_glu, s5_b_glu, w_ffn_up, w_ffn_conv, b_ffn_conv, w_ffn_down):
    assert N_AB == 1 and DEPTH == 2
    Bp, T, D = x_prompt.shape
    Bs = x_sample.shape[0]
    n_pool = cache_kv_cmp.shape[0]
    TM = 512 if T % 512 == 0 else T
    TT_REC = 64 if T % 64 == 0 else T
    TT_S5 = 32 if T % 32 == 0 else T
    TT_FF = 32 if T % 32 == 0 else T
    prow, srow = slice(0, Bp), slice(Bp, Bp + Bs)

    mod = _ada_mod_all(jnp.concatenate([c_prompt, c_sample], axis=0),
                       w_mod.reshape(2 * DEPTH, D, 3 * D), b_mod.reshape(2 * DEPTH, 1, 3 * D))
    btiles = _bias_tiles(rel_bias)
    xs = x_sample.reshape(1, Bs, D)

    w = w_in_ab[0].astype(BF16)
    o = np.cumsum([0, N_HEADS * HEAD_DIM, KV_COLS, KV_COLS, KV_COLS, 3 * N_HEADS, LRU_WIDTH, LRU_WIDTH])
    w_gates = jnp.pad(w[:, o[4]:o[5]], ((0, 0), (0, LANES - 3 * N_HEADS)))
    ws = [w[:, o[0]:o[1]], w[:, o[1]:o[2]], w[:, o[2]:o[3]], w[:, o[3]:o[4]], w_gates, w[:, o[5]:o[6]], w[:, o[6]:o[7]]]
    kinds_p = [['bm'], ['pt'], ['bm', 'pt'], ['bm', 'pt'], ['bm'], ['tm'], ['tm']]
    kinds_s = [['bm'], ['bm'], ['bm'], ['bm'], ['bm'], ['tm'], ['tm']]
    w2, pe2 = _compress_pool_weights(nsa_cmp_w[0], nsa_cmp_pe[0])
    lru = {'conv_w': lru_conv_w[0], 'conv_b': lru_conv_b[0].reshape(1, -1),
           'w_r': _block_diag(lru_w_r[0]).astype(BF16), 'b_r': lru_b_r[0].reshape(1, -1),
           'w_i': _block_diag(lru_w_i[0]).astype(BF16), 'b_i': lru_b_i[0].reshape(1, -1),
           'lam': lru_lambda[0].reshape(1, -1)}
    w_out = w_out_ab[0].astype(BF16)
    w_out_parts = [w_out[0:N_HEADS * HEAD_DIM], w_out[N_HEADS * HEAD_DIM:]]

    shp, scp, gtp = _split_mod(mod[0], prow)
    q, kvc_pt, kvs, kvs_pt, kvw, kvw_pt, gates, xb, gb = _modmm(x_prompt, scp[:, None], shp[:, None],
                                                                list(zip(ws, kinds_p)), TM)
    nc = T // CMP_BLOCK
    npg = T // PAGE_SIZE
    kcv = _compress_pool(jnp.arange(Bp * npg, dtype=jnp.int32), kvc_pt.reshape(Bp * npg, KV_COLS, PAGE_SIZE), w2, pe2)
    kcv = _blocks_by_page(kcv, (Bp, npg)).reshape(Bp, nc, KV_COLS)
    kcv = jnp.concatenate([kcv[:, 0::2], kcv[:, 1::2]], axis=1)
    o_att = _nsa_prompt_attn(q, kvs, kvs_pt, kvw, kvw_pt, kcv[:, :, 0:LANES], jnp.swapaxes(kcv[:, :, LANES:], 1, 2),
                             gates, btiles)
    y_lru, p_lru_buf, p_lru_h = _rglru(xb.reshape(T, Bp, LRU_WIDTH), gb.reshape(T, Bp, LRU_WIDTH),
                                       jnp.zeros((LRU_CONV - 1, Bp, LRU_WIDTH), F32), jnp.zeros((Bp, LRU_WIDTH), F32),
                                       lru, TT_REC)
    xp1 = _mm_res_ln([o_att, y_lru.reshape(T, Bp * LRU_WIDTH)], [False, True], w_out_parts, x_prompt, False,
                     gtp[:, None], ln_g[0, 0], ln_b[0, 0], True, Bp, T, TM)
    paged_out = lambda a: jnp.transpose(a.reshape(Bp, npg, 1, 2, N_KV, HEAD_DIM, PAGE_SIZE), (0, 1, 6, 2, 3, 4, 5))
    p_kv_cmp, p_kv_slc = paged_out(kvc_pt), paged_out(kvs_pt)
    wk = min(NSA_WINDOW, T)
    p_kv_win = _window_from_device_view(
        jnp.swapaxes(kvw_pt[:, npg - wk // PAGE_SIZE:], 1, 2).reshape(Bp, KV_COLS, wk), wk)

    shs, scs, gts = _split_mod(mod[0], srow)
    q_s, kvc_s, kvs_s, kvw_s, gates_s, xb_s, gb_s = _modmm(xs, scs[None], shs[None], list(zip(ws, kinds_s)), Bs)
    q_s, kvc_s, kvs_s, kvw_s, gates_s = q_s[0], kvc_s[0], kvs_s[0], kvw_s[0], gates_s[0]
    pool_cmp_t = jnp.transpose(cache_kv_cmp, (0, 2, 3, 4, 5, 1)).reshape(n_pool, KV_COLS, PAGE_SIZE)
    pool_slc_t = jnp.transpose(cache_kv_slc, (0, 2, 3, 4, 5, 1)).reshape(n_pool, 2, N_KV, HEAD_DIM, PAGE_SIZE)
    n_pages = page_table.shape[1]
    kcv_past = _compress_pool(page_table.reshape(-1), pool_cmp_t, w2, pe2)
    kcv_past = kcv_past.reshape(2 * N_KV, Bs, n_pages, BLK_PER_PAGE * HEAD_DIM)
    tail_pages = jnp.pad(kvc_s[:, :, None], ((0, 0), (0, 0), (0, PAGE_SIZE - 1)))
    kcv_tail = _compress_pool(jnp.arange(Bs, dtype=jnp.int32), tail_pages, w2, pe2)
    kcv_tail = _blocks_by_page(kcv_tail, (Bs,))[:, 0:SEL_BLOCK // CMP_BLOCK]
    kcv_tail = jnp.pad(kcv_tail, ((0, 0), (0, N_PAST_PAGES - SEL_BLOCK // CMP_BLOCK), (0, 0)))
    q_s3 = q_s.reshape(Bs, N_HEADS, HEAD_DIM)
    o_cmp_s, imp_s = _cmp_sample(q_s3, kcv_past, kcv_tail)
    idx_t = _topk_sample(imp_s.reshape(Bs * N_KV, N_CAND).T)
    idx = idx_t.T.reshape(Bs, N_KV, N_SEL)
    per_page_s = PAGE_SIZE // SEL_BLOCK
    idx_p = jnp.minimum(idx, N_PAST_SEL - 1)
    pages = page_table[jnp.arange(Bs)[:, None, None], idx_p // per_page_s]
    blk_id = jnp.arange(N_PAST_SEL + 1, dtype=jnp.int32)
    kpos = blk_id[:, None] * SEL_BLOCK + jnp.arange(SEL_BLOCK, dtype=jnp.int32)[None, :]
    bias_half = jnp.swapaxes(_rel_bias_at(rel_bias, PAST_LEN - kpos), 1, 2)
    in_upper = (jnp.minimum(blk_id, N_PAST_SEL - 1) % per_page_s == 1)[:, None, None]
    zeros_half = jnp.zeros_like(bias_half)
    bias_sel = jnp.where(in_upper, jnp.concatenate([zeros_half, bias_half], axis=2),
                         jnp.concatenate([bias_half, zeros_half], axis=2))
    bias_win = _rel_bias_at(rel_bias, NSA_WINDOW - 1 - jnp.arange(NSA_WINDOW, dtype=jnp.int32)).T
    gates3 = jnp.swapaxes(gates_s[:, 0:3 * N_HEADS].reshape(Bs, 3, N_HEADS), 1, 2)
    o_att_s, win_new = _selwin_sample(idx.reshape(-1), pages.reshape(-1).astype(jnp.int32), pool_slc_t,
                                      q_s3, kvs_s[:, None, :],
                                      _window_to_device_view(cache_kv_win, NSA_WINDOW), kvw_s[:, :, None],
                                      bias_sel, bias_win, _rel_bias_at(rel_bias, jnp.zeros((1,), jnp.int32)).T,
                                      gates3, o_cmp_s)
    y_lru_s, s_lru_buf, s_lru_h = _rglru(xb_s.reshape(1, Bs, LRU_WIDTH), gb_s.reshape(1, Bs, LRU_WIDTH),
                                         jnp.swapaxes(state_lru_conv[:, 0], 0, 1), state_lru_h[:, 0], lru, 1)
    xs1 = _mm_res_ln([o_att_s.reshape(1, Bs, N_HEADS * HEAD_DIM), y_lru_s], [False, False], w_out_parts, xs, False,
                     gts[None], ln_g[0, 0], ln_b[0, 0], True, 1, Bs, Bs)

    ffn_w = [(w_ffn_up[li].astype(BF16), w_ffn_conv[li], b_ffn_conv[li], w_ffn_down[li].astype(BF16)) for li in range(DEPTH)]
    zero_ff = jnp.zeros((FFN_CONV - 1, Bp, D_FF), F32)
    xp2, p_ff0 = _ffn(xp1, mod[1], prow, Bp, T, *ffn_w[0], zero_ff, ln_g[0, 1], ln_b[0, 1], TT_FF, TM)
    xs2, s_ff0 = _ffn(xs1, mod[1], srow, Bs, 1, *ffn_w[0], jnp.swapaxes(state_ffn_conv[:, 0], 0, 1),
                      ln_g[0, 1], ln_b[0, 1], 1, Bs)

    w = w_in_cd[0].astype(BF16)
    ws = [w[:, 0:N_HEADS * HEAD_DIM], w[:, N_HEADS * HEAD_DIM:N_HEADS * HEAD_DIM + KV_COLS], w[:, N_HEADS * HEAD_DIM + KV_COLS:]]
    s5p = _s5_prepare(s5_lambda_re[0], s5_lambda_im[0], s5_log_dt[0], s5_b_re[0], s5_b_im[0], s5_c_re[0], s5_c_im[0],
                      s5_d[0], s5_w_glu[0], s5_b_glu[0])
    w_out = w_out_cd[0].astype(BF16)
    w_out_parts = [w_out[0:N_HEADS * HEAD_DIM], w_out[N_HEADS * HEAD_DIM:]]
    sink_rows = jnp.repeat(swa_sinks[0].reshape(N_KV, 1, GROUP), TQ, axis=2)

    shp, scp, gtp = _split_mod(mod[2], prow)
    q, kv, kv_pt, u = _modmm(xp2, scp[:, None], shp[:, None], list(zip(ws, [['bm'], ['bm', 'pt'], ['tm']])), TM)
    o_att = _swa_prompt_attn(q, kv, kv_pt, sink_rows, btiles)
    zst = jnp.zeros((Bp, S5_NSTATE), F32)
    y_s5, p_s5_re, p_s5_im = _s5(u.reshape(T, Bp, S5_WIDTH), zst, zst, s5p, TT_S5)
    xp3 = _mm_res_ln([o_att, y_s5.reshape(T, Bp * S5_WIDTH)], [False, True], w_out_parts, xp2, False,
                     gtp[:, None], ln_g[1, 0], ln_b[1, 0], True, Bp, T, TM)
    assert SWA_WINDOW == PAGE_SIZE and T >= SWA_WINDOW
    p_kv_swa = _window_from_device_view(kv_pt[:, npg - 1], SWA_WINDOW)

    shs, scs, gts = _split_mod(mod[2], srow)
    q_s, kv_s, u_s = _modmm(xs2, scs[None], shs[None], list(zip(ws, [['bm'], ['bm'], ['tm']])), Bs)
    bias_swa = _rel_bias_at(rel_bias, SWA_WINDOW - 1 - jnp.arange(SWA_WINDOW, dtype=jnp.int32)).T
    o_att_s, swa_new = _swa_sample(q_s[0].reshape(Bs, N_HEADS, HEAD_DIM), _window_to_device_view(cache_kv_swa, SWA_WINDOW),
                                   kv_s[0][:, :, None], bias_swa, swa_sinks[0].reshape(N_HEADS, 1))
    y_s5_s, s_s5_re, s_s5_im = _s5(u_s.reshape(1, Bs, S5_WIDTH), state_s5_re[:, 0].reshape(Bs, S5_NSTATE),
                                   state_s5_im[:, 0].reshape(Bs, S5_NSTATE), s5p, 1)
    xs3 = _mm_res_ln([o_att_s.reshape(1, Bs, N_HEADS * HEAD_DIM), y_s5_s], [False, False], w_out_parts, xs2, False,
                     gts[None], ln_g[1, 0], ln_b[1, 0], True, 1, Bs, Bs)

    xp4, p_ff1 = _ffn(xp3, mod[3], prow, Bp, T, *ffn_w[1], zero_ff, ln_g[1, 1], ln_b[1, 1], TT_FF, TM)
    xs4, s_ff1 = _ffn(xs3, mod[3], srow, Bs, 1, *ffn_w[1], jnp.swapaxes(state_ffn_conv[:, 1], 0, 1),
                      ln_g[1, 1], ln_b[1, 1], 1, Bs)

    kv6 = lambda a, n: a.reshape(n, 1, 1, 2, N_KV, HEAD_DIM)
    return (xp4, xs4.reshape(Bs, 1, D),
            p_kv_cmp, p_kv_slc, p_kv_win,
            jnp.swapaxes(p_lru_buf, 0, 1)[:, None], p_lru_h[:, None], p_kv_swa,
            p_s5_re.reshape(Bp, 1, S5_GROUPS, S5_STATE), p_s5_im.reshape(Bp, 1, S5_GROUPS, S5_STATE),
            jnp.stack([jnp.swapaxes(p_ff0, 0, 1), jnp.swapaxes(p_ff1, 0, 1)], axis=1),
            kv6(kvc_s, Bs), kv6(kvs_s, Bs), _window_from_device_view(win_new, NSA_WINDOW),
            jnp.swapaxes(s_lru_buf, 0, 1)[:, None], s_lru_h[:, None],
            _window_from_device_view(swa_new, SWA_WINDOW),
            s_s5_re.reshape(Bs, 1, S5_GROUPS, S5_STATE), s_s5_im.reshape(Bs, 1, S5_GROUPS, S5_STATE),
            jnp.stack([jnp.swapaxes(s_ff0, 0, 1), jnp.swapaxes(s_ff1, 0, 1)], axis=1))
```

```python
import functools
import math

import numpy as np
import jax
import jax.numpy as jnp
from jax import lax
from jax.experimental import pallas as pl
from jax.experimental.pallas import tpu as pltpu

D_MODEL = 1024
DEPTH = 2
PAST_LEN = 16384
PAGE_SIZE = 128
HEAD_DIM = 64
N_AB = (DEPTH + 1) // 2
N_HEADS = 8
N_KV = 2
GROUP = N_HEADS // N_KV
CMP_BLOCK = 32
SEL_BLOCK = 64
N_SEL = 16
NSA_WINDOW = 512
LRU_WIDTH = D_MODEL // 2
LRU_BLOCKS = 8
LRU_CONV = 4
LRU_C = 8.0
SWA_WINDOW = 128
S5_WIDTH = D_MODEL // 2
S5_GROUP_CH = 16
S5_GROUPS = S5_WIDTH // S5_GROUP_CH
S5_STATE = 64
D_FF = (8 * D_MODEL // 3) // 128 * 128
FFN_CONV = 3
N_BUCKETS = 32
MAX_DISTANCE = 128
DN_ALPHA = (2 * DEPTH) ** 0.25
LN_EPS = 1e-5
SCALE = HEAD_DIM ** -0.5
NEG = -1e30
FORCE_SCORE = 1e4
KV_COLS = 2 * N_KV * HEAD_DIM
F32 = jnp.float32
BF16 = jnp.bfloat16

LANES = 128
TQ = 128
VMEM_LIMIT = 56 * 1024 * 1024


def _cp(sem, vmem=VMEM_LIMIT):
    return pltpu.CompilerParams(dimension_semantics=sem, vmem_limit_bytes=vmem)


def _dot(a, b):
    return jnp.dot(a, b, preferred_element_type=F32)


def _dot_nt(a, b):
    return lax.dot_general(a, b, (((1,), (1,)), ((), ())), preferred_element_type=F32)


def _gelu(x):
    cdf = 0.5 * (1.0 + jnp.tanh(math.sqrt(2.0 / math.pi) * (x + 0.044715 * (x * x * x))))
    return x * cdf


def _softplus(x):
    return jnp.maximum(x, 0.0) + jnp.log1p(jnp.exp(-jnp.abs(x)))


def _expm1(x):
    u = jnp.exp(x)
    um1 = u - 1.0
    edge = (u == 1.0) | (um1 == -1.0)
    r = um1 * x / jnp.log(jnp.where(edge, 2.0, u))
    return jnp.where(u == 1.0, x, jnp.where(um1 == -1.0, -1.0, r))


def _t5_bucket(dist):
    n = jnp.maximum(dist, 0)
    max_exact = N_BUCKETS // 2
    nf = jnp.maximum(n, 1).astype(F32)
    steps = jnp.log(nf / max_exact) / math.log(MAX_DISTANCE / max_exact) * (N_BUCKETS - max_exact)
    large = max_exact + jnp.floor(steps).astype(jnp.int32)
    return jnp.where(n < max_exact, n, jnp.minimum(large, N_BUCKETS - 1))


def _rel_bias_at(rel_bias, dist):
    onehot = jax.nn.one_hot(_t5_bucket(dist), N_BUCKETS, dtype=F32)
    return jnp.matmul(onehot, rel_bias.astype(F32), precision=lax.Precision.HIGHEST)


def _bucket_saturates_from(n0, n1):
    n = np.arange(n0, n1, dtype=np.float64)
    b = 16 + np.floor(np.log(n / 16.0) / math.log(8.0) * 16.0)
    return bool(np.all(b >= N_BUCKETS - 1 + 0.5))


assert _bucket_saturates_from(TQ + 1, PAST_LEN + 2 * TQ)


def _mod_body(c_ref, w_ref, b_ref, o_ref):
    c = c_ref[...]
    h = (c * jax.nn.sigmoid(c)).astype(BF16)
    o_ref[0] = _dot(h, w_ref[0].astype(BF16)) + b_ref[0]


def _ada_mod_all(c_all, w_mod, b_mod):
    n_sub = w_mod.shape[0]
    rows = c_all.shape[0]
    tn = 1024
    return pl.pallas_call(
        _mod_body,
        grid=(n_sub, 3 * D_MODEL // tn),
        in_specs=[pl.BlockSpec((rows, D_MODEL), lambda s, n: (0, 0)),
                  pl.BlockSpec((1, D_MODEL, tn), lambda s, n: (s, 0, n)),
                  pl.BlockSpec((1, 1, tn), lambda s, n: (s, 0, n))],
        out_specs=pl.BlockSpec((1, rows, tn), lambda s, n: (s, 0, n)),
        out_shape=jax.ShapeDtypeStruct((n_sub, rows, 3 * D_MODEL), F32),
        compiler_params=_cp(("arbitrary", "arbitrary")),
        name="ada_mod",
    )(c_all, w_mod, b_mod)


def _modmm_body(x_ref, sc_ref, sh_ref, *refs, w_of_out, kinds):
    n_w = max(w_of_out) + 1
    h = (x_ref[0] * (1.0 + sc_ref[0]) + sh_ref[0]).astype(BF16)
    res = {}
    for o_ref, wi, kind in zip(refs[n_w:], w_of_out, kinds):
        if wi not in res:
            res[wi] = _dot(h, refs[wi][...])
        r = res[wi]
        if kind == 'tm':
            o_ref[...] = r
        elif kind == 'bm':
            o_ref[0] = r
        else:
            for p in range(r.shape[0] // PAGE_SIZE):
                o_ref[0, p] = r[p * PAGE_SIZE:(p + 1) * PAGE_SIZE, :].T


def _modmm(x, sc, sh, outs, tm):
    B, T, D = x.shape
    per_row = sc.shape[1] != 1
    mod_spec = pl.BlockSpec((1, tm if per_row else 1, D), (lambda b, t: (b, t, 0)) if per_row else (lambda b, t: (b, 0, 0)))
    in_specs = [pl.BlockSpec((1, tm, D), lambda b, t: (b, t, 0)), mod_spec, mod_spec]
    out_specs, out_shape, w_of_out, kinds = [], [], [], []
    for wi, (w, ks) in enumerate(outs):
        n = w.shape[1]
        in_specs.append(pl.BlockSpec((D, n), lambda b, t: (0, 0)))
        for kind in ks:
            w_of_out.append(wi)
            kinds.append(kind)
            if kind == 'tm':
                out_specs.append(pl.BlockSpec((tm, n), lambda b, t: (t, b)))
                out_shape.append(jax.ShapeDtypeStruct((T, B * n), F32))
            elif kind == 'bm':
                out_specs.append(pl.BlockSpec((1, tm, n), lambda b, t: (b, t, 0)))
                out_shape.append(jax.ShapeDtypeStruct((B, T, n), F32))
            else:
                out_specs.append(pl.BlockSpec((1, tm // PAGE_SIZE, n, PAGE_SIZE), lambda b, t: (b, t, 0, 0)))
                out_shape.append(jax.ShapeDtypeStruct((B, T // PAGE_SIZE, n, PAGE_SIZE), F32))
    return pl.pallas_call(
        functools.partial(_modmm_body, w_of_out=tuple(w_of_out), kinds=tuple(kinds)),
        grid=(B, T // tm),
        in_specs=in_specs, out_specs=out_specs, out_shape=out_shape,
        compiler_params=_cp(("arbitrary", "arbitrary")),
        name="mod_proj",
    )(x, sc, sh, *[w for w, _ in outs])


def _mm_res_ln_body(*refs, n_a, a_tm, x_tm, o_tm):
    a_refs, w_refs = refs[:n_a], refs[n_a:2 * n_a]
    x_ref, gt_ref, g_ref, b_ref, o_ref = refs[2 * n_a:]
    y = None
    for a_ref, w_ref, tmj in zip(a_refs, w_refs, a_tm):
        a = (a_ref[...] if tmj else a_ref[0]).astype(BF16)
        r = _dot(a, w_ref[...])
        y = r if y is None else y + r
    x = x_ref[...] if x_tm else x_ref[0]
    z = DN_ALPHA * x + (1.0 + gt_ref[0]) * y
    mu = jnp.mean(z, axis=-1, keepdims=True)
    var = jnp.mean(jnp.square(z - mu), axis=-1, keepdims=True)
    out = (z - mu) * lax.rsqrt(var + LN_EPS) * g_ref[...] + b_ref[...]
    if o_tm:
        o_ref[...] = out
    else:
        o_ref[0] = out


def _mm_res_ln(a_list, a_tm, ws, x, x_tm, gate, ln_g, ln_b, o_tm, B, T, tm):
    D = D_MODEL
    per_row = gate.shape[1] != 1

    def spec(k, tmj):
        if tmj:
            return pl.BlockSpec((tm, k), lambda b, t: (t, b))
        return pl.BlockSpec((1, tm, k), lambda b, t: (b, t, 0))

    in_specs = [spec(w.shape[0], tmj) for w, tmj in zip(ws, a_tm)]
    in_specs += [pl.BlockSpec(w.shape, lambda b, t: (0, 0)) for w in ws]
    in_specs += [spec(D, x_tm),
                 pl.BlockSpec((1, tm if per_row else 1, D), (lambda b, t: (b, t, 0)) if per_row else (lambda b, t: (b, 0, 0))),
                 pl.BlockSpec((1, D), lambda b, t: (0, 0)),
                 pl.BlockSpec((1, D), lambda b, t: (0, 0))]
    out_shape = jax.ShapeDtypeStruct((T, B * D) if o_tm else (B, T, D), F32)
    return pl.pallas_call(
        functools.partial(_mm_res_ln_body, n_a=len(ws), a_tm=tuple(a_tm), x_tm=x_tm, o_tm=o_tm),
        grid=(B, T // tm),
        in_specs=in_specs, out_specs=spec(D, o_tm), out_shape=out_shape,
        compiler_params=_cp(("arbitrary", "arbitrary")),
        name="proj_res_ln",
    )(*a_list, *ws, x, gate, ln_g.reshape(1, D), ln_b.reshape(1, D))


POOL_PAGES = 128
POOL_PITCH = 264
BLK_PER_PAGE = PAGE_SIZE // CMP_BLOCK


def _compress_pool_body(ids_ref, pool_ref, w_ref, pe_ref, o_ref, buf_ref, sem, *, P):
    i = pl.program_id(0)
    n = pl.num_programs(0)

    def page_copy(step, slot, p):
        dst = pl.multiple_of((slot * P + p) * POOL_PITCH, 8)
        return pltpu.make_async_copy(pool_ref.at[ids_ref[step * P + p]], buf_ref.at[pl.ds(dst, KV_COLS), :], sem.at[slot])

    def start_pages(step, slot):
        def go(p, c):
            page_copy(step, slot, p).start()
            return c
        lax.fori_loop(0, P, go, 0)

    @pl.when(i == 0)
    def _():
        start_pages(0, 0)

    @pl.when(i + 1 < n)
    def _():
        start_pages(i + 1, (i + 1) % 2)

    slot = i % 2

    def wait_page(p, c):
        page_copy(i, slot, p).wait()
        return c

    lax.fori_loop(0, P, wait_page, 0)
    base = slot * (P * POOL_PITCH)

    def feature_rows(f):
        return buf_ref[pl.ds(base + f, P, stride=POOL_PITCH), :]

    DC = 8
    for s in range(2):
        out = _dot(pe_ref[s].astype(BF16), w_ref[s])[0:1]
        for d0 in range(0, HEAD_DIM, DC):
            lhs = jnp.concatenate(
                [jnp.concatenate([feature_rows((s * N_KV + k) * HEAD_DIM + d) for d in range(d0, d0 + DC)], axis=1)
                 for k in range(N_KV)], axis=0).astype(BF16)
            out = out + _dot(lhs, w_ref[s, d0 * PAGE_SIZE:(d0 + DC) * PAGE_SIZE, :])
        for k in range(N_KV):
            o_ref[s * N_KV + k] = out[k * P:(k + 1) * P]


def _compress_pool(page_ids, pool_t, w2, pe2):
    n = page_ids.shape[0]
    P = math.gcd(n, POOL_PAGES)
    assert P % 8 == 0
    kdim = HEAD_DIM * PAGE_SIZE
    ncol = BLK_PER_PAGE * HEAD_DIM
    grid_spec = pltpu.PrefetchScalarGridSpec(
        num_scalar_prefetch=1,
        grid=(n // P,),
        in_specs=[pl.BlockSpec(memory_space=pl.ANY),
                  pl.BlockSpec((2, kdim, ncol), lambda i, ids: (0, 0, 0), pipeline_mode=pl.Buffered(1)),
                  pl.BlockSpec((2, 8, kdim), lambda i, ids: (0, 0, 0), pipeline_mode=pl.Buffered(1))],
        out_specs=pl.BlockSpec((2 * N_KV, P, ncol), lambda i, ids: (0, i, 0)),
        scratch_shapes=[pltpu.VMEM((2 * P * POOL_PITCH, PAGE_SIZE), F32), pltpu.SemaphoreType.DMA((2,))],
    )
    return pl.pallas_call(
        functools.partial(_compress_pool_body, P=P),
        grid_spec=grid_spec,
        out_shape=jax.ShapeDtypeStruct((2 * N_KV, n, ncol), F32),
        compiler_params=_cp(("arbitrary",)),
        name="nsa_compress_pool",
    )(page_ids, pool_t, w2, pe2)


def _blocks_by_page(kcv, lead):
    kcv = kcv.reshape(2, N_KV, *lead, BLK_PER_PAGE, HEAD_DIM)
    n = len(lead)
    perm = tuple(range(2, 2 + n)) + (2 + n, 0, 1, 3 + n)
    return jnp.transpose(kcv, perm).reshape(*lead, BLK_PER_PAGE, KV_COLS)


def _compress_pool_weights(cmp_w, cmp_pe):
    eye = jnp.eye(BLK_PER_PAGE, dtype=F32)
    w2 = jnp.einsum('scde,bB->sdbcBe', cmp_w, eye).reshape(2, HEAD_DIM * PAGE_SIZE, BLK_PER_PAGE * HEAD_DIM).astype(BF16)
    pe = jnp.transpose(cmp_pe, (0, 2, 1))[:, :, None, :]
    pe = jnp.broadcast_to(pe, (2, HEAD_DIM, BLK_PER_PAGE, CMP_BLOCK)).reshape(2, 1, HEAD_DIM * PAGE_SIZE)
    return w2, jnp.broadcast_to(pe, (2, 8, HEAD_DIM * PAGE_SIZE))


def _padded_queries(q):
    lo = lax.broadcasted_iota(jnp.int32, (TQ, LANES), 1) < HEAD_DIM
    out = []
    for h in range(N_HEADS):
        x = q[:, (h // 2) * LANES:(h // 2 + 1) * LANES]
        k = h // GROUP
        if h % 2 != k:
            x = pltpu.roll(x, HEAD_DIM, axis=1)
        out.append(jnp.where(lo if k == 0 else ~lo, x, 0.0).astype(BF16))
    return out


T_BEYOND, T_DIAG, T_NEAR, T_FAR, T_FAR_EDGE, T_NEAR_EDGE = range(6)
W_LANES = GROUP * TQ


def _col_max(mv, s):
    return jnp.maximum(mv, jnp.max(s.reshape(s.shape[0] // 8, 8, s.shape[1]), axis=0))


def _col_sum(lv, p):
    return lv + jnp.sum(p.reshape(p.shape[0] // 8, 8, p.shape[1]), axis=0)


def _scores(kv_ref, qg, page, pages):
    start = pl.multiple_of(page * TQ, TQ)
    return _dot_nt(kv_ref[0, pl.ds(start, pages * TQ), 0:LANES].astype(BF16), qg)


def _values_t(kvt_ref, k, page, pages):
    vts = [kvt_ref[0, page + p, LANES + k * HEAD_DIM:LANES + (k + 1) * HEAD_DIM, :] for p in range(pages)]
    return (vts[0] if pages == 1 else jnp.concatenate(vts, axis=1)).astype(BF16)


def _softmax_pass(s_ref, kvt_ref, row0, page0, n_steps, pages, m_rows, init):
    rows = pages * TQ

    def body(t, carry):
        out = []
        for k in range(N_KV):
            lv, acc = carry[k]
            start = pl.multiple_of(row0 + t * rows, TQ)
            p = jnp.exp(s_ref[k, pl.ds(start, rows), :] - m_rows[k])
            vt = _values_t(kvt_ref, k, page0 + t * pages, pages)
            out.append((_col_sum(lv, p), acc + _dot(vt, p.astype(BF16))))
        return tuple(out)

    return lax.fori_loop(0, n_steps, body, init)


def _normalise(m_row, lv, acc, extra=None):
    l = jnp.sum(lv, axis=0, keepdims=True)
    if extra is not None:
        l = l + extra
    valid = m_row > 0.5 * NEG
    return jnp.where(valid, acc, 0.0) / jnp.where(valid & (l > 0), l, 1.0)


def _window_branch(qgs, kv_ref, kvt_ref, s_ref, tiles_ref, i, n_win, m_init):
    lo = jnp.maximum(i - n_win, 0)
    count = i + 1 - lo
    start = lo - jnp.where((count % 2 == 1) & (lo >= 1), 1, 0)
    n_steps = (count + 1) // 2

    def tile_of(j):
        d = i - j
        if n_win == 1:
            inside = jnp.where(d == 0, T_DIAG, T_NEAR_EDGE)
        else:
            inside = jnp.where(d == n_win, T_FAR_EDGE, jnp.clip(d, 0, 2) + T_DIAG)
        return jnp.where((d < 0) | (d > n_win), T_BEYOND, inside)

    def pass1(t, mvs):
        j0 = start + 2 * t
        out = []
        for k in range(N_KV):
            s = _scores(kv_ref, qgs[k], j0, 2)
            s = jnp.concatenate([s[0:TQ] + tiles_ref[tile_of(j0), k], s[TQ:2 * TQ] + tiles_ref[tile_of(j0 + 1), k]], axis=0)
            s_ref[k, pl.ds(pl.multiple_of(t * 2 * TQ, 2 * TQ), 2 * TQ), :] = s
            out.append(_col_max(mvs[k], s))
        return tuple(out)

    mvs = lax.fori_loop(0, n_steps, pass1, tuple(m_init))
    m_rows = [jnp.max(mv, axis=0, keepdims=True) for mv in mvs]
    zero = (jnp.zeros((8, W_LANES), F32), jnp.zeros((HEAD_DIM, W_LANES), F32))
    res = _softmax_pass(s_ref, kvt_ref, 0, start, n_steps, 2, m_rows, (zero,) * N_KV)
    return m_rows, res


def _store_heads(o_ref, o_groups):
    for k in range(N_KV):
        for pair in range(GROUP // 2):
            a = o_groups[k][:, (2 * pair) * TQ:(2 * pair + 1) * TQ]
            b = o_groups[k][:, (2 * pair + 1) * TQ:(2 * pair + 2) * TQ]
            col = (k * GROUP + 2 * pair) * HEAD_DIM
            o_ref[0, :, col:col + 2 * HEAD_DIM] = jnp.concatenate([a, b], axis=0).T


def _nsa_prompt_body(q_ref, kvs_ref, vst_ref, kvw_ref, vwt_ref, kc_ref, vct_ref, gates_ref, tiles_ref, o_ref,
                     sel_ref, ssel_ref, swin_ref, *, T):
    i = pl.program_id(1)
    nc = T // CMP_BLOCK
    nsel = nc // 2
    t0 = i * TQ
    qpads = _padded_queries(q_ref[0] * SCALE)
    gt = jax.nn.sigmoid(gates_ref[0]).T
    W = GROUP * TQ
    c_lane = lax.broadcasted_iota(jnp.int32, (1, W), 1) & (TQ - 1)
    tpos = t0 + c_lane
    r_c = lax.broadcasted_iota(jnp.int32, (nc, W), 0)
    blk = 2 * (r_c % nsel) + r_c // nsel
    vis = (blk + 1) * CMP_BLOCK - 1 <= tpos
    j_s = lax.broadcasted_iota(jnp.int32, (nsel, TQ), 0)
    cur = (t0 + lax.broadcasted_iota(jnp.int32, (nsel, TQ), 1)) // SEL_BLOCK
    qgs, o_cmps = [], []
    for k in range(N_KV):
        qg = jnp.concatenate(qpads[GROUP * k:GROUP * (k + 1)], axis=0)
        qgs.append(qg)
        s = jnp.where(vis, _dot_nt(kc_ref[0].astype(BF16), qg), NEG)
        m = jnp.max(s, axis=0, keepdims=True)
        e = jnp.where(vis, jnp.exp(s - m), 0.0)
        den = jnp.sum(e, axis=0, keepdims=True)
        p = e / jnp.where(den > 0, den, 1.0)
        o_cmps.append(_dot(vct_ref[0, k * HEAD_DIM:(k + 1) * HEAD_DIM, :].astype(BF16), p.astype(BF16)))
        ps = p[:, 0:TQ] + p[:, TQ:2 * TQ] + p[:, 2 * TQ:3 * TQ] + p[:, 3 * TQ:4 * TQ]
        imp = ps[0:nsel] + ps[nsel:nc]
        forced = (j_s == 0) | (j_s == cur) | (j_s == cur - 1)
        score = jnp.where(forced, FORCE_SCORE, jnp.where(j_s <= cur, imp, NEG))
        rank = jnp.zeros((nsel, TQ), F32)
        for r in range(nsel):
            row = score[r:r + 1, :]
            ahead = jnp.where(j_s > r, jnp.where(row >= score, 1.0, 0.0), jnp.where(row > score, 1.0, 0.0))
            rank = rank + ahead
        chosen = (rank < N_SEL) & (j_s <= cur)
        sel_ref[k] = jnp.concatenate([jnp.where(chosen, 0.0, NEG)] * GROUP, axis=1)

    n_steps = (i + 2) // 2

    def sel_pass1(t, mvs):
        out = []
        for k in range(N_KV):
            s = _scores(kvs_ref, qgs[k], 2 * t, 2)
            parts = []
            for h in range(4):
                tile = jnp.clip(i - (2 * t + h // 2), -1, 2) + T_DIAG
                half = slice((h % 2) * SEL_BLOCK, (h % 2 + 1) * SEL_BLOCK)
                parts.append(s[h * SEL_BLOCK:(h + 1) * SEL_BLOCK] + sel_ref[k, pl.ds(4 * t + h, 1), :] + tiles_ref[tile, k, half, :])
            s = jnp.concatenate(parts, axis=0)
            ssel_ref[k, pl.ds(pl.multiple_of(t * 2 * TQ, 2 * TQ), 2 * TQ), :] = s
            out.append(_col_max(mvs[k], s))
        return tuple(out)

    m0 = jnp.full((8, W), NEG, F32)
    mvs = lax.fori_loop(0, n_steps, sel_pass1, (m0, m0))
    m_sel = [jnp.max(mv, axis=0, keepdims=True) for mv in mvs]
    zero = (jnp.zeros((8, W), F32), jnp.zeros((HEAD_DIM, W), F32))
    r_sel = _softmax_pass(ssel_ref, vst_ref, 0, 0, n_steps, 2, m_sel, (zero, zero))
    m_win, r_win = _window_branch(qgs, kvw_ref, vwt_ref, swin_ref, tiles_ref, i, NSA_WINDOW // TQ, (m0, m0))

    o_groups = []
    for k in range(N_KV):
        def gate_row(branch):
            return jnp.concatenate([gt[branch * N_HEADS + GROUP * k + hh:branch * N_HEADS + GROUP * k + hh + 1, :]
                                    for hh in range(GROUP)], axis=1)

        o_slc = _normalise(m_sel[k], *r_sel[k])
        o_win = _normalise(m_win[k], *r_win[k])
        o_groups.append(gate_row(0) * o_cmps[k] + gate_row(1) * o_slc + gate_row(2) * o_win)
    _store_heads(o_ref, o_groups)


def _bias_tiles(rel_bias):
    r = jnp.arange(TQ, dtype=jnp.int32)[:, None]
    c = jnp.arange(TQ, dtype=jnp.int32)[None, :]
    near = jnp.stack([_rel_bias_at(rel_bias, d * TQ + c - r) for d in range(2)])
    near = jnp.transpose(near.reshape(2, TQ, TQ, N_KV, GROUP), (0, 3, 1, 4, 2)).reshape(2, N_KV, TQ, GROUP * TQ)
    far = jnp.repeat(rel_bias[N_BUCKETS - 1].reshape(N_KV, 1, GROUP), TQ, axis=2).astype(F32)
    causal = jnp.tile(jnp.where(c >= r, 0.0, NEG), (1, GROUP))
    inside = jnp.tile(jnp.where(c < r, 0.0, NEG), (1, GROUP))
    full = jnp.zeros((N_KV, TQ, GROUP * TQ), F32)
    tiles = jnp.stack([full + NEG, near[0] + causal, near[1] + full, far + full, far + inside, near[1] + inside])
    return tiles.astype(F32)


def _nsa_prompt_attn(q, kvs, kvs_pt, kvw, kvw_pt, kc, vct, gates, btiles):
    B, T, _ = q.shape
    nc = T // CMP_BLOCK
    W = GROUP * TQ
    n_tiles = btiles.shape[0]
    assert T // TQ >= NSA_WINDOW // TQ + 2
    seq = pl.BlockSpec((1, T, KV_COLS), lambda b, i: (b, 0, 0))
    seq_pt = pl.BlockSpec((1, T // PAGE_SIZE, KV_COLS, PAGE_SIZE), lambda b, i: (b, 0, 0, 0))
    return pl.pallas_call(
        functools.partial(_nsa_prompt_body, T=T),
        grid=(B, T // TQ),
        in_specs=[pl.BlockSpec((1, TQ, N_HEADS * HEAD_DIM), lambda b, i: (b, i, 0)),
                  seq, seq_pt, seq, seq_pt,
                  pl.BlockSpec((1, nc, LANES), lambda b, i: (b, 0, 0)),
                  pl.BlockSpec((1, LANES, nc), lambda b, i: (b, 0, 0)),
                  pl.BlockSpec((1, TQ, LANES), lambda b, i: (b, i, 0)),
                  pl.BlockSpec((n_tiles, N_KV, TQ, W), lambda b, i: (0, 0, 0, 0))],
        out_specs=pl.BlockSpec((1, TQ, N_HEADS * HEAD_DIM), lambda b, i: (b, i, 0)),
        out_shape=jax.ShapeDtypeStruct((B, T, N_HEADS * HEAD_DIM), F32),
        scratch_shapes=[pltpu.VMEM((N_KV, nc // 2, W), F32), pltpu.VMEM((N_KV, T, W), F32),
                        pltpu.VMEM((N_KV, NSA_WINDOW + 2 * TQ, W), F32)],
        compiler_params=_cp(("arbitrary", "arbitrary")),
        name="nsa_prompt_attn",
    )(q, kvs, kvs_pt, kvw, kvw_pt, kc, vct, gates, btiles)


def _swa_prompt_body(q_ref, kv_ref, vt_ref, sink_ref, tiles_ref, o_ref, s_ref):
    i = pl.program_id(1)
    qpads = _padded_queries(q_ref[0] * SCALE)
    qgs = [jnp.concatenate(qpads[GROUP * k:GROUP * (k + 1)], axis=0) for k in range(N_KV)]
    m_init = [jnp.broadcast_to(sink_ref[k], (8, W_LANES)) for k in range(N_KV)]
    m_rows, res = _window_branch(qgs, kv_ref, vt_ref, s_ref, tiles_ref, i, SWA_WINDOW // TQ, m_init)
    _store_heads(o_ref, [_normalise(m_rows[k], *res[k], extra=jnp.exp(sink_ref[k] - m_rows[k])) for k in range(N_KV)])


def _swa_prompt_attn(q, kv, kv_pt, sink_rows, btiles):
    B, T, _ = q.shape
    W = GROUP * TQ
    n_tiles = btiles.shape[0]
    return pl.pallas_call(
        _swa_prompt_body,
        grid=(B, T // TQ),
        in_specs=[pl.BlockSpec((1, TQ, N_HEADS * HEAD_DIM), lambda b, i: (b, i, 0)),
                  pl.BlockSpec((1, T, KV_COLS), lambda b, i: (b, 0, 0)),
                  pl.BlockSpec((1, T // PAGE_SIZE, KV_COLS, PAGE_SIZE), lambda b, i: (b, 0, 0, 0)),
                  pl.BlockSpec((N_KV, 1, W), lambda b, i: (0, 0, 0)),
                  pl.BlockSpec((n_tiles, N_KV, TQ, W), lambda b, i: (0, 0, 0, 0))],
        out_specs=pl.BlockSpec((1, TQ, N_HEADS * HEAD_DIM), lambda b, i: (b, i, 0)),
        out_shape=jax.ShapeDtypeStruct((B, T, N_HEADS * HEAD_DIM), F32),
        scratch_shapes=[pltpu.VMEM((N_KV, SWA_WINDOW + TQ, W), F32)],
        compiler_params=_cp(("arbitrary", "arbitrary")),
        name="swa_prompt_attn",
    )(q, kv, kv_pt, sink_rows, btiles)


def _rglru_body(xb_ref, gb_ref, prev0_ref, h0_ref, cw_ref, cb_ref, wr_ref, br_ref, wi_ref, bi_ref, lam_ref,
                y_ref, buf_ref, hl_ref, prev_ref, h_ref, a_ref, u_ref, *, tt):
    i = pl.program_id(0)

    @pl.when(i == 0)
    def _():
        prev_ref[...] = prev0_ref[...]
        h_ref[...] = h0_ref[...]

    x = xb_ref[...]
    B, W = x.shape[1], x.shape[2]
    xp = jnp.concatenate([prev_ref[...], x], axis=0)
    xc = xp[0:tt] * cw_ref[0:1, :]
    for j in range(1, LRU_CONV):
        xc = xc + xp[j:j + tt] * cw_ref[j:j + 1, :]
    xc = xc + cb_ref[...]
    prev_ref[...] = xp[tt:tt + LRU_CONV - 1]
    xf = xc.reshape(tt * B, W)
    xh = xf.astype(BF16)
    r = jax.nn.sigmoid(_dot(xh, wr_ref[...]) + br_ref[...])
    ig = jax.nn.sigmoid(_dot(xh, wi_ref[...]) + bi_ref[...])
    log_a = -LRU_C * r * _softplus(-lam_ref[...])
    a_ref[...] = jnp.exp(log_a).reshape(tt, B, W)
    u_ref[...] = (jnp.sqrt(-_expm1(2.0 * log_a)) * (ig * xf)).reshape(tt, B, W)

    def step(t, h):
        h = a_ref[t] * h + u_ref[t]
        y_ref[t] = h
        return h

    h = lax.fori_loop(0, tt, step, h_ref[...])
    h_ref[...] = h
    y_ref[...] = y_ref[...] * _gelu(gb_ref[...])
    buf_ref[...] = prev_ref[...]
    hl_ref[...] = h


def _rglru(xb, gb, prev0, h0, prm, tt):
    T, B, W = xb.shape
    full2 = lambda shp: pl.BlockSpec(shp, lambda i: (0, 0))
    blk = pl.BlockSpec((tt, B, W), lambda i: (i, 0, 0))
    return pl.pallas_call(
        functools.partial(_rglru_body, tt=tt),
        grid=(T // tt,),
        in_specs=[blk, blk, pl.BlockSpec((LRU_CONV - 1, B, W), lambda i: (0, 0, 0)), full2((B, W)),
                  full2((LRU_CONV, W)), full2((1, W)), full2((W, W)), full2((1, W)), full2((W, W)), full2((1, W)),
                  full2((1, W))],
        out_specs=[blk, pl.BlockSpec((LRU_CONV - 1, B, W), lambda i: (0, 0, 0)), full2((B, W))],
        out_shape=[jax.ShapeDtypeStruct((T, B, W), F32), jax.ShapeDtypeStruct((LRU_CONV - 1, B, W), F32),
                   jax.ShapeDtypeStruct((B, W), F32)],
        scratch_shapes=[pltpu.VMEM((LRU_CONV - 1, B, W), F32), pltpu.VMEM((B, W), F32),
                        pltpu.VMEM((tt, B, W), F32), pltpu.VMEM((tt, B, W), F32)],
        compiler_params=_cp(("arbitrary",)),
        name="rglru",
    )(xb, gb, prev0, h0, prm['conv_w'], prm['conv_b'], prm['w_r'], prm['b_r'], prm['w_i'], prm['b_i'], prm['lam'])


def _block_diag(w):
    n, d, e = w.shape
    return jnp.einsum('nde,nm->ndme', w, jnp.eye(n, dtype=w.dtype)).reshape(n * d, n * e)


S5_CHUNKS = 4
S5_CH_IN = S5_WIDTH // S5_CHUNKS
S5_CH_ST = S5_GROUPS * S5_STATE // S5_CHUNKS
S5_NSTATE = S5_GROUPS * S5_STATE


def _s5_disc(lr, li, dt):
    mag = jnp.exp(lr * dt)
    ab_re, ab_im = mag * jnp.cos(li * dt), mag * jnp.sin(li * dt)
    den = lr * lr + li * li
    f_re = ((ab_re - 1.0) * lr + ab_im * li) / den
    f_im = (ab_im * lr - (ab_re - 1.0) * li) / den
    return ab_re, ab_im, f_re, f_im


def _s5_param_body(lr_ref, li_ref, ldt_ref, lre_ref, lie_ref, bre_ref, bim_ref, are_ref, aim_ref, bbre_ref, bbim_ref):
    dt = jnp.exp(ldt_ref[...])
    ab_re, ab_im, _, _ = _s5_disc(lr_ref[...], li_ref[...], dt)
    are_ref[...] = ab_re
    aim_ref[...] = ab_im
    _, _, f_re, f_im = _s5_disc(lre_ref[...], lie_ref[...], dt)
    br, bim = bre_ref[...], bim_ref[...]
    bbre_ref[...] = f_re * br - f_im * bim
    bbim_ref[...] = f_re * bim + f_im * br


def _s5_params(lam_re, lam_im, log_dt, b_re, b_im):
    G, P, C = b_re.shape
    rep = lambda a: jnp.repeat(a, C, axis=1)
    shapes = [jax.ShapeDtypeStruct((G, P), F32)] * 2 + [jax.ShapeDtypeStruct((G, P * C), F32)] * 2
    return pl.pallas_call(_s5_param_body, out_shape=shapes, name="s5_discretise")(
        lam_re, lam_im, log_dt.reshape(G, 1), rep(lam_re), rep(lam_im), b_re.reshape(G, P * C), b_im.reshape(G, P * C))


def _s5_body(u_ref, h0r_ref, h0i_ref, are_ref, aim_ref, bre_ref, bim_ref, cre_ref, cim_ref, d_ref, wg_ref, bg_ref,
             y_ref, sr_ref, si_ref, hr_ref, hi_ref, xr_ref, xi_ref, sr_st_ref, si_st_ref, *, tt):
    i = pl.program_id(0)

    @pl.when(i == 0)
    def _():
        hr_ref[...] = h0r_ref[...]
        hi_ref[...] = h0i_ref[...]

    u3 = u_ref[...]
    B, W = u3.shape[1], u3.shape[2]
    uf = u3.reshape(tt * B, W)
    y_parts = []
    for ck in range(S5_CHUNKS):
        uc = uf[:, ck * S5_CH_IN:(ck + 1) * S5_CH_IN].astype(BF16)
        xr_ref[...] = _dot(uc, bre_ref[ck]).reshape(tt, B, S5_CH_ST)
        xi_ref[...] = _dot(uc, bim_ref[ck]).reshape(tt, B, S5_CH_ST)
        lanes = slice(ck * S5_CH_ST, (ck + 1) * S5_CH_ST)
        a_re = jnp.broadcast_to(are_ref[:, lanes], (B, S5_CH_ST))
        a_im = jnp.broadcast_to(aim_ref[:, lanes], (B, S5_CH_ST))

        def step(t, h):
            h_re, h_im = h
            n_re = a_re * h_re - a_im * h_im + xr_ref[t]
            n_im = a_re * h_im + a_im * h_re + xi_ref[t]
            sr_st_ref[t] = n_re
            si_st_ref[t] = n_im
            return n_re, n_im

        h_re, h_im = lax.fori_loop(0, tt, step, (hr_ref[:, lanes], hi_ref[:, lanes]))
        hr_ref[:, lanes] = h_re
        hi_ref[:, lanes] = h_im
        hre = sr_st_ref[...].reshape(tt * B, S5_CH_ST).astype(BF16)
        him = si_st_ref[...].reshape(tt * B, S5_CH_ST).astype(BF16)
        y_parts.append(_dot(hre, cre_ref[ck]) - _dot(him, cim_ref[ck]))
    y = jnp.concatenate(y_parts, axis=1) + d_ref[...] * uf
    g = _gelu(y)
    out = g * jax.nn.sigmoid(_dot(g.astype(BF16), wg_ref[...]) + bg_ref[...])
    y_ref[...] = out.reshape(tt, B, W)
    sr_ref[...] = hr_ref[...]
    si_ref[...] = hi_ref[...]


def _s5(u, h0_re, h0_im, prm, tt):
    T, B, W = u.shape
    full2 = lambda shp: pl.BlockSpec(shp, lambda i: (0, 0))
    full3 = lambda shp: pl.BlockSpec(shp, lambda i: (0, 0, 0))
    blk = pl.BlockSpec((tt, B, W), lambda i: (i, 0, 0))
    st = jax.ShapeDtypeStruct((B, S5_NSTATE), F32)
    return pl.pallas_call(
        functools.partial(_s5_body, tt=tt),
        grid=(T // tt,),
        in_specs=[blk, full2((B, S5_NSTATE)), full2((B, S5_NSTATE)), full2((1, S5_NSTATE)), full2((1, S5_NSTATE)),
                  full3((S5_CHUNKS, S5_CH_IN, S5_CH_ST)), full3((S5_CHUNKS, S5_CH_IN, S5_CH_ST)),
                  full3((S5_CHUNKS, S5_CH_ST, S5_CH_IN)), full3((S5_CHUNKS, S5_CH_ST, S5_CH_IN)),
                  full2((1, W)), full2((W, W)), full2((1, W))],
        out_specs=[blk, full2((B, S5_NSTATE)), full2((B, S5_NSTATE))],
        out_shape=[jax.ShapeDtypeStruct((T, B, W), F32), st, st],
        scratch_shapes=[pltpu.VMEM((B, S5_NSTATE), F32), pltpu.VMEM((B, S5_NSTATE), F32),
                        pltpu.VMEM((tt, B, S5_CH_ST), F32), pltpu.VMEM((tt, B, S5_CH_ST), F32),
                        pltpu.VMEM((tt, B, S5_CH_ST), F32), pltpu.VMEM((tt, B, S5_CH_ST), F32)],
        compiler_params=_cp(("arbitrary",)),
        name="s5",
    )(u, h0_re, h0_im, prm['a_re'], prm['a_im'], prm['b_re'], prm['b_im'], prm['c_re'], prm['c_im'],
      prm['d'], prm['w_glu'], prm['b_glu'])


def _s5_prepare(lam_re, lam_im, log_dt, b_re, b_im, c_re, c_im, d, w_glu, b_glu):
    a_re, a_im, bb_re, bb_im = _s5_params(lam_re, lam_im, log_dt, b_re, b_im)
    gpc = S5_GROUPS // S5_CHUNKS
    eye = jnp.eye(gpc, dtype=F32)

    def in_mat(bb):
        bb = bb.reshape(S5_CHUNKS, gpc, S5_STATE, S5_GROUP_CH)
        return jnp.einsum('kgpc,gh->kgchp', bb, eye).reshape(S5_CHUNKS, S5_CH_IN, S5_CH_ST).astype(BF16)

    def out_mat(c):
        c = c.reshape(S5_CHUNKS, gpc, S5_GROUP_CH, S5_STATE)
        return jnp.einsum('kgcp,gh->kgphc', c, eye).reshape(S5_CHUNKS, S5_CH_ST, S5_CH_IN).astype(BF16)

    return {'a_re': a_re.reshape(1, S5_NSTATE), 'a_im': a_im.reshape(1, S5_NSTATE),
            'b_re': in_mat(bb_re), 'b_im': in_mat(bb_im), 'c_re': out_mat(c_re), 'c_im': out_mat(c_im),
            'd': d.reshape(1, S5_WIDTH), 'w_glu': w_glu.astype(BF16), 'b_glu': b_glu.reshape(1, S5_WIDTH)}


FF_CHUNK = 896


def _ffn_up_body(x_ref, sc_ref, sh_ref, w_ref, cw_ref, cb_ref, prev0_ref, a_ref, buf_ref, prev_ref, *, tt):
    i = pl.program_id(0)

    @pl.when(i == 0)
    def _():
        prev_ref[...] = prev0_ref[...]

    x = x_ref[...]
    B, D = x.shape[1], x.shape[2]
    h = (x * (1.0 + sc_ref[...]) + sh_ref[...]).reshape(tt * B, D).astype(BF16)
    for c0 in range(0, D_FF, FF_CHUNK):
        cols = slice(c0, c0 + FF_CHUNK)
        g = _dot(h, w_ref[:, cols]).reshape(tt, B, FF_CHUNK)
        v = _dot(h, w_ref[:, D_FF + c0:D_FF + c0 + FF_CHUNK]).reshape(tt, B, FF_CHUNK)
        gp = jnp.concatenate([prev_ref[:, :, cols], g], axis=0)
        y = gp[0:tt] * cw_ref[0:1, cols]
        for j in range(1, FFN_CONV):
            y = y + gp[j:j + tt] * cw_ref[j:j + 1, cols]
        y = y + cb_ref[:, cols]
        prev_ref[:, :, cols] = gp[tt:tt + FFN_CONV - 1]
        a_ref[:, :, cols] = (_gelu(y) * v).astype(BF16)
    buf_ref[...] = prev_ref[...]


def _ffn_up(x, sc, sh, w_up, conv_w, conv_b, prev0, tt):
    T, B, D = x.shape
    full2 = lambda shp: pl.BlockSpec(shp, lambda i: (0, 0))
    return pl.pallas_call(
        functools.partial(_ffn_up_body, tt=tt),
        grid=(T // tt,),
        in_specs=[pl.BlockSpec((tt, B, D), lambda i: (i, 0, 0)), full2((B, D)), full2((B, D)),
                  full2((D, 2 * D_FF)), full2((FFN_CONV, D_FF)), full2((1, D_FF)),
                  pl.BlockSpec((FFN_CONV - 1, B, D_FF), lambda i: (0, 0, 0))],
        out_specs=[pl.BlockSpec((tt, B, D_FF), lambda i: (i, 0, 0)),
                   pl.BlockSpec((FFN_CONV - 1, B, D_FF), lambda i: (0, 0, 0))],
        out_shape=[jax.ShapeDtypeStruct((T, B, D_FF), BF16), jax.ShapeDtypeStruct((FFN_CONV - 1, B, D_FF), F32)],
        scratch_shapes=[pltpu.VMEM((FFN_CONV - 1, B, D_FF), F32)],
        compiler_params=_cp(("arbitrary",)),
        name="ffn_up_conv",
    )(x, sc, sh, w_up, conv_w, conv_b.reshape(1, D_FF), prev0)


N_PAST_PAGES = PAST_LEN // PAGE_SIZE
N_CAND = 3 * N_PAST_PAGES
N_PAST_CMP = PAST_LEN // CMP_BLOCK
N_CMP_ALL = N_PAST_CMP + 2
N_PAST_SEL = PAST_LEN // SEL_BLOCK


def _softmax_rows(s, mask):
    if mask is not None:
        s = jnp.where(mask, s, NEG)
    m = jnp.max(s, axis=1, keepdims=True)
    e = jnp.exp(s - m)
    if mask is not None:
        e = jnp.where(mask, e, 0.0)
    den = jnp.sum(e, axis=1, keepdims=True)
    return e / jnp.where(den > 0, den, 1.0)


def _cmp_sample_body(q_ref, kcv_ref, tail_ref):
    q = (q_ref[0] * SCALE).astype(BF16)
    n_pg = kcv_ref.shape[2]
    parts = []
    for blk in range(BLK_PER_PAGE):
        cols = slice(blk * HEAD_DIM, (blk + 1) * HEAD_DIM)
        parts.append(_by_group(_dot_nt(q, kcv_ref[0, 0, :, cols].astype(BF16)), _dot_nt(q, kcv_ref[1, 0, :, cols].astype(BF16))))
    parts.append(_by_group(_dot_nt(q, tail_ref[0, :, 0:HEAD_DIM].astype(BF16)),
                           _dot_nt(q, tail_ref[0, :, HEAD_DIM:2 * HEAD_DIM].astype(BF16))))
    s = jnp.concatenate(parts, axis=1)
    lane = lax.broadcasted_iota(jnp.int32, s.shape, 1)
    idx = jnp.where(lane < BLK_PER_PAGE * n_pg, BLK_PER_PAGE * (lane % n_pg) + lane // n_pg, lane)
    vis = (idx < N_CMP_ALL) & ((idx + 1) * CMP_BLOCK - 1 <= PAST_LEN)
    p = _softmax_rows(s, vis)
    pb = p.astype(BF16)
    o = jnp.zeros((N_HEADS, HEAD_DIM), F32)
    for blk in range(BLK_PER_PAGE):
        cols = slice(blk * HEAD_DIM, (blk + 1) * HEAD_DIM)
        pp = pb[:, blk * n_pg:(blk + 1) * n_pg]
        o = o + _by_group(_dot(pp, kcv_ref[2, 0, :, cols].astype(BF16)), _dot(pp, kcv_ref[3, 0, :, cols].astype(BF16)))
    pt = pb[:, BLK_PER_PAGE * n_pg:]
    o = o + _by_group(_dot(pt, tail_ref[0, :, 2 * HEAD_DIM:3 * HEAD_DIM].astype(BF16)),
                      _dot(pt, tail_ref[0, :, 3 * HEAD_DIM:4 * HEAD_DIM].astype(BF16)))
    row = lax.broadcasted_iota(jnp.int32, s.shape, 0)
    g0 = jnp.sum(jnp.where(row < GROUP, p, 0.0), axis=0, keepdims=True)
    g1 = jnp.sum(jnp.where(row >= GROUP, p, 0.0), axis=0, keepdims=True)
    gs = jnp.concatenate([g0, g1], axis=0)
    gt = jnp.sum(gs[:, 4 * n_pg:], axis=1, keepdims=True)
    first = lax.broadcasted_iota(jnp.int32, (N_KV, n_pg), 1) == 0
    imp = jnp.concatenate([gs[:, 0:n_pg] + gs[:, n_pg:2 * n_pg], gs[:, 2 * n_pg:3 * n_pg] + gs[:, 3 * n_pg:4 * n_pg],
                           jnp.where(first, gt, 0.0)], axis=1)
    return o, imp


SAMPLE_REQS = 4


def _cmp_sample_batch(q_ref, kcv_ref, tail_ref, o_ref, imp_ref, *, R):
    res = [_cmp_sample_body(q_ref.at[pl.ds(r, 1)], kcv_ref.at[:, pl.ds(r, 1)], tail_ref.at[pl.ds(r, 1)]) for r in range(R)]
    for r in range(R):
        o_ref[r], imp_ref[r] = res[r]


def _cmp_sample(q, kcv_pages, tail):
    B = q.shape[0]
    R = math.gcd(B, SAMPLE_REQS)
    n_pg = kcv_pages.shape[2]
    return pl.pallas_call(
        functools.partial(_cmp_sample_batch, R=R),
        grid=(B // R,),
        in_specs=[pl.BlockSpec((R, N_HEADS, HEAD_DIM), lambda b: (b, 0, 0)),
                  pl.BlockSpec((2 * N_KV, R, n_pg, BLK_PER_PAGE * HEAD_DIM), lambda b: (0, b, 0, 0)),
                  pl.BlockSpec((R, n_pg, KV_COLS), lambda b: (b, 0, 0))],
        out_specs=[pl.BlockSpec((R, N_HEADS, HEAD_DIM), lambda b: (b, 0, 0)),
                   pl.BlockSpec((R, N_KV, 3 * n_pg), lambda b: (b, 0, 0))],
        out_shape=[jax.ShapeDtypeStruct((B, N_HEADS, HEAD_DIM), F32), jax.ShapeDtypeStruct((B, N_KV, 3 * n_pg), F32)],
        compiler_params=_cp(("arbitrary",)),
        name="nsa_sample_cmp",
    )(q, kcv_pages, tail)


def _topk_sample_body(imp_ref, idx_ref):
    imp = imp_ref[...]
    r = lax.broadcasted_iota(jnp.int32, imp.shape, 0)
    n_pg = N_PAST_PAGES
    big = 4 * N_PAST_SEL
    j = jnp.where(r < n_pg, 2 * r, jnp.where(r < 2 * n_pg, 2 * (r - n_pg) + 1, jnp.where(r == 2 * n_pg, N_PAST_SEL, big)))
    cur = N_PAST_SEL
    forced = (j == 0) | (j == cur) | (j == cur - 1)
    score = jnp.where(forced, FORCE_SCORE, jnp.where(j <= cur, imp, -jnp.inf))
    rows = []
    for _ in range(N_SEL):
        top = jnp.max(score, axis=0, keepdims=True)
        idx = jnp.min(jnp.where(score == top, j, big), axis=0, keepdims=True)
        rows.append(idx)
        score = jnp.where(j == idx, -jnp.inf, score)
    idx_ref[...] = jnp.concatenate(rows, axis=0)


def _topk_sample(imp_t):
    return pl.pallas_call(
        _topk_sample_body,
        out_shape=jax.ShapeDtypeStruct((N_SEL, imp_t.shape[1]), jnp.int32),
        name="nsa_sample_topk",
    )(imp_t)


def _shift_in(win_ref, new_ref, width):
    lane = lax.broadcasted_iota(jnp.int32, (KV_COLS, width), 1)
    return jnp.where(lane == width - 1, new_ref[0], pltpu.roll(win_ref[0], width - 1, axis=1))


def _by_group(a0, a1):
    row = lax.broadcasted_iota(jnp.int32, a0.shape, 0)
    return jnp.where(row < GROUP, a0, a1)


def _window_scores(q, buf):
    return _by_group(_dot(q, buf[0:HEAD_DIM].astype(BF16)), _dot(q, buf[HEAD_DIM:2 * HEAD_DIM].astype(BF16)))


def _window_values(p, buf):
    pb = p.astype(BF16)
    return _by_group(_dot_nt(pb, buf[2 * HEAD_DIM:3 * HEAD_DIM].astype(BF16)),
                     _dot_nt(pb, buf[3 * HEAD_DIM:4 * HEAD_DIM].astype(BF16)))


def _selwin_sample_body(idx_ref, page_ref, pool_ref, q_ref, newrow_ref, win_ref, newwin_ref, bsel_ref, bwin_ref, b0_ref,
                        gate_ref, ocmp_ref, o_ref, winout_ref, g_ref, bias_ref, sem, *, R):
    step = pl.program_id(0)
    nsteps = pl.num_programs(0)
    per_req = N_KV * N_SEL
    nl = N_SEL * PAGE_SIZE

    def page_copy(st, slot, r, k, n):
        page = page_ref[(st * R + r) * per_req + k * N_SEL + n]
        return pltpu.make_async_copy(pool_ref.at[page, :, k],
                                     g_ref.at[slot, r, k, :, :, pl.ds(n * PAGE_SIZE, PAGE_SIZE)],
                                     sem.at[slot, (r * N_KV + k) * N_SEL + n])

    def start_all(st, slot):
        for r in range(R):
            for k in range(N_KV):
                for n in range(N_SEL):
                    page_copy(st, slot, r, k, n).start()

    @pl.when(step == 0)
    def _():
        start_all(0, 0)

    @pl.when(step + 1 < nsteps)
    def _():
        start_all(step + 1, (step + 1) % 2)

    slot = step % 2
    for r in range(R):
        for k in range(N_KV):
            for n in range(N_SEL):
                page_copy(step, slot, r, k, n).wait()

    lane = lax.broadcasted_iota(jnp.int32, (1, nl), 1)
    row = lax.broadcasted_iota(jnp.int32, (N_HEADS, nl), 0)
    row1 = lax.broadcasted_iota(jnp.int32, (N_HEADS, 1), 0)
    results = []
    for r in range(R):
        q = (q_ref[r] * SCALE).astype(BF16)
        scores, masks, vts, has_new = [], [], [], []
        for k in range(N_KV):
            nvec = jnp.zeros((1, nl), jnp.int32)
            seen = jnp.int32(0)
            for n in range(N_SEL):
                blk = idx_ref[(step * R + r) * per_req + k * N_SEL + n]
                nvec = jnp.where(lane // PAGE_SIZE == n, blk, nvec)
                seen = jnp.maximum(seen, jnp.where(blk == N_PAST_SEL, 1, 0))
                bias_ref[r, k, :, n * PAGE_SIZE:(n + 1) * PAGE_SIZE] = bsel_ref[blk]
            has_new.append(seen)
            half = jnp.minimum(nvec, N_PAST_SEL - 1) % (PAGE_SIZE // SEL_BLOCK)
            in_half = (lane % PAGE_SIZE) // SEL_BLOCK == half
            masks.append(jnp.where(in_half & (nvec < N_PAST_SEL), 1.0, 0.0))
            scores.append(_dot(q, g_ref[slot, r, k, 0].astype(BF16)) + bias_ref[r, k])
            vts.append(g_ref[slot, r, k, 1].astype(BF16))
        mask = jnp.where(row < GROUP, masks[0], masks[1]) > 0.5
        s = jnp.where(mask, jnp.where(row < GROUP, scores[0], scores[1]), NEG)
        new = newrow_ref[r].astype(BF16).astype(F32)
        own = lambda a, b: jnp.where(row1 < GROUP, jnp.broadcast_to(a, (N_HEADS, HEAD_DIM)), jnp.broadcast_to(b, (N_HEADS, HEAD_DIM)))
        k_new = own(new[:, 0:HEAD_DIM], new[:, HEAD_DIM:2 * HEAD_DIM])
        v_new = own(new[:, 2 * HEAD_DIM:3 * HEAD_DIM], new[:, 3 * HEAD_DIM:4 * HEAD_DIM])
        valid_new = jnp.where(row1 < GROUP, has_new[0], has_new[1]) > 0
        s_new = jnp.sum(q.astype(F32) * k_new, axis=1, keepdims=True) + b0_ref[...]
        s_new = jnp.where(valid_new, s_new, NEG)
        m = jnp.maximum(jnp.max(s, axis=1, keepdims=True), s_new)
        e = jnp.where(mask, jnp.exp(s - m), 0.0)
        e_new = jnp.where(valid_new, jnp.exp(s_new - m), 0.0)
        den = jnp.sum(e, axis=1, keepdims=True) + e_new
        den = jnp.where(den > 0, den, 1.0)
        pb = (e / den).astype(BF16)
        p_new = (e_new / den).astype(BF16).astype(F32)
        o_slc = _by_group(_dot_nt(pb, vts[0]), _dot_nt(pb, vts[1])) + p_new * v_new
        win = _shift_in(win_ref.at[pl.ds(r, 1)], newwin_ref.at[pl.ds(r, 1)], NSA_WINDOW)
        pw = _softmax_rows(_window_scores(q, win) + bwin_ref[...], None)
        o_win = _window_values(pw, win)
        g = jax.nn.sigmoid(gate_ref[r])
        results.append((win, g[:, 0:1] * ocmp_ref[r] + g[:, 1:2] * o_slc + g[:, 2:3] * o_win))
    for r in range(R):
        winout_ref[r], o_ref[r] = results[r]


def _selwin_sample(idx_flat, page_flat, pool5, q, new_row, win, new_win, bias_sel, bias_win, bias0, gates, o_cmp):
    B = q.shape[0]
    R = math.gcd(B, SAMPLE_REQS)
    nl = N_SEL * PAGE_SIZE
    spec3 = lambda shp: pl.BlockSpec(shp, lambda b, *_: (b, 0, 0))
    grid_spec = pltpu.PrefetchScalarGridSpec(
        num_scalar_prefetch=2,
        grid=(B // R,),
        in_specs=[pl.BlockSpec(memory_space=pl.ANY),
                  spec3((R, N_HEADS, HEAD_DIM)), spec3((R, 1, KV_COLS)), spec3((R, KV_COLS, NSA_WINDOW)),
                  spec3((R, KV_COLS, 1)),
                  pl.BlockSpec((N_PAST_SEL + 1, N_HEADS, PAGE_SIZE), lambda b, *_: (0, 0, 0)),
                  pl.BlockSpec((N_HEADS, NSA_WINDOW), lambda b, *_: (0, 0)),
                  pl.BlockSpec((N_HEADS, 1), lambda b, *_: (0, 0)),
                  spec3((R, N_HEADS, 3)), spec3((R, N_HEADS, HEAD_DIM))],
        out_specs=[spec3((R, N_HEADS, HEAD_DIM)), spec3((R, KV_COLS, NSA_WINDOW))],
        scratch_shapes=[pltpu.VMEM((2, R, N_KV, 2, HEAD_DIM, nl), F32), pltpu.VMEM((R, N_KV, N_HEADS, nl), F32),
                        pltpu.SemaphoreType.DMA((2, R * N_KV * N_SEL))],
    )
    return pl.pallas_call(
        functools.partial(_selwin_sample_body, R=R),
        grid_spec=grid_spec,
        out_shape=[jax.ShapeDtypeStruct((B, N_HEADS, HEAD_DIM), F32), jax.ShapeDtypeStruct((B, KV_COLS, NSA_WINDOW), F32)],
        compiler_params=_cp(("arbitrary",)),
        name="nsa_sample_sel_win",
    )(idx_flat, page_flat, pool5, q, new_row, win, new_win, bias_sel, bias_win, bias0, gates, o_cmp)


def _swa_sample_one(q, win, bias, sink):
    q = (q * SCALE).astype(BF16)
    s = _window_scores(q, win) + bias
    m = jnp.maximum(jnp.max(s, axis=1, keepdims=True), sink)
    e = jnp.exp(s - m)
    den = jnp.sum(e, axis=1, keepdims=True) + jnp.exp(sink - m)
    p = e / jnp.where(den > 0, den, 1.0)
    return _window_values(p, win)


def _swa_sample_batch(q_ref, win_ref, new_ref, bias_ref, sink_ref, o_ref, winout_ref, *, R):
    wins = [_shift_in(win_ref.at[pl.ds(r, 1)], new_ref.at[pl.ds(r, 1)], SWA_WINDOW) for r in range(R)]
    outs = [_swa_sample_one(q_ref[r], wins[r], bias_ref[...], sink_ref[...]) for r in range(R)]
    for r in range(R):
        winout_ref[r] = wins[r]
        o_ref[r] = outs[r]


def _swa_sample(q, win, new_col, bias, sinks):
    B = q.shape[0]
    R = math.gcd(B, SAMPLE_REQS)
    spec3 = lambda shp: pl.BlockSpec(shp, lambda b: (b, 0, 0))
    return pl.pallas_call(
        functools.partial(_swa_sample_batch, R=R),
        grid=(B // R,),
        in_specs=[spec3((R, N_HEADS, HEAD_DIM)), spec3((R, KV_COLS, SWA_WINDOW)), spec3((R, KV_COLS, 1)),
                  pl.BlockSpec((N_HEADS, SWA_WINDOW), lambda b: (0, 0)),
                  pl.BlockSpec((N_HEADS, 1), lambda b: (0, 0))],
        out_specs=[spec3((R, N_HEADS, HEAD_DIM)), spec3((R, KV_COLS, SWA_WINDOW))],
        out_shape=[jax.ShapeDtypeStruct((B, N_HEADS, HEAD_DIM), F32), jax.ShapeDtypeStruct((B, KV_COLS, SWA_WINDOW), F32)],
        compiler_params=_cp(("arbitrary",)),
        name="swa_sample",
    )(q, win, new_col, bias, sinks)


def _window_to_device_view(cache, width):
    return jnp.transpose(cache, (0, 1, 3, 4, 5, 2)).reshape(cache.shape[0], KV_COLS, width)


def _window_from_device_view(buf, width):
    return jnp.transpose(buf.reshape(buf.shape[0], 1, 2, N_KV, HEAD_DIM, width), (0, 1, 5, 2, 3, 4))


def _split_mod(m, rows):
    sh, sc, gt = m[rows, 0:D_MODEL], m[rows, D_MODEL:2 * D_MODEL], m[rows, 2 * D_MODEL:]
    return sh, sc, gt


def _ffn(x_tm, mod, rows, B, T, w_up, conv_w, conv_b, w_down, prev0, ln_g, ln_b, tt, tm):
    sh, sc, gt = _split_mod(mod, rows)
    a, buf = _ffn_up(x_tm.reshape(T, B, D_MODEL), sc, sh, w_up, conv_w, conv_b, prev0, tt)
    gate = gt[:, None, :] if T > 1 else gt[None]
    nb, nt = (B, T) if T > 1 else (1, B)
    x = _mm_res_ln([a.reshape(nt, nb * D_FF)], [True], [w_down], x_tm.reshape(nt, nb * D_MODEL), True, gate,
                   ln_g, ln_b, False, nb, nt, tm)
    return x, buf


def kernel(x_prompt, x_sample, cache_kv_cmp, cache_kv_slc, cache_kv_win, state_lru_conv, state_lru_h, cache_kv_swa, state_s5_re, state_s5_im, state_ffn_conv, page_table, c_prompt, c_sample, rel_bias, w_mod, b_mod, ln_g, ln_b, w_in_ab, w_out_ab, nsa_cmp_pe, nsa_cmp_w, lru_conv_w, lru_conv_b, lru_w_r, lru_b_r, lru_w_i, lru_b_i, lru_lambda, w_in_cd, w_out_cd, swa_sinks, s5_lambda_re, s5_lambda_im, s5_log_dt, s5_b_re, s5_b_im, s5_c_re, s5_c_im, s5_d, s5_w_glu, s5_b_glu, w_ffn_up, w_ffn_conv, b_ffn_conv, w_ffn_down):
    assert N_AB == 1 and DEPTH == 2
    Bp, T, D = x_prompt.shape
    Bs = x_sample.shape[0]
    n_pool = cache_kv_cmp.shape[0]
    TM = 512 if T % 512 == 0 else T
    TT_REC = 64 if T % 64 == 0 else T
    TT_S5 = 32 if T % 32 == 0 else T
    TT_FF = 32 if T % 32 == 0 else T
    prow, srow = slice(0, Bp), slice(Bp, Bp + Bs)

    mod = _ada_mod_all(jnp.concatenate([c_prompt, c_sample], axis=0),
                       w_mod.reshape(2 * DEPTH, D, 3 * D), b_mod.reshape(2 * DEPTH, 1, 3 * D))
    btiles = _bias_tiles(rel_bias)
    xs = x_sample.reshape(1, Bs, D)

    w = w_in_ab[0].astype(BF16)
    o = np.cumsum([0, N_HEADS * HEAD_DIM, KV_COLS, KV_COLS, KV_COLS, 3 * N_HEADS, LRU_WIDTH, LRU_WIDTH])
    w_gates = jnp.pad(w[:, o[4]:o[5]], ((0, 0), (0, LANES - 3 * N_HEADS)))
    ws = [w[:, o[0]:o[1]], w[:, o[1]:o[2]], w[:, o[2]:o[3]], w[:, o[3]:o[4]], w_gates, w[:, o[5]:o[6]], w[:, o[6]:o[7]]]
    kinds_p = [['bm'], ['pt'], ['bm', 'pt'], ['bm', 'pt'], ['bm'], ['tm'], ['tm']]
    kinds_s = [['bm'], ['bm'], ['bm'], ['bm'], ['bm'], ['tm'], ['tm']]
    w2, pe2 = _compress_pool_weights(nsa_cmp_w[0], nsa_cmp_pe[0])
    lru = {'conv_w': lru_conv_w[0], 'conv_b': lru_conv_b[0].reshape(1, -1),
           'w_r': _block_diag(lru_w_r[0]).astype(BF16), 'b_r': lru_b_r[0].reshape(1, -1),
           'w_i': _block_diag(lru_w_i[0]).astype(BF16), 'b_i': lru_b_i[0].reshape(1, -1),
           'lam': lru_lambda[0].reshape(1, -1)}
    w_out = w_out_ab[0].astype(BF16)
    w_out_parts = [w_out[0:N_HEADS * HEAD_DIM], w_out[N_HEADS * HEAD_DIM:]]

    shp, scp, gtp = _split_mod(mod[0], prow)
    q, kvc_pt, kvs, kvs_pt, kvw, kvw_pt, gates, xb, gb = _modmm(x_prompt, scp[:, None], shp[:, None],
                                                                list(zip(ws, kinds_p)), TM)
    nc = T // CMP_BLOCK
    npg = T // PAGE_SIZE
    kcv = _compress_pool(jnp.arange(Bp * npg, dtype=jnp.int32), kvc_pt.reshape(Bp * npg, KV_COLS, PAGE_SIZE), w2, pe2)
    kcv = _blocks_by_page(kcv, (Bp, npg)).reshape(Bp, nc, KV_COLS)
    kcv = jnp.concatenate([kcv[:, 0::2], kcv[:, 1::2]], axis=1)
    o_att = _nsa_prompt_attn(q, kvs, kvs_pt, kvw, kvw_pt, kcv[:, :, 0:LANES], jnp.swapaxes(kcv[:, :, LANES:], 1, 2),
                             gates, btiles)
    y_lru, p_lru_buf, p_lru_h = _rglru(xb.reshape(T, Bp, LRU_WIDTH), gb.reshape(T, Bp, LRU_WIDTH),
                                       jnp.zeros((LRU_CONV - 1, Bp, LRU_WIDTH), F32), jnp.zeros((Bp, LRU_WIDTH), F32),
                                       lru, TT_REC)
    xp1 = _mm_res_ln([o_att, y_lru.reshape(T, Bp * LRU_WIDTH)], [False, True], w_out_parts, x_prompt, False,
                     gtp[:, None], ln_g[0, 0], ln_b[0, 0], True, Bp, T, TM)
    paged_out = lambda a: jnp.transpose(a.reshape(Bp, npg, 1, 2, N_KV, HEAD_DIM, PAGE_SIZE), (0, 1, 6, 2, 3, 4, 5))
    p_kv_cmp, p_kv_slc = paged_out(kvc_pt), paged_out(kvs_pt)
    wk = min(NSA_WINDOW, T)
    p_kv_win = _window_from_device_view(
        jnp.swapaxes(kvw_pt[:, npg - wk // PAGE_SIZE:], 1, 2).reshape(Bp, KV_COLS, wk), wk)

    shs, scs, gts = _split_mod(mod[0], srow)
    q_s, kvc_s, kvs_s, kvw_s, gates_s, xb_s, gb_s = _modmm(xs, scs[None], shs[None], list(zip(ws, kinds_s)), Bs)
    q_s, kvc_s, kvs_s, kvw_s, gates_s = q_s[0], kvc_s[0], kvs_s[0], kvw_s[0], gates_s[0]
    pool_cmp_t = jnp.transpose(cache_kv_cmp, (0, 2, 3, 4, 5, 1)).reshape(n_pool, KV_COLS, PAGE_SIZE)
    pool_slc_t = jnp.transpose(cache_kv_slc, (0, 2, 3, 4, 5, 1)).reshape(n_pool, 2, N_KV, HEAD_DIM, PAGE_SIZE)
    n_pages = page_table.shape[1]
    kcv_past = _compress_pool(page_table.reshape(-1), pool_cmp_t, w2, pe2)
    kcv_past = kcv_past.reshape(2 * N_KV, Bs, n_pages, BLK_PER_PAGE * HEAD_DIM)
    tail_pages = jnp.pad(kvc_s[:, :, None], ((0, 0), (0, 0), (0, PAGE_SIZE - 1)))
    kcv_tail = _compress_pool(jnp.arange(Bs, dtype=jnp.int32), tail_pages, w2, pe2)
    kcv_tail = _blocks_by_page(kcv_tail, (Bs,))[:, 0:SEL_BLOCK // CMP_BLOCK]
    kcv_tail = jnp.pad(kcv_tail, ((0, 0), (0, N_PAST_PAGES - SEL_BLOCK // CMP_BLOCK), (0, 0)))
    q_s3 = q_s.reshape(Bs, N_HEADS, HEAD_DIM)
    o_cmp_s, imp_s = _cmp_sample(q_s3, kcv_past, kcv_tail)
    idx_t = _topk_sample(imp_s.reshape(Bs * N_KV, N_CAND).T)
    idx = idx_t.T.reshape(Bs, N_KV, N_SEL)
    per_page_s = PAGE_SIZE // SEL_BLOCK
    idx_p = jnp.minimum(idx, N_PAST_SEL - 1)
    pages = page_table[jnp.arange(Bs)[:, None, None], idx_p // per_page_s]
    blk_id = jnp.arange(N_PAST_SEL + 1, dtype=jnp.int32)
    kpos = blk_id[:, None] * SEL_BLOCK + jnp.arange(SEL_BLOCK, dtype=jnp.int32)[None, :]
    bias_half = jnp.swapaxes(_rel_bias_at(rel_bias, PAST_LEN - kpos), 1, 2)
    in_upper = (jnp.minimum(blk_id, N_PAST_SEL - 1) % per_page_s == 1)[:, None, None]
    zeros_half = jnp.zeros_like(bias_half)
    bias_sel = jnp.where(in_upper, jnp.concatenate([zeros_half, bias_half], axis=2),
                         jnp.concatenate([bias_half, zeros_half], axis=2))
    bias_win = _rel_bias_at(rel_bias, NSA_WINDOW - 1 - jnp.arange(NSA_WINDOW, dtype=jnp.int32)).T
    gates3 = jnp.swapaxes(gates_s[:, 0:3 * N_HEADS].reshape(Bs, 3, N_HEADS), 1, 2)
    o_att_s, win_new = _selwin_sample(idx.reshape(-1), pages.reshape(-1).astype(jnp.int32), pool_slc_t,
                                      q_s3, kvs_s[:, None, :],
                                      _window_to_device_view(cache_kv_win, NSA_WINDOW), kvw_s[:, :, None],
                                      bias_sel, bias_win, _rel_bias_at(rel_bias, jnp.zeros((1,), jnp.int32)).T,
                                      gates3, o_cmp_s)
    y_lru_s, s_lru_buf, s_lru_h = _rglru(xb_s.reshape(1, Bs, LRU_WIDTH), gb_s.reshape(1, Bs, LRU_WIDTH),
                                         jnp.swapaxes(state_lru_conv[:, 0], 0, 1), state_lru_h[:, 0], lru, 1)
    xs1 = _mm_res_ln([o_att_s.reshape(1, Bs, N_HEADS * HEAD_DIM), y_lru_s], [False, False], w_out_parts, xs, False,
                     gts[None], ln_g[0, 0], ln_b[0, 0], True, 1, Bs, Bs)

    ffn_w = [(w_ffn_up[li].astype(BF16), w_ffn_conv[li], b_ffn_conv[li], w_ffn_down[li].astype(BF16)) for li in range(DEPTH)]
    zero_ff = jnp.zeros((FFN_CONV - 1, Bp, D_FF), F32)
    xp2, p_ff0 = _ffn(xp1, mod[1], prow, Bp, T, *ffn_w[0], zero_ff, ln_g[0, 1], ln_b[0, 1], TT_FF, TM)
    xs2, s_ff0 = _ffn(xs1, mod[1], srow, Bs, 1, *ffn_w[0], jnp.swapaxes(state_ffn_conv[:, 0], 0, 1),
                      ln_g[0, 1], ln_b[0, 1], 1, Bs)

    w = w_in_cd[0].astype(BF16)
    ws = [w[:, 0:N_HEADS * HEAD_DIM], w[:, N_HEADS * HEAD_DIM:N_HEADS * HEAD_DIM + KV_COLS], w[:, N_HEADS * HEAD_DIM + KV_COLS:]]
    s5p = _s5_prepare(s5_lambda_re[0], s5_lambda_im[0], s5_log_dt[0], s5_b_re[0], s5_b_im[0], s5_c_re[0], s5_c_im[0],
                      s5_d[0], s5_w_glu[0], s5_b_glu[0])
    w_out = w_out_cd[0].astype(BF16)
    w_out_parts = [w_out[0:N_HEADS * HEAD_DIM], w_out[N_HEADS * HEAD_DIM:]]
    sink_rows = jnp.repeat(swa_sinks[0].reshape(N_KV, 1, GROUP), TQ, axis=2)

    shp, scp, gtp = _split_mod(mod[2], prow)
    q, kv, kv_pt, u = _modmm(xp2, scp[:, None], shp[:, None], list(zip(ws, [['bm'], ['bm', 'pt'], ['tm']])), TM)
    o_att = _swa_prompt_attn(q, kv, kv_pt, sink_rows, btiles)
    zst = jnp.zeros((Bp, S5_NSTATE), F32)
    y_s5, p_s5_re, p_s5_im = _s5(u.reshape(T, Bp, S5_WIDTH), zst, zst, s5p, TT_S5)
    xp3 = _mm_res_ln([o_att, y_s5.reshape(T, Bp * S5_WIDTH)], [False, True], w_out_parts, xp2, False,
                     gtp[:, None], ln_g[1, 0], ln_b[1, 0], True, Bp, T, TM)
    assert SWA_WINDOW == PAGE_SIZE and T >= SWA_WINDOW
    p_kv_swa = _window_from_device_view(kv_pt[:, npg - 1], SWA_WINDOW)

    shs, scs, gts = _split_mod(mod[2], srow)
    q_s, kv_s, u_s = _modmm(xs2, scs[None], shs[None], list(zip(ws, [['bm'], ['bm'], ['tm']])), Bs)
    bias_swa = _rel_bias_at(rel_bias, SWA_WINDOW - 1 - jnp.arange(SWA_WINDOW, dtype=jnp.int32)).T
    o_att_s, swa_new = _swa_sample(q_s[0].reshape(Bs, N_HEADS, HEAD_DIM), _window_to_device_view(cache_kv_swa, SWA_WINDOW),
                                   kv_s[0][:, :, None], bias_swa, swa_sinks[0].reshape(N_HEADS, 1))
    y_s5_s, s_s5_re, s_s5_im = _s5(u_s.reshape(1, Bs, S5_WIDTH), state_s5_re[:, 0].reshape(Bs, S5_NSTATE),
                                   state_s5_im[:, 0].reshape(Bs, S5_NSTATE), s5p, 1)
    xs3 = _mm_res_ln([o_att_s.reshape(1, Bs, N_HEADS * HEAD_DIM), y_s5_s], [False, False], w_out_parts, xs2, False,
                     gts[None], ln_g[1, 0], ln_b[1, 0], True, 1, Bs, Bs)

    xp4, p_ff1 = _ffn(xp3, mod[3], prow, Bp, T, *ffn_w[1], zero_ff, ln_g[1, 1], ln_b[1, 1], TT_FF, TM)
    xs4, s_ff1 = _ffn(xs3, mod[3], srow, Bs, 1, *ffn_w[1], jnp.swapaxes(state_ffn_conv[:, 1], 0, 1),
                      ln_g[1, 1], ln_b[1, 1], 1, Bs)

    kv6 = lambda a, n: a.reshape(n, 1, 1, 2, N_KV, HEAD_DIM)
    return (xp4, xs4.reshape(Bs, 1, D),
            p_kv_cmp, p_kv_slc, p_kv_win,
            jnp.swapaxes(p_lru_buf, 0, 1)[:, None], p_lru_h[:, None], p_kv_swa,
            p_s5_re.reshape(Bp, 1, S5_GROUPS, S5_STATE), p_s5_im.reshape(Bp, 1, S5_GROUPS, S5_STATE),
            jnp.stack([jnp.swapaxes(p_ff0, 0, 1), jnp.swapaxes(p_ff1, 0, 1)], axis=1),
            kv6(kvc_s, Bs), kv6(kvs_s, Bs), _window_from_device_view(win_new, NSA_WINDOW),
            jnp.swapaxes(s_lru_buf, 0, 1)[:, None], s_lru_h[:, None],
            _window_from_device_view(swa_new, SWA_WINDOW),
            s_s5_re.reshape(Bs, 1, S5_GROUPS, S5_STATE), s_s5_im.reshape(Bs, 1, S5_GROUPS, S5_STATE),
            jnp.stack([jnp.swapaxes(s_ff0, 0, 1), jnp.swapaxes(s_ff1, 0, 1)], axis=1))
```

```python
import functools
import math

import numpy as np
import jax
import jax.numpy as jnp
from jax import lax
from jax.experimental import pallas as pl
from jax.experimental.pallas import tpu as pltpu

D_MODEL = 1024
DEPTH = 2
PAST_LEN = 16384
PAGE_SIZE = 128
HEAD_DIM = 64
N_AB = (DEPTH + 1) // 2
N_HEADS = 8
N_KV = 2
GROUP = N_HEADS // N_KV
CMP_BLOCK = 32
SEL_BLOCK = 64
N_SEL = 16
NSA_WINDOW = 512
LRU_WIDTH = D_MODEL // 2
LRU_BLOCKS = 8
LRU_CONV = 4
LRU_C = 8.0
SWA_WINDOW = 128
S5_WIDTH = D_MODEL // 2
S5_GROUP_CH = 16
S5_GROUPS = S5_WIDTH // S5_GROUP_CH
S5_STATE = 64
D_FF = (8 * D_MODEL // 3) // 128 * 128
FFN_CONV = 3
N_BUCKETS = 32
MAX_DISTANCE = 128
DN_ALPHA = (2 * DEPTH) ** 0.25
LN_EPS = 1e-5
SCALE = HEAD_DIM ** -0.5
NEG = -1e30
FORCE_SCORE = 1e4
KV_COLS = 2 * N_KV * HEAD_DIM
F32 = jnp.float32
BF16 = jnp.bfloat16

LANES = 128
TQ = 128
VMEM_LIMIT = 56 * 1024 * 1024


def _cp(sem, vmem=VMEM_LIMIT):
    return pltpu.CompilerParams(dimension_semantics=sem, vmem_limit_bytes=vmem)


def _dot(a, b):
    return jnp.dot(a, b, preferred_element_type=F32)


def _dot_nt(a, b):
    return lax.dot_general(a, b, (((1,), (1,)), ((), ())), preferred_element_type=F32)


def _gelu(x):
    cdf = 0.5 * (1.0 + jnp.tanh(math.sqrt(2.0 / math.pi) * (x + 0.044715 * (x * x * x))))
    return x * cdf


def _softplus(x):
    return jnp.maximum(x, 0.0) + jnp.log1p(jnp.exp(-jnp.abs(x)))


def _expm1(x):
    u = jnp.exp(x)
    um1 = u - 1.0
    edge = (u == 1.0) | (um1 == -1.0)
    r = um1 * x / jnp.log(jnp.where(edge, 2.0, u))
    return jnp.where(u == 1.0, x, jnp.where(um1 == -1.0, -1.0, r))


def _t5_bucket(dist):
    n = jnp.maximum(dist, 0)
    max_exact = N_BUCKETS // 2
    nf = jnp.maximum(n, 1).astype(F32)
    steps = jnp.log(nf / max_exact) / math.log(MAX_DISTANCE / max_exact) * (N_BUCKETS - max_exact)
    large = max_exact + jnp.floor(steps).astype(jnp.int32)
    return jnp.where(n < max_exact, n, jnp.minimum(large, N_BUCKETS - 1))


def _rel_bias_at(rel_bias, dist):
    onehot = jax.nn.one_hot(_t5_bucket(dist), N_BUCKETS, dtype=F32)
    return jnp.matmul(onehot, rel_bias.astype(F32), precision=lax.Precision.HIGHEST)


def _bucket_saturates_from(n0, n1):
    n = np.arange(n0, n1, dtype=np.float64)
    b = 16 + np.floor(np.log(n / 16.0) / math.log(8.0) * 16.0)
    return bool(np.all(b >= N_BUCKETS - 1 + 0.5))


assert _bucket_saturates_from(TQ + 1, PAST_LEN + 2 * TQ)


def _mod_body(c_ref, w_ref, b_ref, o_ref):
    c = c_ref[...]
    h = (c * jax.nn.sigmoid(c)).astype(BF16)
    o_ref[0] = _dot(h, w_ref[0].astype(BF16)) + b_ref[0]


def _ada_mod_all(c_all, w_mod, b_mod):
    n_sub = w_mod.shape[0]
    rows = c_all.shape[0]
    tn = 1024
    return pl.pallas_call(
        _mod_body,
        grid=(n_sub, 3 * D_MODEL // tn),
        in_specs=[pl.BlockSpec((rows, D_MODEL), lambda s, n: (0, 0)),
                  pl.BlockSpec((1, D_MODEL, tn), lambda s, n: (s, 0, n)),
                  pl.BlockSpec((1, 1, tn), lambda s, n: (s, 0, n))],
        out_specs=pl.BlockSpec((1, rows, tn), lambda s, n: (s, 0, n)),
        out_shape=jax.ShapeDtypeStruct((n_sub, rows, 3 * D_MODEL), F32),
        compiler_params=_cp(("arbitrary", "arbitrary")),
        name="ada_mod",
    )(c_all, w_mod, b_mod)


def _modmm_body(x_ref, sc_ref, sh_ref, *refs, w_of_out, kinds):
    n_w = max(w_of_out) + 1
    h = (x_ref[0] * (1.0 + sc_ref[0]) + sh_ref[0]).astype(BF16)
    res = {}
    for o_ref, wi, kind in zip(refs[n_w:], w_of_out, kinds):
        if wi not in res:
            res[wi] = _dot(h, refs[wi][...])
        r = res[wi]
        if kind == 'tm':
            o_ref[...] = r
        elif kind == 'bm':
            o_ref[0] = r
        else:
            for p in range(r.shape[0] // PAGE_SIZE):
                o_ref[0, p] = r[p * PAGE_SIZE:(p + 1) * PAGE_SIZE, :].T


def _modmm(x, sc, sh, outs, tm):
    B, T, D = x.shape
    per_row = sc.shape[1] != 1
    mod_spec = pl.BlockSpec((1, tm if per_row else 1, D), (lambda b, t: (b, t, 0)) if per_row else (lambda b, t: (b, 0, 0)))
    in_specs = [pl.BlockSpec((1, tm, D), lambda b, t: (b, t, 0)), mod_spec, mod_spec]
    out_specs, out_shape, w_of_out, kinds = [], [], [], []
    for wi, (w, ks) in enumerate(outs):
        n = w.shape[1]
        in_specs.append(pl.BlockSpec((D, n), lambda b, t: (0, 0)))
        for kind in ks:
            w_of_out.append(wi)
            kinds.append(kind)
            if kind == 'tm':
                out_specs.append(pl.BlockSpec((tm, n), lambda b, t: (t, b)))
                out_shape.append(jax.ShapeDtypeStruct((T, B * n), F32))
            elif kind == 'bm':
                out_specs.append(pl.BlockSpec((1, tm, n), lambda b, t: (b, t, 0)))
                out_shape.append(jax.ShapeDtypeStruct((B, T, n), F32))
            else:
                out_specs.append(pl.BlockSpec((1, tm // PAGE_SIZE, n, PAGE_SIZE), lambda b, t: (b, t, 0, 0)))
                out_shape.append(jax.ShapeDtypeStruct((B, T // PAGE_SIZE, n, PAGE_SIZE), F32))
    return pl.pallas_call(
        functools.partial(_modmm_body, w_of_out=tuple(w_of_out), kinds=tuple(kinds)),
        grid=(B, T // tm),
        in_specs=in_specs, out_specs=out_specs, out_shape=out_shape,
        compiler_params=_cp(("arbitrary", "arbitrary")),
        name="mod_proj",
    )(x, sc, sh, *[w for w, _ in outs])


def _mm_res_ln_body(*refs, n_a, a_tm, x_tm, o_tm):
    a_refs, w_refs = refs[:n_a], refs[n_a:2 * n_a]
    x_ref, gt_ref, g_ref, b_ref, o_ref = refs[2 * n_a:]
    y = None
    for a_ref, w_ref, tmj in zip(a_refs, w_refs, a_tm):
        a = (a_ref[...] if tmj else a_ref[0]).astype(BF16)
        r = _dot(a, w_ref[...])
        y = r if y is None else y + r
    x = x_ref[...] if x_tm else x_ref[0]
    z = DN_ALPHA * x + (1.0 + gt_ref[0]) * y
    mu = jnp.mean(z, axis=-1, keepdims=True)
    var = jnp.mean(jnp.square(z - mu), axis=-1, keepdims=True)
    out = (z - mu) * lax.rsqrt(var + LN_EPS) * g_ref[...] + b_ref[...]
    if o_tm:
        o_ref[...] = out
    else:
        o_ref[0] = out


def _mm_res_ln(a_list, a_tm, ws, x, x_tm, gate, ln_g, ln_b, o_tm, B, T, tm):
    D = D_MODEL
    per_row = gate.shape[1] != 1

    def spec(k, tmj):
        if tmj:
            return pl.BlockSpec((tm, k), lambda b, t: (t, b))
        return pl.BlockSpec((1, tm, k), lambda b, t: (b, t, 0))

    in_specs = [spec(w.shape[0], tmj) for w, tmj in zip(ws, a_tm)]
    in_specs += [pl.BlockSpec(w.shape, lambda b, t: (0, 0)) for w in ws]
    in_specs += [spec(D, x_tm),
                 pl.BlockSpec((1, tm if per_row else 1, D), (lambda b, t: (b, t, 0)) if per_row else (lambda b, t: (b, 0, 0))),
                 pl.BlockSpec((1, D), lambda b, t: (0, 0)),
                 pl.BlockSpec((1, D), lambda b, t: (0, 0))]
    out_shape = jax.ShapeDtypeStruct((T, B * D) if o_tm else (B, T, D), F32)
    return pl.pallas_call(
        functools.partial(_mm_res_ln_body, n_a=len(ws), a_tm=tuple(a_tm), x_tm=x_tm, o_tm=o_tm),
        grid=(B, T // tm),
        in_specs=in_specs, out_specs=spec(D, o_tm), out_shape=out_shape,
        compiler_params=_cp(("arbitrary", "arbitrary")),
        name="proj_res_ln",
    )(*a_list, *ws, x, gate, ln_g.reshape(1, D), ln_b.reshape(1, D))


POOL_PAGES = 128
POOL_PITCH = 264
BLK_PER_PAGE = PAGE_SIZE // CMP_BLOCK


def _compress_pool_body(ids_ref, pool_ref, w_ref, pe_ref, o_ref, buf_ref, sem, *, P):
    i = pl.program_id(0)
    n = pl.num_programs(0)

    def page_copy(step, slot, p):
        dst = pl.multiple_of((slot * P + p) * POOL_PITCH, 8)
        return pltpu.make_async_copy(pool_ref.at[ids_ref[step * P + p]], buf_ref.at[pl.ds(dst, KV_COLS), :], sem.at[slot])

    def start_pages(step, slot):
        def go(p, c):
            page_copy(step, slot, p).start()
            return c
        lax.fori_loop(0, P, go, 0)

    @pl.when(i == 0)
    def _():
        start_pages(0, 0)

    @pl.when(i + 1 < n)
    def _():
        start_pages(i + 1, (i + 1) % 2)

    slot = i % 2

    def wait_page(p, c):
        page_copy(i, slot, p).wait()
        return c

    lax.fori_loop(0, P, wait_page, 0)
    base = slot * (P * POOL_PITCH)

    def feature_rows(f):
        return buf_ref[pl.ds(base + f, P, stride=POOL_PITCH), :]

    DC = 8
    for s in range(2):
        out = _dot(pe_ref[s].astype(BF16), w_ref[s])[0:1]
        for d0 in range(0, HEAD_DIM, DC):
            lhs = jnp.concatenate(
                [jnp.concatenate([feature_rows((s * N_KV + k) * HEAD_DIM + d) for d in range(d0, d0 + DC)], axis=1)
                 for k in range(N_KV)], axis=0).astype(BF16)
            out = out + _dot(lhs, w_ref[s, d0 * PAGE_SIZE:(d0 + DC) * PAGE_SIZE, :])
        for k in range(N_KV):
            o_ref[s * N_KV + k] = out[k * P:(k + 1) * P]


def _compress_pool(page_ids, pool_t, w2, pe2):
    n = page_ids.shape[0]
    P = math.gcd(n, POOL_PAGES)
    assert P % 8 == 0
    kdim = HEAD_DIM * PAGE_SIZE
    ncol = BLK_PER_PAGE * HEAD_DIM
    grid_spec = pltpu.PrefetchScalarGridSpec(
        num_scalar_prefetch=1,
        grid=(n // P,),
        in_specs=[pl.BlockSpec(memory_space=pl.ANY),
                  pl.BlockSpec((2, kdim, ncol), lambda i, ids: (0, 0, 0), pipeline_mode=pl.Buffered(1)),
                  pl.BlockSpec((2, 8, kdim), lambda i, ids: (0, 0, 0), pipeline_mode=pl.Buffered(1))],
        out_specs=pl.BlockSpec((2 * N_KV, P, ncol), lambda i, ids: (0, i, 0)),
        scratch_shapes=[pltpu.VMEM((2 * P * POOL_PITCH, PAGE_SIZE), F32), pltpu.SemaphoreType.DMA((2,))],
    )
    return pl.pallas_call(
        functools.partial(_compress_pool_body, P=P),
        grid_spec=grid_spec,
        out_shape=jax.ShapeDtypeStruct((2 * N_KV, n, ncol), F32),
        compiler_params=_cp(("arbitrary",)),
        name="nsa_compress_pool",
    )(page_ids, pool_t, w2, pe2)


def _blocks_by_page(kcv, lead):
    kcv = kcv.reshape(2, N_KV, *lead, BLK_PER_PAGE, HEAD_DIM)
    n = len(lead)
    perm = tuple(range(2, 2 + n)) + (2 + n, 0, 1, 3 + n)
    return jnp.transpose(kcv, perm).reshape(*lead, BLK_PER_PAGE, KV_COLS)


def _compress_pool_weights(cmp_w, cmp_pe):
    eye = jnp.eye(BLK_PER_PAGE, dtype=F32)
    w2 = jnp.einsum('scde,bB->sdbcBe', cmp_w, eye).reshape(2, HEAD_DIM * PAGE_SIZE, BLK_PER_PAGE * HEAD_DIM).astype(BF16)
    pe = jnp.transpose(cmp_pe, (0, 2, 1))[:, :, None, :]
    pe = jnp.broadcast_to(pe, (2, HEAD_DIM, BLK_PER_PAGE, CMP_BLOCK)).reshape(2, 1, HEAD_DIM * PAGE_SIZE)
    return w2, jnp.broadcast_to(pe, (2, 8, HEAD_DIM * PAGE_SIZE))


def _padded_queries(q):
    lo = lax.broadcasted_iota(jnp.int32, (TQ, LANES), 1) < HEAD_DIM
    out = []
    for h in range(N_HEADS):
        x = q[:, (h // 2) * LANES:(h // 2 + 1) * LANES]
        k = h // GROUP
        if h % 2 != k:
            x = pltpu.roll(x, HEAD_DIM, axis=1)
        out.append(jnp.where(lo if k == 0 else ~lo, x, 0.0).astype(BF16))
    return out


T_BEYOND, T_DIAG, T_NEAR, T_FAR, T_FAR_EDGE, T_NEAR_EDGE = range(6)
W_LANES = GROUP * TQ


def _col_max(mv, s):
    return jnp.maximum(mv, jnp.max(s.reshape(s.shape[0] // 8, 8, s.shape[1]), axis=0))


def _col_sum(lv, p):
    return lv + jnp.sum(p.reshape(p.shape[0] // 8, 8, p.shape[1]), axis=0)


def _scores(kv_ref, qg, page, pages):
    start = pl.multiple_of(page * TQ, TQ)
    return _dot_nt(kv_ref[0, pl.ds(start, pages * TQ), 0:LANES].astype(BF16), qg)


def _values_t(kvt_ref, k, page, pages):
    vts = [kvt_ref[0, page + p, LANES + k * HEAD_DIM:LANES + (k + 1) * HEAD_DIM, :] for p in range(pages)]
    return (vts[0] if pages == 1 else jnp.concatenate(vts, axis=1)).astype(BF16)


def _softmax_pass(s_ref, kvt_ref, row0, page0, n_steps, pages, m_rows, init):
    rows = pages * TQ

    def body(t, carry):
        out = []
        for k in range(N_KV):
            lv, acc = carry[k]
            start = pl.multiple_of(row0 + t * rows, TQ)
            p = jnp.exp(s_ref[k, pl.ds(start, rows), :] - m_rows[k])
            vt = _values_t(kvt_ref, k, page0 + t * pages, pages)
            out.append((_col_sum(lv, p), acc + _dot(vt, p.astype(BF16))))
        return tuple(out)

    return lax.fori_loop(0, n_steps, body, init)


def _normalise(m_row, lv, acc, extra=None):
    l = jnp.sum(lv, axis=0, keepdims=True)
    if extra is not None:
        l = l + extra
    valid = m_row > 0.5 * NEG
    return jnp.where(valid, acc, 0.0) / jnp.where(valid & (l > 0), l, 1.0)


def _window_branch(qgs, kv_ref, kvt_ref, s_ref, tiles_ref, i, n_win, m_init):
    lo = jnp.maximum(i - n_win, 0)
    count = i + 1 - lo
    start = lo - jnp.where((count % 2 == 1) & (lo >= 1), 1, 0)
    n_steps = (count + 1) // 2

    def tile_of(j):
        d = i - j
        if n_win == 1:
            inside = jnp.where(d == 0, T_DIAG, T_NEAR_EDGE)
        else:
            inside = jnp.where(d == n_win, T_FAR_EDGE, jnp.clip(d, 0, 2) + T_DIAG)
        return jnp.where((d < 0) | (d > n_win), T_BEYOND, inside)

    def pass1(t, mvs):
        j0 = start + 2 * t
        out = []
        for k in range(N_KV):
            s = _scores(kv_ref, qgs[k], j0, 2)
            s = jnp.concatenate([s[0:TQ] + tiles_ref[tile_of(j0), k], s[TQ:2 * TQ] + tiles_ref[tile_of(j0 + 1), k]], axis=0)
            s_ref[k, pl.ds(pl.multiple_of(t * 2 * TQ, 2 * TQ), 2 * TQ), :] = s
            out.append(_col_max(mvs[k], s))
        return tuple(out)

    mvs = lax.fori_loop(0, n_steps, pass1, tuple(m_init))
    m_rows = [jnp.max(mv, axis=0, keepdims=True) for mv in mvs]
    zero = (jnp.zeros((8, W_LANES), F32), jnp.zeros((HEAD_DIM, W_LANES), F32))
    res = _softmax_pass(s_ref, kvt_ref, 0, start, n_steps, 2, m_rows, (zero,) * N_KV)
    return m_rows, res


def _store_heads(o_ref, o_groups):
    for k in range(N_KV):
        for pair in range(GROUP // 2):
            a = o_groups[k][:, (2 * pair) * TQ:(2 * pair + 1) * TQ]
            b = o_groups[k][:, (2 * pair + 1) * TQ:(2 * pair + 2) * TQ]
            col = (k * GROUP + 2 * pair) * HEAD_DIM
            o_ref[0, :, col:col + 2 * HEAD_DIM] = jnp.concatenate([a, b], axis=0).T


def _nsa_prompt_body(q_ref, kvs_ref, vst_ref, kvw_ref, vwt_ref, kc_ref, vct_ref, gates_ref, tiles_ref, o_ref,
                     sel_ref, ssel_ref, swin_ref, *, T):
    i = pl.program_id(1)
    nc = T // CMP_BLOCK
    nsel = nc // 2
    t0 = i * TQ
    qpads = _padded_queries(q_ref[0] * SCALE)
    gt = jax.nn.sigmoid(gates_ref[0]).T
    W = GROUP * TQ
    c_lane = lax.broadcasted_iota(jnp.int32, (1, W), 1) & (TQ - 1)
    tpos = t0 + c_lane
    r_c = lax.broadcasted_iota(jnp.int32, (nc, W), 0)
    blk = 2 * (r_c % nsel) + r_c // nsel
    vis = (blk + 1) * CMP_BLOCK - 1 <= tpos
    j_s = lax.broadcasted_iota(jnp.int32, (nsel, TQ), 0)
    cur = (t0 + lax.broadcasted_iota(jnp.int32, (nsel, TQ), 1)) // SEL_BLOCK
    qgs, o_cmps = [], []
    for k in range(N_KV):
        qg = jnp.concatenate(qpads[GROUP * k:GROUP * (k + 1)], axis=0)
        qgs.append(qg)
        s = jnp.where(vis, _dot_nt(kc_ref[0].astype(BF16), qg), NEG)
        m = jnp.max(s, axis=0, keepdims=True)
        e = jnp.where(vis, jnp.exp(s - m), 0.0)
        den = jnp.sum(e, axis=0, keepdims=True)
        p = e / jnp.where(den > 0, den, 1.0)
        o_cmps.append(_dot(vct_ref[0, k * HEAD_DIM:(k + 1) * HEAD_DIM, :].astype(BF16), p.astype(BF16)))
        ps = p[:, 0:TQ] + p[:, TQ:2 * TQ] + p[:, 2 * TQ:3 * TQ] + p[:, 3 * TQ:4 * TQ]
        imp = ps[0:nsel] + ps[nsel:nc]
        forced = (j_s == 0) | (j_s == cur) | (j_s == cur - 1)
        score = jnp.where(forced, FORCE_SCORE, jnp.where(j_s <= cur, imp, NEG))
        rank = jnp.zeros((nsel, TQ), F32)
        for r in range(nsel):
            row = score[r:r + 1, :]
            ahead = jnp.where(j_s > r, jnp.where(row >= score, 1.0, 0.0), jnp.where(row > score, 1.0, 0.0))
            rank = rank + ahead
        chosen = (rank < N_SEL) & (j_s <= cur)
        sel_ref[k] = jnp.concatenate([jnp.where(chosen, 0.0, NEG)] * GROUP, axis=1)

    n_steps = (i + 2) // 2

    def sel_pass1(t, mvs):
        out = []
        for k in range(N_KV):
            s = _scores(kvs_ref, qgs[k], 2 * t, 2)
            parts = []
            for h in range(4):
                tile = jnp.clip(i - (2 * t + h // 2), -1, 2) + T_DIAG
                half = slice((h % 2) * SEL_BLOCK, (h % 2 + 1) * SEL_BLOCK)
                parts.append(s[h * SEL_BLOCK:(h + 1) * SEL_BLOCK] + sel_ref[k, pl.ds(4 * t + h, 1), :] + tiles_ref[tile, k, half, :])
            s = jnp.concatenate(parts, axis=0)
            ssel_ref[k, pl.ds(pl.multiple_of(t * 2 * TQ, 2 * TQ), 2 * TQ), :] = s
            out.append(_col_max(mvs[k], s))
        return tuple(out)

    m0 = jnp.full((8, W), NEG, F32)
    mvs = lax.fori_loop(0, n_steps, sel_pass1, (m0, m0))
    m_sel = [jnp.max(mv, axis=0, keepdims=True) for mv in mvs]
    zero = (jnp.zeros((8, W), F32), jnp.zeros((HEAD_DIM, W), F32))
    r_sel = _softmax_pass(ssel_ref, vst_ref, 0, 0, n_steps, 2, m_sel, (zero, zero))
    m_win, r_win = _window_branch(qgs, kvw_ref, vwt_ref, swin_ref, tiles_ref, i, NSA_WINDOW // TQ, (m0, m0))

    o_groups = []
    for k in range(N_KV):
        def gate_row(branch):
            return jnp.concatenate([gt[branch * N_HEADS + GROUP * k + hh:branch * N_HEADS + GROUP * k + hh + 1, :]
                                    for hh in range(GROUP)], axis=1)

        o_slc = _normalise(m_sel[k], *r_sel[k])
        o_win = _normalise(m_win[k], *r_win[k])
        o_groups.append(gate_row(0) * o_cmps[k] + gate_row(1) * o_slc + gate_row(2) * o_win)
    _store_heads(o_ref, o_groups)


def _bias_tiles(rel_bias):
    r = jnp.arange(TQ, dtype=jnp.int32)[:, None]
    c = jnp.arange(TQ, dtype=jnp.int32)[None, :]
    near = jnp.stack([_rel_bias_at(rel_bias, d * TQ + c - r) for d in range(2)])
    near = jnp.transpose(near.reshape(2, TQ, TQ, N_KV, GROUP), (0, 3, 1, 4, 2)).reshape(2, N_KV, TQ, GROUP * TQ)
    far = jnp.repeat(rel_bias[N_BUCKETS - 1].reshape(N_KV, 1, GROUP), TQ, axis=2).astype(F32)
    causal = jnp.tile(jnp.where(c >= r, 0.0, NEG), (1, GROUP))
    inside = jnp.tile(jnp.where(c < r, 0.0, NEG), (1, GROUP))
    full = jnp.zeros((N_KV, TQ, GROUP * TQ), F32)
    tiles = jnp.stack([full + NEG, near[0] + causal, near[1] + full, far + full, far + inside, near[1] + inside])
    return tiles.astype(F32)


def _nsa_prompt_attn(q, kvs, kvs_pt, kvw, kvw_pt, kc, vct, gates, btiles):
    B, T, _ = q.shape
    nc = T // CMP_BLOCK
    W = GROUP * TQ
    n_tiles = btiles.shape[0]
    assert T // TQ >= NSA_WINDOW // TQ + 2
    seq = pl.BlockSpec((1, T, KV_COLS), lambda b, i: (b, 0, 0))
    seq_pt = pl.BlockSpec((1, T // PAGE_SIZE, KV_COLS, PAGE_SIZE), lambda b, i: (b, 0, 0, 0))
    return pl.pallas_call(
        functools.partial(_nsa_prompt_body, T=T),
        grid=(B, T // TQ),
        in_specs=[pl.BlockSpec((1, TQ, N_HEADS * HEAD_DIM), lambda b, i: (b, i, 0)),
                  seq, seq_pt, seq, seq_pt,
                  pl.BlockSpec((1, nc, LANES), lambda b, i: (b, 0, 0)),
                  pl.BlockSpec((1, LANES, nc), lambda b, i: (b, 0, 0)),
                  pl.BlockSpec((1, TQ, LANES), lambda b, i: (b, i, 0)),
                  pl.BlockSpec((n_tiles, N_KV, TQ, W), lambda b, i: (0, 0, 0, 0))],
        out_specs=pl.BlockSpec((1, TQ, N_HEADS * HEAD_DIM), lambda b, i: (b, i, 0)),
        out_shape=jax.ShapeDtypeStruct((B, T, N_HEADS * HEAD_DIM), F32),
        scratch_shapes=[pltpu.VMEM((N_KV, nc // 2, W), F32), pltpu.VMEM((N_KV, T, W), F32),
                        pltpu.VMEM((N_KV, NSA_WINDOW + 2 * TQ, W), F32)],
        compiler_params=_cp(("arbitrary", "arbitrary")),
        name="nsa_prompt_attn",
    )(q, kvs, kvs_pt, kvw, kvw_pt, kc, vct, gates, btiles)


def _swa_prompt_body(q_ref, kv_ref, vt_ref, sink_ref, tiles_ref, o_ref, s_ref):
    i = pl.program_id(1)
    qpads = _padded_queries(q_ref[0] * SCALE)
    qgs = [jnp.concatenate(qpads[GROUP * k:GROUP * (k + 1)], axis=0) for k in range(N_KV)]
    m_init = [jnp.broadcast_to(sink_ref[k], (8, W_LANES)) for k in range(N_KV)]
    m_rows, res = _window_branch(qgs, kv_ref, vt_ref, s_ref, tiles_ref, i, SWA_WINDOW // TQ, m_init)
    _store_heads(o_ref, [_normalise(m_rows[k], *res[k], extra=jnp.exp(sink_ref[k] - m_rows[k])) for k in range(N_KV)])


def _swa_prompt_attn(q, kv, kv_pt, sink_rows, btiles):
    B, T, _ = q.shape
    W = GROUP * TQ
    n_tiles = btiles.shape[0]
    return pl.pallas_call(
        _swa_prompt_body,
        grid=(B, T // TQ),
        in_specs=[pl.BlockSpec((1, TQ, N_HEADS * HEAD_DIM), lambda b, i: (b, i, 0)),
                  pl.BlockSpec((1, T, KV_COLS), lambda b, i: (b, 0, 0)),
                  pl.BlockSpec((1, T // PAGE_SIZE, KV_COLS, PAGE_SIZE), lambda b, i: (b, 0, 0, 0)),
                  pl.BlockSpec((N_KV, 1, W), lambda b, i: (0, 0, 0)),
                  pl.BlockSpec((n_tiles, N_KV, TQ, W), lambda b, i: (0, 0, 0, 0))],
        out_specs=pl.BlockSpec((1, TQ, N_HEADS * HEAD_DIM), lambda b, i: (b, i, 0)),
        out_shape=jax.ShapeDtypeStruct((B, T, N_HEADS * HEAD_DIM), F32),
        scratch_shapes=[pltpu.VMEM((N_KV, SWA_WINDOW + TQ, W), F32)],
        compiler_params=_cp(("arbitrary", "arbitrary")),
        name="swa_prompt_attn",
    )(q, kv, kv_pt, sink_rows, btiles)


def _rglru_body(xb_ref, gb_ref, prev0_ref, h0_ref, cw_ref, cb_ref, wr_ref, br_ref, wi_ref, bi_ref, lam_ref,
                y_ref, buf_ref, hl_ref, prev_ref, h_ref, a_ref, u_ref, *, tt):
    i = pl.program_id(0)

    @pl.when(i == 0)
    def _():
        prev_ref[...] = prev0_ref[...]
        h_ref[...] = h0_ref[...]

    x = xb_ref[...]
    B, W = x.shape[1], x.shape[2]
    xp = jnp.concatenate([prev_ref[...], x], axis=0)
    xc = xp[0:tt] * cw_ref[0:1, :]
    for j in range(1, LRU_CONV):
        xc = xc + xp[j:j + tt] * cw_ref[j:j + 1, :]
    xc = xc + cb_ref[...]
    prev_ref[...] = xp[tt:tt + LRU_CONV - 1]
    xf = xc.reshape(tt * B, W)
    xh = xf.astype(BF16)
    r = jax.nn.sigmoid(_dot(xh, wr_ref[...]) + br_ref[...])
    ig = jax.nn.sigmoid(_dot(xh, wi_ref[...]) + bi_ref[...])
    log_a = -LRU_C * r * _softplus(-lam_ref[...])
    a_ref[...] = jnp.exp(log_a).reshape(tt, B, W)
    u_ref[...] = (jnp.sqrt(-_expm1(2.0 * log_a)) * (ig * xf)).reshape(tt, B, W)

    def step(t, h):
        h = a_ref[t] * h + u_ref[t]
        y_ref[t] = h
        return h

    h = lax.fori_loop(0, tt, step, h_ref[...])
    h_ref[...] = h
    y_ref[...] = y_ref[...] * _gelu(gb_ref[...])
    buf_ref[...] = prev_ref[...]
    hl_ref[...] = h


def _rglru(xb, gb, prev0, h0, prm, tt):
    T, B, W = xb.shape
    full2 = lambda shp: pl.BlockSpec(shp, lambda i: (0, 0))
    blk = pl.BlockSpec((tt, B, W), lambda i: (i, 0, 0))
    return pl.pallas_call(
        functools.partial(_rglru_body, tt=tt),
        grid=(T // tt,),
        in_specs=[blk, blk, pl.BlockSpec((LRU_CONV - 1, B, W), lambda i: (0, 0, 0)), full2((B, W)),
                  full2((LRU_CONV, W)), full2((1, W)), full2((W, W)), full2((1, W)), full2((W, W)), full2((1, W)),
                  full2((1, W))],
        out_specs=[blk, pl.BlockSpec((LRU_CONV - 1, B, W), lambda i: (0, 0, 0)), full2((B, W))],
        out_shape=[jax.ShapeDtypeStruct((T, B, W), F32), jax.ShapeDtypeStruct((LRU_CONV - 1, B, W), F32),
                   jax.ShapeDtypeStruct((B, W), F32)],
        scratch_shapes=[pltpu.VMEM((LRU_CONV - 1, B, W), F32), pltpu.VMEM((B, W), F32),
                        pltpu.VMEM((tt, B, W), F32), pltpu.VMEM((tt, B, W), F32)],
        compiler_params=_cp(("arbitrary",)),
        name="rglru",
    )(xb, gb, prev0, h0, prm['conv_w'], prm['conv_b'], prm['w_r'], prm['b_r'], prm['w_i'], prm['b_i'], prm['lam'])


def _block_diag(w):
    n, d, e = w.shape
    return jnp.einsum('nde,nm->ndme', w, jnp.eye(n, dtype=w.dtype)).reshape(n * d, n * e)


S5_CHUNKS = 4
S5_CH_IN = S5_WIDTH // S5_CHUNKS
S5_CH_ST = S5_GROUPS * S5_STATE // S5_CHUNKS
S5_NSTATE = S5_GROUPS * S5_STATE


def _s5_disc(lr, li, dt):
    mag = jnp.exp(lr * dt)
    ab_re, ab_im = mag * jnp.cos(li * dt), mag * jnp.sin(li * dt)
    den = lr * lr + li * li
    f_re = ((ab_re - 1.0) * lr + ab_im * li) / den
    f_im = (ab_im * lr - (ab_re - 1.0) * li) / den
    return ab_re, ab_im, f_re, f_im


def _s5_param_body(lr_ref, li_ref, ldt_ref, lre_ref, lie_ref, bre_ref, bim_ref, are_ref, aim_ref, bbre_ref, bbim_ref):
    dt = jnp.exp(ldt_ref[...])
    ab_re, ab_im, _, _ = _s5_disc(lr_ref[...], li_ref[...], dt)
    are_ref[...] = ab_re
    aim_ref[...] = ab_im
    _, _, f_re, f_im = _s5_disc(lre_ref[...], lie_ref[...], dt)
    br, bim = bre_ref[...], bim_ref[...]
    bbre_ref[...] = f_re * br - f_im * bim
    bbim_ref[...] = f_re * bim + f_im * br


def _s5_params(lam_re, lam_im, log_dt, b_re, b_im):
    G, P, C = b_re.shape
    rep = lambda a: jnp.repeat(a, C, axis=1)
    shapes = [jax.ShapeDtypeStruct((G, P), F32)] * 2 + [jax.ShapeDtypeStruct((G, P * C), F32)] * 2
    return pl.pallas_call(_s5_param_body, out_shape=shapes, name="s5_discretise")(
        lam_re, lam_im, log_dt.reshape(G, 1), rep(lam_re), rep(lam_im), b_re.reshape(G, P * C), b_im.reshape(G, P * C))


def _s5_body(u_ref, h0r_ref, h0i_ref, are_ref, aim_ref, bre_ref, bim_ref, cre_ref, cim_ref, d_ref, wg_ref, bg_ref,
             y_ref, sr_ref, si_ref, hr_ref, hi_ref, xr_ref, xi_ref, sr_st_ref, si_st_ref, *, tt):
    i = pl.program_id(0)

    @pl.when(i == 0)
    def _():
        hr_ref[...] = h0r_ref[...]
        hi_ref[...] = h0i_ref[...]

    u3 = u_ref[...]
    B, W = u3.shape[1], u3.shape[2]
    uf = u3.reshape(tt * B, W)
    y_parts = []
    for ck in range(S5_CHUNKS):
        uc = uf[:, ck * S5_CH_IN:(ck + 1) * S5_CH_IN].astype(BF16)
        xr_ref[...] = _dot(uc, bre_ref[ck]).reshape(tt, B, S5_CH_ST)
        xi_ref[...] = _dot(uc, bim_ref[ck]).reshape(tt, B, S5_CH_ST)
        lanes = slice(ck * S5_CH_ST, (ck + 1) * S5_CH_ST)
        a_re = jnp.broadcast_to(are_ref[:, lanes], (B, S5_CH_ST))
        a_im = jnp.broadcast_to(aim_ref[:, lanes], (B, S5_CH_ST))

        def step(t, h):
            h_re, h_im = h
            n_re = a_re * h_re - a_im * h_im + xr_ref[t]
            n_im = a_re * h_im + a_im * h_re + xi_ref[t]
            sr_st_ref[t] = n_re
            si_st_ref[t] = n_im
            return n_re, n_im

        h_re, h_im = lax.fori_loop(0, tt, step, (hr_ref[:, lanes], hi_ref[:, lanes]))
        hr_ref[:, lanes] = h_re
        hi_ref[:, lanes] = h_im
        hre = sr_st_ref[...].reshape(tt * B, S5_CH_ST).astype(BF16)
        him = si_st_ref[...].reshape(tt * B, S5_CH_ST).astype(BF16)
        y_parts.append(_dot(hre, cre_ref[ck]) - _dot(him, cim_ref[ck]))
    y = jnp.concatenate(y_parts, axis=1) + d_ref[...] * uf
    g = _gelu(y)
    out = g * jax.nn.sigmoid(_dot(g.astype(BF16), wg_ref[...]) + bg_ref[...])
    y_ref[...] = out.reshape(tt, B, W)
    sr_ref[...] = hr_ref[...]
    si_ref[...] = hi_ref[...]


def _s5(u, h0_re, h0_im, prm, tt):
    T, B, W = u.shape
    full2 = lambda shp: pl.BlockSpec(shp, lambda i: (0, 0))
    full3 = lambda shp: pl.BlockSpec(shp, lambda i: (0, 0, 0))
    blk = pl.BlockSpec((tt, B, W), lambda i: (i, 0, 0))
    st = jax.ShapeDtypeStruct((B, S5_NSTATE), F32)
    return pl.pallas_call(
        functools.partial(_s5_body, tt=tt),
        grid=(T // tt,),
        in_specs=[blk, full2((B, S5_NSTATE)), full2((B, S5_NSTATE)), full2((1, S5_NSTATE)), full2((1, S5_NSTATE)),
                  full3((S5_CHUNKS, S5_CH_IN, S5_CH_ST)), full3((S5_CHUNKS, S5_CH_IN, S5_CH_ST)),
                  full3((S5_CHUNKS, S5_CH_ST, S5_CH_IN)), full3((S5_CHUNKS, S5_CH_ST, S5_CH_IN)),
                  full2((1, W)), full2((W, W)), full2((1, W))],
        out_specs=[blk, full2((B, S5_NSTATE)), full2((B, S5_NSTATE))],
        out_shape=[jax.ShapeDtypeStruct((T, B, W), F32), st, st],
        scratch_shapes=[pltpu.VMEM((B, S5_NSTATE), F32), pltpu.VMEM((B, S5_NSTATE), F32),
                        pltpu.VMEM((tt, B, S5_CH_ST), F32), pltpu.VMEM((tt, B, S5_CH_ST), F32),
                        pltpu.VMEM((tt, B, S5_CH_ST), F32), pltpu.VMEM((tt, B, S5_CH_ST), F32)],
        compiler_params=_cp(("arbitrary",)),
        name="s5",
    )(u, h0_re, h0_im, prm['a_re'], prm['a_im'], prm['b_re'], prm['b_im'], prm['c_re'], prm['c_im'],
      prm['d'], prm['w_glu'], prm['b_glu'])


def _s5_prepare(lam_re, lam_im, log_dt, b_re, b_im, c_re, c_im, d, w_glu, b_glu):
    a_re, a_im, bb_re, bb_im = _s5_params(lam_re, lam_im, log_dt, b_re, b_im)
    gpc = S5_GROUPS // S5_CHUNKS
    eye = jnp.eye(gpc, dtype=F32)

    def in_mat(bb):
        bb = bb.reshape(S5_CHUNKS, gpc, S5_STATE, S5_GROUP_CH)
        return jnp.einsum('kgpc,gh->kgchp', bb, eye).reshape(S5_CHUNKS, S5_CH_IN, S5_CH_ST).astype(BF16)

    def out_mat(c):
        c = c.reshape(S5_CHUNKS, gpc, S5_GROUP_CH, S5_STATE)
        return jnp.einsum('kgcp,gh->kgphc', c, eye).reshape(S5_CHUNKS, S5_CH_ST, S5_CH_IN).astype(BF16)

    return {'a_re': a_re.reshape(1, S5_NSTATE), 'a_im': a_im.reshape(1, S5_NSTATE),
            'b_re': in_mat(bb_re), 'b_im': in_mat(bb_im), 'c_re': out_mat(c_re), 'c_im': out_mat(c_im),
            'd': d.reshape(1, S5_WIDTH), 'w_glu': w_glu.astype(BF16), 'b_glu': b_glu.reshape(1, S5_WIDTH)}


FF_CHUNK = 896


def _ffn_up_body(x_ref, sc_ref, sh_ref, w_ref, cw_ref, cb_ref, prev0_ref, a_ref, buf_ref, prev_ref, *, tt):
    i = pl.program_id(0)

    @pl.when(i == 0)
    def _():
        prev_ref[...] = prev0_ref[...]

    x = x_ref[...]
    B, D = x.shape[1], x.shape[2]
    h = (x * (1.0 + sc_ref[...]) + sh_ref[...]).reshape(tt * B, D).astype(BF16)
    for c0 in range(0, D_FF, FF_CHUNK):
        cols = slice(c0, c0 + FF_CHUNK)
        g = _dot(h, w_ref[:, cols]).reshape(tt, B, FF_CHUNK)
        v = _dot(h, w_ref[:, D_FF + c0:D_FF + c0 + FF_CHUNK]).reshape(tt, B, FF_CHUNK)
        gp = jnp.concatenate([prev_ref[:, :, cols], g], axis=0)
        y = gp[0:tt] * cw_ref[0:1, cols]
        for j in range(1, FFN_CONV):
            y = y + gp[j:j + tt] * cw_ref[j:j + 1, cols]
        y = y + cb_ref[:, cols]
        prev_ref[:, :, cols] = gp[tt:tt + FFN_CONV - 1]
        a_ref[:, :, cols] = (_gelu(y) * v).astype(BF16)
    buf_ref[...] = prev_ref[...]


def _ffn_up(x, sc, sh, w_up, conv_w, conv_b, prev0, tt):
    T, B, D = x.shape
    full2 = lambda shp: pl.BlockSpec(shp, lambda i: (0, 0))
    return pl.pallas_call(
        functools.partial(_ffn_up_body, tt=tt),
        grid=(T // tt,),
        in_specs=[pl.BlockSpec((tt, B, D), lambda i: (i, 0, 0)), full2((B, D)), full2((B, D)),
                  full2((D, 2 * D_FF)), full2((FFN_CONV, D_FF)), full2((1, D_FF)),
                  pl.BlockSpec((FFN_CONV - 1, B, D_FF), lambda i: (0, 0, 0))],
        out_specs=[pl.BlockSpec((tt, B, D_FF), lambda i: (i, 0, 0)),
                   pl.BlockSpec((FFN_CONV - 1, B, D_FF), lambda i: (0, 0, 0))],
        out_shape=[jax.ShapeDtypeStruct((T, B, D_FF), BF16), jax.ShapeDtypeStruct((FFN_CONV - 1, B, D_FF), F32)],
        scratch_shapes=[pltpu.VMEM((FFN_CONV - 1, B, D_FF), F32)],
        compiler_params=_cp(("arbitrary",)),
        name="ffn_up_conv",
    )(x, sc, sh, w_up, conv_w, conv_b.reshape(1, D_FF), prev0)


N_PAST_PAGES = PAST_LEN // PAGE_SIZE
N_CAND = 3 * N_PAST_PAGES
N_PAST_CMP = PAST_LEN // CMP_BLOCK
N_CMP_ALL = N_PAST_CMP + 2
N_PAST_SEL = PAST_LEN // SEL_BLOCK


def _softmax_rows(s, mask):
    if mask is not None:
        s = jnp.where(mask, s, NEG)
    m = jnp.max(s, axis=1, keepdims=True)
    e = jnp.exp(s - m)
    if mask is not None:
        e = jnp.where(mask, e, 0.0)
    den = jnp.sum(e, axis=1, keepdims=True)
    return e / jnp.where(den > 0, den, 1.0)


def _cmp_sample_body(q_ref, kcv_ref, tail_ref):
    q = (q_ref[0] * SCALE).astype(BF16)
    n_pg = kcv_ref.shape[2]
    parts = []
    for blk in range(BLK_PER_PAGE):
        cols = slice(blk * HEAD_DIM, (blk + 1) * HEAD_DIM)
        parts.append(_by_group(_dot_nt(q, kcv_ref[0, 0, :, cols].astype(BF16)), _dot_nt(q, kcv_ref[1, 0, :, cols].astype(BF16))))
    parts.append(_by_group(_dot_nt(q, tail_ref[0, :, 0:HEAD_DIM].astype(BF16)),
                           _dot_nt(q, tail_ref[0, :, HEAD_DIM:2 * HEAD_DIM].astype(BF16))))
    s = jnp.concatenate(parts, axis=1)
    lane = lax.broadcasted_iota(jnp.int32, s.shape, 1)
    idx = jnp.where(lane < BLK_PER_PAGE * n_pg, BLK_PER_PAGE * (lane % n_pg) + lane // n_pg, lane)
    vis = (idx < N_CMP_ALL) & ((idx + 1) * CMP_BLOCK - 1 <= PAST_LEN)
    p = _softmax_rows(s, vis)
    pb = p.astype(BF16)
    o = jnp.zeros((N_HEADS, HEAD_DIM), F32)
    for blk in range(BLK_PER_PAGE):
        cols = slice(blk * HEAD_DIM, (blk + 1) * HEAD_DIM)
        pp = pb[:, blk * n_pg:(blk + 1) * n_pg]
        o = o + _by_group(_dot(pp, kcv_ref[2, 0, :, cols].astype(BF16)), _dot(pp, kcv_ref[3, 0, :, cols].astype(BF16)))
    pt = pb[:, BLK_PER_PAGE * n_pg:]
    o = o + _by_group(_dot(pt, tail_ref[0, :, 2 * HEAD_DIM:3 * HEAD_DIM].astype(BF16)),
                      _dot(pt, tail_ref[0, :, 3 * HEAD_DIM:4 * HEAD_DIM].astype(BF16)))
    row = lax.broadcasted_iota(jnp.int32, s.shape, 0)
    g0 = jnp.sum(jnp.where(row < GROUP, p, 0.0), axis=0, keepdims=True)
    g1 = jnp.sum(jnp.where(row >= GROUP, p, 0.0), axis=0, keepdims=True)
    gs = jnp.concatenate([g0, g1], axis=0)
    gt = jnp.sum(gs[:, 4 * n_pg:], axis=1, keepdims=True)
    first = lax.broadcasted_iota(jnp.int32, (N_KV, n_pg), 1) == 0
    imp = jnp.concatenate([gs[:, 0:n_pg] + gs[:, n_pg:2 * n_pg], gs[:, 2 * n_pg:3 * n_pg] + gs[:, 3 * n_pg:4 * n_pg],
                           jnp.where(first, gt, 0.0)], axis=1)
    return o, imp


SAMPLE_REQS = 4


def _cmp_sample_batch(q_ref, kcv_ref, tail_ref, o_ref, imp_ref, *, R):
    res = [_cmp_sample_body(q_ref.at[pl.ds(r, 1)], kcv_ref.at[:, pl.ds(r, 1)], tail_ref.at[pl.ds(r, 1)]) for r in range(R)]
    for r in range(R):
        o_ref[r], imp_ref[r] = res[r]


def _cmp_sample(q, kcv_pages, tail):
    B = q.shape[0]
    R = math.gcd(B, SAMPLE_REQS)
    n_pg = kcv_pages.shape[2]
    return pl.pallas_call(
        functools.partial(_cmp_sample_batch, R=R),
        grid=(B // R,),
        in_specs=[pl.BlockSpec((R, N_HEADS, HEAD_DIM), lambda b: (b, 0, 0)),
                  pl.BlockSpec((2 * N_KV, R, n_pg, BLK_PER_PAGE * HEAD_DIM), lambda b: (0, b, 0, 0)),
                  pl.BlockSpec((R, n_pg, KV_COLS), lambda b: (b, 0, 0))],
        out_specs=[pl.BlockSpec((R, N_HEADS, HEAD_DIM), lambda b: (b, 0, 0)),
                   pl.BlockSpec((R, N_KV, 3 * n_pg), lambda b: (b, 0, 0))],
        out_shape=[jax.ShapeDtypeStruct((B, N_HEADS, HEAD_DIM), F32), jax.ShapeDtypeStruct((B, N_KV, 3 * n_pg), F32)],
        compiler_params=_cp(("arbitrary",)),
        name="nsa_sample_cmp",
    )(q, kcv_pages, tail)


def _topk_sample_body(imp_ref, idx_ref):
    imp = imp_ref[...]
    r = lax.broadcasted_iota(jnp.int32, imp.shape, 0)
    n_pg = N_PAST_PAGES
    big = 4 * N_PAST_SEL
    j = jnp.where(r < n_pg, 2 * r, jnp.where(r < 2 * n_pg, 2 * (r - n_pg) + 1, jnp.where(r == 2 * n_pg, N_PAST_SEL, big)))
    cur = N_PAST_SEL
    forced = (j == 0) | (j == cur) | (j == cur - 1)
    score = jnp.where(forced, FORCE_SCORE, jnp.where(j <= cur, imp, -jnp.inf))
    rows = []
    for _ in range(N_SEL):
        top = jnp.max(score, axis=0, keepdims=True)
        idx = jnp.min(jnp.where(score == top, j, big), axis=0, keepdims=True)
        rows.append(idx)
        score = jnp.where(j == idx, -jnp.inf, score)
    idx_ref[...] = jnp.concatenate(rows, axis=0)


def _topk_sample(imp_t):
    return pl.pallas_call(
        _topk_sample_body,
        out_shape=jax.ShapeDtypeStruct((N_SEL, imp_t.shape[1]), jnp.int32),
        name="nsa_sample_topk",
    )(imp_t)


def _shift_in(win_ref, new_ref, width):
    lane = lax.broadcasted_iota(jnp.int32, (KV_COLS, width), 1)
    return jnp.where(lane == width - 1, new_ref[0], pltpu.roll(win_ref[0], width - 1, axis=1))


def _by_group(a0, a1):
    row = lax.broadcasted_iota(jnp.int32, a0.shape, 0)
    return jnp.where(row < GROUP, a0, a1)


def _window_scores(q, buf):
    return _by_group(_dot(q, buf[0:HEAD_DIM].astype(BF16)), _dot(q, buf[HEAD_DIM:2 * HEAD_DIM].astype(BF16)))


def _window_values(p, buf):
    pb = p.astype(BF16)
    return _by_group(_dot_nt(pb, buf[2 * HEAD_DIM:3 * HEAD_DIM].astype(BF16)),
                     _dot_nt(pb, buf[3 * HEAD_DIM:4 * HEAD_DIM].astype(BF16)))


def _selwin_sample_body(idx_ref, page_ref, pool_ref, q_ref, newrow_ref, win_ref, newwin_ref, bsel_ref, bwin_ref, b0_ref,
                        gate_ref, ocmp_ref, o_ref, winout_ref, g_ref, bias_ref, sem, *, R):
    step = pl.program_id(0)
    nsteps = pl.num_programs(0)
    per_req = N_KV * N_SEL
    nl = N_SEL * PAGE_SIZE

    def page_copy(st, slot, r, k, n):
        page = page_ref[(st * R + r) * per_req + k * N_SEL + n]
        return pltpu.make_async_copy(pool_ref.at[page, :, k],
                                     g_ref.at[slot, r, k, :, :, pl.ds(n * PAGE_SIZE, PAGE_SIZE)],
                                     sem.at[slot, (r * N_KV + k) * N_SEL + n])

    def start_all(st, slot):
        for r in range(R):
            for k in range(N_KV):
                for n in range(N_SEL):
                    page_copy(st, slot, r, k, n).start(priority=n % 2)

    @pl.when(step == 0)
    def _():
        start_all(0, 0)

    @pl.when(step + 1 < nsteps)
    def _():
        start_all(step + 1, (step + 1) % 2)

    slot = step % 2
    for r in range(R):
        for k in range(N_KV):
            for n in range(N_SEL):
                page_copy(step, slot, r, k, n).wait()

    lane = lax.broadcasted_iota(jnp.int32, (1, nl), 1)
    row = lax.broadcasted_iota(jnp.int32, (N_HEADS, nl), 0)
    row1 = lax.broadcasted_iota(jnp.int32, (N_HEADS, 1), 0)
    results = []
    for r in range(R):
        q = (q_ref[r] * SCALE).astype(BF16)
        scores, masks, vts, has_new = [], [], [], []
        for k in range(N_KV):
            nvec = jnp.zeros((1, nl), jnp.int32)
            seen = jnp.int32(0)
            for n in range(N_SEL):
                blk = idx_ref[(step * R + r) * per_req + k * N_SEL + n]
                nvec = jnp.where(lane // PAGE_SIZE == n, blk, nvec)
                seen = jnp.maximum(seen, jnp.where(blk == N_PAST_SEL, 1, 0))
                bias_ref[r, k, :, n * PAGE_SIZE:(n + 1) * PAGE_SIZE] = bsel_ref[blk]
            has_new.append(seen)
            half = jnp.minimum(nvec, N_PAST_SEL - 1) % (PAGE_SIZE // SEL_BLOCK)
            in_half = (lane % PAGE_SIZE) // SEL_BLOCK == half
            masks.append(jnp.where(in_half & (nvec < N_PAST_SEL), 1.0, 0.0))
            scores.append(_dot(q, g_ref[slot, r, k, 0].astype(BF16)) + bias_ref[r, k])
            vts.append(g_ref[slot, r, k, 1].astype(BF16))
        mask = jnp.where(row < GROUP, masks[0], masks[1]) > 0.5
        s = jnp.where(mask, jnp.where(row < GROUP, scores[0], scores[1]), NEG)
        new = newrow_ref[r].astype(BF16).astype(F32)
        own = lambda a, b: jnp.where(row1 < GROUP, jnp.broadcast_to(a, (N_HEADS, HEAD_DIM)), jnp.broadcast_to(b, (N_HEADS, HEAD_DIM)))
        k_new = own(new[:, 0:HEAD_DIM], new[:, HEAD_DIM:2 * HEAD_DIM])
        v_new = own(new[:, 2 * HEAD_DIM:3 * HEAD_DIM], new[:, 3 * HEAD_DIM:4 * HEAD_DIM])
        valid_new = jnp.where(row1 < GROUP, has_new[0], has_new[1]) > 0
        s_new = jnp.sum(q.astype(F32) * k_new, axis=1, keepdims=True) + b0_ref[...]
        s_new = jnp.where(valid_new, s_new, NEG)
        m = jnp.maximum(jnp.max(s, axis=1, keepdims=True), s_new)
        e = jnp.where(mask, jnp.exp(s - m), 0.0)
        e_new = jnp.where(valid_new, jnp.exp(s_new - m), 0.0)
        den = jnp.sum(e, axis=1, keepdims=True) + e_new
        den = jnp.where(den > 0, den, 1.0)
        pb = (e / den).astype(BF16)
        p_new = (e_new / den).astype(BF16).astype(F32)
        o_slc = _by_group(_dot_nt(pb, vts[0]), _dot_nt(pb, vts[1])) + p_new * v_new
        win = _shift_in(win_ref.at[pl.ds(r, 1)], newwin_ref.at[pl.ds(r, 1)], NSA_WINDOW)
        pw = _softmax_rows(_window_scores(q, win) + bwin_ref[...], None)
        o_win = _window_values(pw, win)
        g = jax.nn.sigmoid(gate_ref[r])
        results.append((win, g[:, 0:1] * ocmp_ref[r] + g[:, 1:2] * o_slc + g[:, 2:3] * o_win))
    for r in range(R):
        winout_ref[r], o_ref[r] = results[r]


def _selwin_sample(idx_flat, page_flat, pool5, q, new_row, win, new_win, bias_sel, bias_win, bias0, gates, o_cmp):
    B = q.shape[0]
    R = math.gcd(B, SAMPLE_REQS)
    nl = N_SEL * PAGE_SIZE
    spec3 = lambda shp: pl.BlockSpec(shp, lambda b, *_: (b, 0, 0))
    grid_spec = pltpu.PrefetchScalarGridSpec(
        num_scalar_prefetch=2,
        grid=(B // R,),
        in_specs=[pl.BlockSpec(memory_space=pl.ANY),
                  spec3((R, N_HEADS, HEAD_DIM)), spec3((R, 1, KV_COLS)), spec3((R, KV_COLS, NSA_WINDOW)),
                  spec3((R, KV_COLS, 1)),
                  pl.BlockSpec((N_PAST_SEL + 1, N_HEADS, PAGE_SIZE), lambda b, *_: (0, 0, 0)),
                  pl.BlockSpec((N_HEADS, NSA_WINDOW), lambda b, *_: (0, 0)),
                  pl.BlockSpec((N_HEADS, 1), lambda b, *_: (0, 0)),
                  spec3((R, N_HEADS, 3)), spec3((R, N_HEADS, HEAD_DIM))],
        out_specs=[spec3((R, N_HEADS, HEAD_DIM)), spec3((R, KV_COLS, NSA_WINDOW))],
        scratch_shapes=[pltpu.VMEM((2, R, N_KV, 2, HEAD_DIM, nl), F32), pltpu.VMEM((R, N_KV, N_HEADS, nl), F32),
                        pltpu.SemaphoreType.DMA((2, R * N_KV * N_SEL))],
    )
    return pl.pallas_call(
        functools.partial(_selwin_sample_body, R=R),
        grid_spec=grid_spec,
        out_shape=[jax.ShapeDtypeStruct((B, N_HEADS, HEAD_DIM), F32), jax.ShapeDtypeStruct((B, KV_COLS, NSA_WINDOW), F32)],
        compiler_params=_cp(("arbitrary",)),
        name="nsa_sample_sel_win",
    )(idx_flat, page_flat, pool5, q, new_row, win, new_win, bias_sel, bias_win, bias0, gates, o_cmp)


def _swa_sample_one(q, win, bias, sink):
    q = (q * SCALE).astype(BF16)
    s = _window_scores(q, win) + bias
    m = jnp.maximum(jnp.max(s, axis=1, keepdims=True), sink)
    e = jnp.exp(s - m)
    den = jnp.sum(e, axis=1, keepdims=True) + jnp.exp(sink - m)
    p = e / jnp.where(den > 0, den, 1.0)
    return _window_values(p, win)


def _swa_sample_batch(q_ref, win_ref, new_ref, bias_ref, sink_ref, o_ref, winout_ref, *, R):
    wins = [_shift_in(win_ref.at[pl.ds(r, 1)], new_ref.at[pl.ds(r, 1)], SWA_WINDOW) for r in range(R)]
    outs = [_swa_sample_one(q_ref[r], wins[r], bias_ref[...], sink_ref[...]) for r in range(R)]
    for r in range(R):
        winout_ref[r] = wins[r]
        o_ref[r] = outs[r]


def _swa_sample(q, win, new_col, bias, sinks):
    B = q.shape[0]
    R = math.gcd(B, SAMPLE_REQS)
    spec3 = lambda shp: pl.BlockSpec(shp, lambda b: (b, 0, 0))
    return pl.pallas_call(
        functools.partial(_swa_sample_batch, R=R),
        grid=(B // R,),
        in_specs=[spec3((R, N_HEADS, HEAD_DIM)), spec3((R, KV_COLS, SWA_WINDOW)), spec3((R, KV_COLS, 1)),
                  pl.BlockSpec((N_HEADS, SWA_WINDOW), lambda b: (0, 0)),
                  pl.BlockSpec((N_HEADS, 1), lambda b: (0, 0))],
        out_specs=[spec3((R, N_HEADS, HEAD_DIM)), spec3((R, KV_COLS, SWA_WINDOW))],
        out_shape=[jax.ShapeDtypeStruct((B, N_HEADS, HEAD_DIM), F32), jax.ShapeDtypeStruct((B, KV_COLS, SWA_WINDOW), F32)],
        compiler_params=_cp(("arbitrary",)),
        name="swa_sample",
    )(q, win, new_col, bias, sinks)


def _window_to_device_view(cache, width):
    return jnp.transpose(cache, (0, 1, 3, 4, 5, 2)).reshape(cache.shape[0], KV_COLS, width)


def _window_from_device_view(buf, width):
    return jnp.transpose(buf.reshape(buf.shape[0], 1, 2, N_KV, HEAD_DIM, width), (0, 1, 5, 2, 3, 4))


def _split_mod(m, rows):
    sh, sc, gt = m[rows, 0:D_MODEL], m[rows, D_MODEL:2 * D_MODEL], m[rows, 2 * D_MODEL:]
    return sh, sc, gt


def _ffn(x_tm, mod, rows, B, T, w_up, conv_w, conv_b, w_down, prev0, ln_g, ln_b, tt, tm):
    sh, sc, gt = _split_mod(mod, rows)
    a, buf = _ffn_up(x_tm.reshape(T, B, D_MODEL), sc, sh, w_up, conv_w, conv_b, prev0, tt)
    gate = gt[:, None, :] if T > 1 else gt[None]
    nb, nt = (B, T) if T > 1 else (1, B)
    x = _mm_res_ln([a.reshape(nt, nb * D_FF)], [True], [w_down], x_tm.reshape(nt, nb * D_MODEL), True, gate,
                   ln_g, ln_b, False, nb, nt, tm)
    return x, buf


def kernel(x_prompt, x_sample, cache_kv_cmp, cache_kv_slc, cache_kv_win, state_lru_conv, state_lru_h, cache_kv_swa, state_s5_re, state_s5_im, state_ffn_conv, page_table, c_prompt, c_sample, rel_bias, w_mod, b_mod, ln_g, ln_b, w_in_ab, w_out_ab, nsa_cmp_pe, nsa_cmp_w, lru_conv_w, lru_conv_b, lru_w_r, lru_b_r, lru_w_i, lru_b_i, lru_lambda, w_in_cd, w_out_cd, swa_sinks, s5_lambda_re, s5_lambda_im, s5_log_dt, s5_b_re, s5_b_im, s5_c_re, s5_c_im, s5_d, s5_w_glu, s5_b_glu, w_ffn_up, w_ffn_conv, b_ffn_conv, w_ffn_down):
    assert N_AB == 1 and DEPTH == 2
    Bp, T, D = x_prompt.shape
    Bs = x_sample.shape[0]
    n_pool = cache_kv_cmp.shape[0]
    TM = 512 if T % 512 == 0 else T
    TT_REC = 64 if T % 64 == 0 else T
    TT_S5 = 32 if T % 32 == 0 else T
    TT_FF = 32 if T % 32 == 0 else T
    prow, srow = slice(0, Bp), slice(Bp, Bp + Bs)

    mod = _ada_mod_all(jnp.concatenate([c_prompt, c_sample], axis=0),
                       w_mod.reshape(2 * DEPTH, D, 3 * D), b_mod.reshape(2 * DEPTH, 1, 3 * D))
    btiles = _bias_tiles(rel_bias)
    xs = x_sample.reshape(1, Bs, D)

    w = w_in_ab[0].astype(BF16)
    o = np.cumsum([0, N_HEADS * HEAD_DIM, KV_COLS, KV_COLS, KV_COLS, 3 * N_HEADS, LRU_WIDTH, LRU_WIDTH])
    w_gates = jnp.pad(w[:, o[4]:o[5]], ((0, 0), (0, LANES - 3 * N_HEADS)))
    ws = [w[:, o[0]:o[1]], w[:, o[1]:o[2]], w[:, o[2]:o[3]], w[:, o[3]:o[4]], w_gates, w[:, o[5]:o[6]], w[:, o[6]:o[7]]]
    kinds_p = [['bm'], ['pt'], ['bm', 'pt'], ['bm', 'pt'], ['bm'], ['tm'], ['tm']]
    kinds_s = [['bm'], ['bm'], ['bm'], ['bm'], ['bm'], ['tm'], ['tm']]
    w2, pe2 = _compress_pool_weights(nsa_cmp_w[0], nsa_cmp_pe[0])
    lru = {'conv_w': lru_conv_w[0], 'conv_b': lru_conv_b[0].reshape(1, -1),
           'w_r': _block_diag(lru_w_r[0]).astype(BF16), 'b_r': lru_b_r[0].reshape(1, -1),
           'w_i': _block_diag(lru_w_i[0]).astype(BF16), 'b_i': lru_b_i[0].reshape(1, -1),
           'lam': lru_lambda[0].reshape(1, -1)}
    w_out = w_out_ab[0].astype(BF16)
    w_out_parts = [w_out[0:N_HEADS * HEAD_DIM], w_out[N_HEADS * HEAD_DIM:]]

    shp, scp, gtp = _split_mod(mod[0], prow)
    q, kvc_pt, kvs, kvs_pt, kvw, kvw_pt, gates, xb, gb = _modmm(x_prompt, scp[:, None], shp[:, None],
                                                                list(zip(ws, kinds_p)), TM)
    nc = T // CMP_BLOCK
    npg = T // PAGE_SIZE
    kcv = _compress_pool(jnp.arange(Bp * npg, dtype=jnp.int32), kvc_pt.reshape(Bp * npg, KV_COLS, PAGE_SIZE), w2, pe2)
    kcv = _blocks_by_page(kcv, (Bp, npg)).reshape(Bp, nc, KV_COLS)
    kcv = jnp.concatenate([kcv[:, 0::2], kcv[:, 1::2]], axis=1)
    o_att = _nsa_prompt_attn(q, kvs, kvs_pt, kvw, kvw_pt, kcv[:, :, 0:LANES], jnp.swapaxes(kcv[:, :, LANES:], 1, 2),
                             gates, btiles)
    y_lru, p_lru_buf, p_lru_h = _rglru(xb.reshape(T, Bp, LRU_WIDTH), gb.reshape(T, Bp, LRU_WIDTH),
                                       jnp.zeros((LRU_CONV - 1, Bp, LRU_WIDTH), F32), jnp.zeros((Bp, LRU_WIDTH), F32),
                                       lru, TT_REC)
    xp1 = _mm_res_ln([o_att, y_lru.reshape(T, Bp * LRU_WIDTH)], [False, True], w_out_parts, x_prompt, False,
                     gtp[:, None], ln_g[0, 0], ln_b[0, 0], True, Bp, T, TM)
    paged_out = lambda a: jnp.transpose(a.reshape(Bp, npg, 1, 2, N_KV, HEAD_DIM, PAGE_SIZE), (0, 1, 6, 2, 3, 4, 5))
    p_kv_cmp, p_kv_slc = paged_out(kvc_pt), paged_out(kvs_pt)
    wk = min(NSA_WINDOW, T)
    p_kv_win = _window_from_device_view(
        jnp.swapaxes(kvw_pt[:, npg - wk // PAGE_SIZE:], 1, 2).reshape(Bp, KV_COLS, wk), wk)

    shs, scs, gts = _split_mod(mod[0], srow)
    q_s, kvc_s, kvs_s, kvw_s, gates_s, xb_s, gb_s = _modmm(xs, scs[None], shs[None], list(zip(ws, kinds_s)), Bs)
    q_s, kvc_s, kvs_s, kvw_s, gates_s = q_s[0], kvc_s[0], kvs_s[0], kvw_s[0], gates_s[0]
    pool_cmp_t = jnp.transpose(cache_kv_cmp, (0, 2, 3, 4, 5, 1)).reshape(n_pool, KV_COLS, PAGE_SIZE)
    pool_slc_t = jnp.transpose(cache_kv_slc, (0, 2, 3, 4, 5, 1)).reshape(n_pool, 2, N_KV, HEAD_DIM, PAGE_SIZE)
    n_pages = page_table.shape[1]
    kcv_past = _compress_pool(page_table.reshape(-1), pool_cmp_t, w2, pe2)
    kcv_past = kcv_past.reshape(2 * N_KV, Bs, n_pages, BLK_PER_PAGE * HEAD_DIM)
    tail_pages = jnp.pad(kvc_s[:, :, None], ((0, 0), (0, 0), (0, PAGE_SIZE - 1)))
    kcv_tail = _compress_pool(jnp.arange(Bs, dtype=jnp.int32), tail_pages, w2, pe2)
    kcv_tail = _blocks_by_page(kcv_tail, (Bs,))[:, 0:SEL_BLOCK // CMP_BLOCK]
    kcv_tail = jnp.pad(kcv_tail, ((0, 0), (0, N_PAST_PAGES - SEL_BLOCK // CMP_BLOCK), (0, 0)))
    q_s3 = q_s.reshape(Bs, N_HEADS, HEAD_DIM)
    o_cmp_s, imp_s = _cmp_sample(q_s3, kcv_past, kcv_tail)
    idx_t = _topk_sample(imp_s.reshape(Bs * N_KV, N_CAND).T)
    idx = idx_t.T.reshape(Bs, N_KV, N_SEL)
    per_page_s = PAGE_SIZE // SEL_BLOCK
    idx_p = jnp.minimum(idx, N_PAST_SEL - 1)
    pages = page_table[jnp.arange(Bs)[:, None, None], idx_p // per_page_s]
    blk_id = jnp.arange(N_PAST_SEL + 1, dtype=jnp.int32)
    kpos = blk_id[:, None] * SEL_BLOCK + jnp.arange(SEL_BLOCK, dtype=jnp.int32)[None, :]
    bias_half = jnp.swapaxes(_rel_bias_at(rel_bias, PAST_LEN - kpos), 1, 2)
    in_upper = (jnp.minimum(blk_id, N_PAST_SEL - 1) % per_page_s == 1)[:, None, None]
    zeros_half = jnp.zeros_like(bias_half)
    bias_sel = jnp.where(in_upper, jnp.concatenate([zeros_half, bias_half], axis=2),
                         jnp.concatenate([bias_half, zeros_half], axis=2))
    bias_win = _rel_bias_at(rel_bias, NSA_WINDOW - 1 - jnp.arange(NSA_WINDOW, dtype=jnp.int32)).T
    gates3 = jnp.swapaxes(gates_s[:, 0:3 * N_HEADS].reshape(Bs, 3, N_HEADS), 1, 2)
    o_att_s, win_new = _selwin_sample(idx.reshape(-1), pages.reshape(-1).astype(jnp.int32), pool_slc_t,
                                      q_s3, kvs_s[:, None, :],
                                      _window_to_device_view(cache_kv_win, NSA_WINDOW), kvw_s[:, :, None],
                                      bias_sel, bias_win, _rel_bias_at(rel_bias, jnp.zeros((1,), jnp.int32)).T,
                                      gates3, o_cmp_s)
    y_lru_s, s_lru_buf, s_lru_h = _rglru(xb_s.reshape(1, Bs, LRU_WIDTH), gb_s.reshape(1, Bs, LRU_WIDTH),
                                         jnp.swapaxes(state_lru_conv[:, 0], 0, 1), state_lru_h[:, 0], lru, 1)
    xs1 = _mm_res_ln([o_att_s.reshape(1, Bs, N_HEADS * HEAD_DIM), y_lru_s], [False, False], w_out_parts, xs, False,
                     gts[None], ln_g[0, 0], ln_b[0, 0], True, 1, Bs, Bs)

    ffn_w = [(w_ffn_up[li].astype(BF16), w_ffn_conv[li], b_ffn_conv[li], w_ffn_down[li].astype(BF16)) for li in range(DEPTH)]
    zero_ff = jnp.zeros((FFN_CONV - 1, Bp, D_FF), F32)
    xp2, p_ff0 = _ffn(xp1, mod[1], prow, Bp, T, *ffn_w[0], zero_ff, ln_g[0, 1], ln_b[0, 1], TT_FF, TM)
    xs2, s_ff0 = _ffn(xs1, mod[1], srow, Bs, 1, *ffn_w[0], jnp.swapaxes(state_ffn_conv[:, 0], 0, 1),
                      ln_g[0, 1], ln_b[0, 1], 1, Bs)

    w = w_in_cd[0].astype(BF16)
    ws = [w[:, 0:N_HEADS * HEAD_DIM], w[:, N_HEADS * HEAD_DIM:N_HEADS * HEAD_DIM + KV_COLS], w[:, N_HEADS * HEAD_DIM + KV_COLS:]]
    s5p = _s5_prepare(s5_lambda_re[0], s5_lambda_im[0], s5_log_dt[0], s5_b_re[0], s5_b_im[0], s5_c_re[0], s5_c_im[0],
                      s5_d[0], s5_w_glu[0], s5_b_glu[0])
    w_out = w_out_cd[0].astype(BF16)
    w_out_parts = [w_out[0:N_HEADS * HEAD_DIM], w_out[N_HEADS * HEAD_DIM:]]
    sink_rows = jnp.repeat(swa_sinks[0].reshape(N_KV, 1, GROUP), TQ, axis=2)

    shp, scp, gtp = _split_mod(mod[2], prow)
    q, kv, kv_pt, u = _modmm(xp2, scp[:, None], shp[:, None], list(zip(ws, [['bm'], ['bm', 'pt'], ['tm']])), TM)
    o_att = _swa_prompt_attn(q, kv, kv_pt, sink_rows, btiles)
    zst = jnp.zeros((Bp, S5_NSTATE), F32)
    y_s5, p_s5_re, p_s5_im = _s5(u.reshape(T, Bp, S5_WIDTH), zst, zst, s5p, TT_S5)
    xp3 = _mm_res_ln([o_att, y_s5.reshape(T, Bp * S5_WIDTH)], [False, True], w_out_parts, xp2, False,
                     gtp[:, None], ln_g[1, 0], ln_b[1, 0], True, Bp, T, TM)
    assert SWA_WINDOW == PAGE_SIZE and T >= SWA_WINDOW
    p_kv_swa = _window_from_device_view(kv_pt[:, npg - 1], SWA_WINDOW)

    shs, scs, gts = _split_mod(mod[2], srow)
    q_s, kv_s, u_s = _modmm(xs2, scs[None], shs[None], list(zip(ws, [['bm'], ['bm'], ['tm']])), Bs)
    bias_swa = _rel_bias_at(rel_bias, SWA_WINDOW - 1 - jnp.arange(SWA_WINDOW, dtype=jnp.int32)).T
    o_att_s, swa_new = _swa_sample(q_s[0].reshape(Bs, N_HEADS, HEAD_DIM), _window_to_device_view(cache_kv_swa, SWA_WINDOW),
                                   kv_s[0][:, :, None], bias_swa, swa_sinks[0].reshape(N_HEADS, 1))
    y_s5_s, s_s5_re, s_s5_im = _s5(u_s.reshape(1, Bs, S5_WIDTH), state_s5_re[:, 0].reshape(Bs, S5_NSTATE),
                                   state_s5_im[:, 0].reshape(Bs, S5_NSTATE), s5p, 1)
    xs3 = _mm_res_ln([o_att_s.reshape(1, Bs, N_HEADS * HEAD_DIM), y_s5_s], [False, False], w_out_parts, xs2, False,
                     gts[None], ln_g[1, 0], ln_b[1, 0], True, 1, Bs, Bs)

    xp4, p_ff1 = _ffn(xp3, mod[3], prow, Bp, T, *ffn_w[1], zero_ff, ln_g[1, 1], ln_b[1, 1], TT_FF, TM)
    xs4, s_ff1 = _ffn(xs3, mod[3], srow, Bs, 1, *ffn_w[1], jnp.swapaxes(state_ffn_conv[:, 1], 0, 1),
                      ln_g[1, 1], ln_b[1, 1], 1, Bs)

    kv6 = lambda a, n: a.reshape(n, 1, 1, 2, N_KV, HEAD_DIM)
    return (xp4, xs4.reshape(Bs, 1, D),
            p_kv_cmp, p_kv_slc, p_kv_win,
            jnp.swapaxes(p_lru_buf, 0, 1)[:, None], p_lru_h[:, None], p_kv_swa,
            p_s5_re.reshape(Bp, 1, S5_GROUPS, S5_STATE), p_s5_im.reshape(Bp, 1, S5_GROUPS, S5_STATE),
            jnp.stack([jnp.swapaxes(p_ff0, 0, 1), jnp.swapaxes(p_ff1, 0, 1)], axis=1),
            kv6(kvc_s, Bs), kv6(kvs_s, Bs), _window_from_device_view(win_new, NSA_WINDOW),
            jnp.swapaxes(s_lru_buf, 0, 1)[:, None], s_lru_h[:, None],
            _window_from_device_view(swa_new, SWA_WINDOW),
            s_s5_re.reshape(Bs, 1, S5_GROUPS, S5_STATE), s_s5_im.reshape(Bs, 1, S5_GROUPS, S5_STATE),
            jnp.stack([jnp.swapaxes(s_ff0, 0, 1), jnp.swapaxes(s_ff1, 0, 1)], axis=1))
```
